```python
import math
import jax, jax.numpy as jnp
from jax import lax
import numpy as np

D_MODEL = 2048
BATCH = 4
SEQ = 2048
DEPTH = 4

N_MIXERS = 3
N_META = 16
EPS = 1e-6
MLP_HIDDEN = 4 * D_MODEL
GDN_HEAD_DIM = 128
GDN_QK_HEADS = D_MODEL // 128
GDN_V_HEADS = 2 * GDN_QK_HEADS
GDN_QK_DIM = GDN_QK_HEADS * GDN_HEAD_DIM
GDN_V_DIM = GDN_V_HEADS * GDN_HEAD_DIM
GDN_CONV = 4
GDN_CHUNK = 64
GDN_IN = 2 * GDN_QK_DIM + 2 * GDN_V_DIM + 2 * GDN_V_HEADS
MLA_HEADS = D_MODEL // 128
MLA_Q_LORA = 3 * D_MODEL // 8
MLA_KV_LORA = D_MODEL // 4
MLA_NOPE = 128
MLA_ROPE = 64
MLA_V = 128
MLA_QK = MLA_NOPE + MLA_ROPE
MLA_IN = MLA_Q_LORA + MLA_KV_LORA + MLA_ROPE
ATTN_BLOCK = 128
ROPE_THETA = 10000.0
SSM_D_INNER = 2 * D_MODEL
SSM_HEAD_DIM = 64
SSM_HEADS = SSM_D_INNER // SSM_HEAD_DIM
SSM_GROUPS = 8
SSM_HPG = SSM_HEADS // SSM_GROUPS
SSM_STATE = 128
SSM_CONV = 4
SSM_CHUNK = 128
SSM_CONV_DIM = SSM_D_INNER + 2 * SSM_GROUPS * SSM_STATE
SSM_IN = SSM_D_INNER + SSM_CONV_DIM + SSM_HEADS
N_GDN = (DEPTH + 2) // 3
N_MLA = (DEPTH + 1) // 3
N_SSM = DEPTH // 3

kernel_name = 'hybrid_gdn_mla_ssd_trunk'

F32 = jnp.float32


def rms_norm(x, gain):
    xf = x.astype(F32)
    y = xf * lax.rsqrt(jnp.mean(xf * xf, axis=-1, keepdims=True) + EPS)
    return (y * gain.astype(F32)).astype(x.dtype)


def l2_normalize(x):
    xf = x.astype(F32)
    return (xf * lax.rsqrt(jnp.sum(xf * xf, axis=-1, keepdims=True) + EPS)).astype(x.dtype)


def causal_depthwise_conv(x, w, bias=None):
    width, ch = w.shape
    y = lax.conv_general_dilated(x, w[:, None, :], window_strides=(1,), padding=[(width - 1, 0)],
                                 dimension_numbers=('NWC', 'WIO', 'NWC'), feature_group_count=ch)
    if bias is not None:
        y = y + bias
    return y


def pad_time(t, n):
    return jnp.pad(t, [(0, 0), (n, 0)] + [(0, 0)] * (t.ndim - 2))


def gated_delta_rule_chunked(q, k, v, g, beta):
    dtype = v.dtype
    b, T, H, Dk = q.shape
    Dv = v.shape[-1]
    C = GDN_CHUNK
    n = T // C

    def chunks(t):
        t = t.astype(F32).reshape((b, n, C, H) + t.shape[3:])
        return jnp.moveaxis(t, 3, 2)

    q, k, v, g, beta = chunks(q), chunks(k), chunks(v), chunks(g), chunks(beta)
    G = jnp.cumsum(g, axis=-1)
    causal = jnp.tril(jnp.ones((C, C), bool))
    strict = jnp.tril(jnp.ones((C, C), bool), -1)
    decay = jnp.exp(jnp.where(causal, G[..., :, None] - G[..., None, :], -jnp.inf))
    kb = k * beta[..., None]
    A = jnp.where(strict, jnp.einsum('bnhid,bnhjd->bnhij', kb, k) * decay, 0.0)
    M = A + jnp.eye(C, dtype=F32)
    u = lax.linalg.triangular_solve(M, v * beta[..., None], left_side=True, lower=True, unit_diagonal=True)
    w = lax.linalg.triangular_solve(M, kb * jnp.exp(G)[..., None], left_side=True, lower=True, unit_diagonal=True)
    qk = jnp.einsum('bnhid,bnhjd->bnhij', q, k) * decay
    g_last = G[..., -1]
    k_dec = k * jnp.exp(g_last[..., None] - G)[..., None]
    q_dec = q * jnp.exp(G)[..., None]

    def step(S, xs):
        qd, qkc, uc, wc, kd, gl = xs
        v_new = uc - jnp.einsum('bhcd,bhde->bhce', wc, S)
        o = jnp.einsum('bhcd,bhde->bhce', qd, S) + jnp.einsum('bhij,bhje->bhie', qkc, v_new)
        S = S * jnp.exp(gl)[..., None, None] + jnp.einsum('bhcd,bhce->bhde', kd, v_new)
        return S, o

    S0 = jnp.zeros((b, H, Dk, Dv), F32)
    xs = tuple(jnp.moveaxis(t, 1, 0) for t in (q_dec, qk, u, w, k_dec, g_last))
    _, o = lax.scan(step, S0, xs)
    o = jnp.moveaxis(jnp.moveaxis(o, 0, 1), 3, 2).reshape(b, T, H, Dv)
    return o.astype(dtype)


def ssd_chunked(x, dt, A, Bm, Cm):
    dtype = x.dtype
    b, T, G, R, P = x.shape
    C = SSM_CHUNK
    n = T // C
    x = x.astype(F32).reshape(b, n, C, G, R, P)
    dt = dt.astype(F32).reshape(b, n, C, G, R)
    Bm = Bm.astype(F32).reshape(b, n, C, G, -1)
    Cm = Cm.astype(F32).reshape(b, n, C, G, -1)
    a_cum = jnp.cumsum(jnp.moveaxis(dt * A.astype(F32), 2, -1), axis=-1)
    causal = jnp.tril(jnp.ones((C, C), bool))
    Lmat = jnp.exp(jnp.where(causal, a_cum[..., :, None] - a_cum[..., None, :], -jnp.inf))
    xdt = x * dt[..., None]
    CB = jnp.einsum('bclgn,bcsgn->bcgls', Cm, Bm)
    scores = CB[:, :, :, None] * Lmat
    y_diag = jnp.einsum('bcghls,bcsghp->bclghp', scores, xdt)
    decay_states = jnp.exp(a_cum[..., -1:] - a_cum)
    states = jnp.einsum('bclgn,bcghl,bclghp->bcghpn', Bm, decay_states, xdt)
    chunk_decay = jnp.exp(a_cum[..., -1])

    def step(S, xs):
        st, dec = xs
        return S * dec[..., None, None] + st, S

    S0 = jnp.zeros((b, G, R, P, Bm.shape[-1]), F32)
    _, prev = lax.scan(step, S0, (jnp.moveaxis(states, 1, 0), jnp.moveaxis(chunk_decay, 1, 0)))
    prev = jnp.moveaxis(prev, 0, 1)
    y_off = jnp.einsum('bclgn,bcghpn,bcghl->bclghp', Cm, prev, jnp.exp(a_cum))
    return (y_diag + y_off).reshape(b, T, G, R, P).astype(dtype)


def causal_block_attention(q, k, v, scale):
    b, L, H, _ = q.shape
    Dv = v.shape[-1]
    kpos = jnp.arange(L)

    def attend(qb, qpos):
        s = jnp.einsum('bqhd,bkhd->bhqk', qb, k).astype(F32) * scale
        s = jnp.where(kpos[None, :] <= qpos[:, None], s, -jnp.inf)
        p = jax.nn.softmax(s, axis=-1).astype(v.dtype)
        return jnp.einsum('bhqk,bkhd->bqhd', p, v)

    o_meta = attend(q[:, :N_META], jnp.arange(N_META))
    nb = (L - N_META) // ATTN_BLOCK
    qr = jnp.moveaxis(q[:, N_META:].reshape(b, nb, ATTN_BLOCK, H, -1), 1, 0)

    def blk(args):
        qb, j = args
        return attend(qb, N_META + j * ATTN_BLOCK + jnp.arange(ATTN_BLOCK))

    o_real = lax.map(blk, (qr, jnp.arange(nb)))
    o_real = jnp.moveaxis(o_real, 0, 1).reshape(b, L - N_META, H, Dv)
    return jnp.concatenate([o_meta, o_real], axis=1)


def rope_tables(L):
    inv = ROPE_THETA ** (-jnp.arange(0, MLA_ROPE, 2, dtype=F32) / MLA_ROPE)
    ang = jnp.arange(L, dtype=F32)[:, None] * inv[None, :]
    ang = jnp.concatenate([ang, ang], axis=-1)
    return jnp.cos(ang), jnp.sin(ang)


def apply_rope(t, cos, sin):
    half = t.shape[-1] // 2
    rot = jnp.concatenate([-t[..., half:], t[..., :half]], axis=-1)
    return (t * cos[None, :, None, :] + rot * sin[None, :, None, :]).astype(t.dtype)


def gdn_mixer(u, w_in, conv_w, a_log, dt_bias, norm_w, w_out):
    b, L, _ = u.shape
    s1 = 2 * GDN_QK_DIM + GDN_V_DIM
    s2 = s1 + GDN_V_DIM
    s3 = s2 + GDN_V_HEADS
    qkv, z, bb, aa = jnp.split(u @ w_in, [s1, s2, s3], axis=-1)
    qkv = jax.nn.silu(causal_depthwise_conv(qkv, conv_w))
    q, k, v = jnp.split(qkv, [GDN_QK_DIM, 2 * GDN_QK_DIM], axis=-1)
    rep = GDN_V_HEADS // GDN_QK_HEADS
    q = jnp.repeat(l2_normalize(q.reshape(b, L, GDN_QK_HEADS, GDN_HEAD_DIM)), rep, axis=2) * (GDN_HEAD_DIM ** -0.5)
    k = jnp.repeat(l2_normalize(k.reshape(b, L, GDN_QK_HEADS, GDN_HEAD_DIM)), rep, axis=2)
    v = v.reshape(b, L, GDN_V_HEADS, GDN_HEAD_DIM)
    beta = jax.nn.sigmoid(bb)
    g = -jnp.exp(a_log) * jax.nn.softplus(aa + dt_bias)
    pad = GDN_CHUNK - N_META
    o = gated_delta_rule_chunked(pad_time(q, pad), pad_time(k, pad), pad_time(v, pad),
                                 pad_time(g, pad), pad_time(beta, pad))[:, pad:]
    o = rms_norm(o, norm_w) * jax.nn.silu(z.reshape(b, L, GDN_V_HEADS, GDN_HEAD_DIM))
    return o.reshape(b, L, GDN_V_DIM) @ w_out


def mla_mixer(u, w_in, norm_q_lat, norm_kv_lat, w_uq, w_ukv, q_norm, k_norm, w_out, cos, sin):
    b, L, _ = u.shape
    c_q, c_kv, k_pe = jnp.split(u @ w_in, [MLA_Q_LORA, MLA_Q_LORA + MLA_KV_LORA], axis=-1)
    c_q = rms_norm(c_q, norm_q_lat)
    c_kv = rms_norm(c_kv, norm_kv_lat)
    q = (c_q @ w_uq).reshape(b, L, MLA_HEADS, MLA_QK)
    kv = (c_kv @ w_ukv).reshape(b, L, MLA_HEADS, MLA_NOPE + MLA_V)
    q_nope, q_pe = q[..., :MLA_NOPE], q[..., MLA_NOPE:]
    k_nope, v = kv[..., :MLA_NOPE], kv[..., MLA_NOPE:]
    q_nope = rms_norm(q_nope, q_norm[:MLA_NOPE])
    q_pe = apply_rope(rms_norm(q_pe, q_norm[MLA_NOPE:]), cos, sin)
    k_nope = rms_norm(k_nope, k_norm[:MLA_NOPE])
    k_pe = apply_rope(rms_norm(k_pe[:, :, None, :], k_norm[MLA_NOPE:]), cos, sin)
    q_full = jnp.concatenate([q_nope, q_pe], axis=-1)
    k_full = jnp.concatenate([k_nope, jnp.broadcast_to(k_pe, (b, L, MLA_HEADS, MLA_ROPE))], axis=-1)
    o = causal_block_attention(q_full, k_full, v, MLA_QK ** -0.5)
    return o.reshape(b, L, MLA_HEADS * MLA_V) @ w_out


def ssm_mixer(u, w_in, conv_w, conv_b, a_log, dt_bias, d_skip, norm_w, w_out):
    b, L, _ = u.shape
    z, xbc, dt_raw = jnp.split(u @ w_in, [SSM_D_INNER, SSM_D_INNER + SSM_CONV_DIM], axis=-1)
    xbc = jax.nn.silu(causal_depthwise_conv(xbc, conv_w, conv_b))
    xs, Bm, Cm = jnp.split(xbc, [SSM_D_INNER, SSM_D_INNER + SSM_GROUPS * SSM_STATE], axis=-1)
    xs = xs.reshape(b, L, SSM_GROUPS, SSM_HPG, SSM_HEAD_DIM)
    Bm = Bm.reshape(b, L, SSM_GROUPS, SSM_STATE)
    Cm = Cm.reshape(b, L, SSM_GROUPS, SSM_STATE)
    dt = jax.nn.softplus(dt_raw + dt_bias).reshape(b, L, SSM_GROUPS, SSM_HPG)
    A = -jnp.exp(a_log).reshape(SSM_GROUPS, SSM_HPG)
    pad = SSM_CHUNK - N_META
    y = ssd_chunked(pad_time(xs, pad), pad_time(dt, pad), A, pad_time(Bm, pad), pad_time(Cm, pad))[:, pad:]
    y = y + xs * d_skip.reshape(SSM_GROUPS, SSM_HPG)[..., None]
    y = y.reshape(b, L, SSM_D_INNER) * jax.nn.silu(z)
    y = rms_norm(y.reshape(b, L, SSM_GROUPS, -1), norm_w.reshape(SSM_GROUPS, -1)).reshape(b, L, SSM_D_INNER)
    return y @ w_out


def sq_relu_mlp(u, w_up, w_down):
    return jnp.square(jax.nn.relu(u @ w_up)) @ w_down


def setup_inputs(seed: int = 0) -> dict:
    key = jax.random.key(seed)
    ks = iter(jax.random.split(key, 48))

    def normal(shape, scale):
        return jax.random.normal(next(ks), shape, F32) * scale

    def gain(shape):
        return 1.0 + normal(shape, 0.02)

    def dt_bias(shape):
        dt = jnp.exp(jax.random.uniform(next(ks), shape, F32, math.log(1e-3), math.log(1e-1)))
        return dt + jnp.log(-jnp.expm1(-dt))

    def a_log(shape):
        return jnp.log(jax.random.uniform(next(ks), shape, F32, 1.0, 16.0))

    D = D_MODEL
    return {
        'x': normal((BATCH, SEQ, D), 1.0),
        'meta_tokens': normal((N_META, D), 1.0),
        'norm_mix': gain((DEPTH, D)),
        'norm_mlp': gain((DEPTH, D)),
        'mlp_w_up': normal((DEPTH, D, MLP_HIDDEN), D ** -0.5),
        'mlp_w_down': normal((DEPTH, MLP_HIDDEN, D), MLP_HIDDEN ** -0.5),
        'gdn_w_in': normal((N_GDN, D, GDN_IN), D ** -0.5),
        'gdn_conv_w': normal((N_GDN, GDN_CONV, 2 * GDN_QK_DIM + GDN_V_DIM), GDN_CONV ** -0.5),
        'gdn_a_log': a_log((N_GDN, GDN_V_HEADS)),
        'gdn_dt_bias': dt_bias((N_GDN, GDN_V_HEADS)),
        'gdn_norm': gain((N_GDN, GDN_HEAD_DIM)),
        'gdn_w_out': normal((N_GDN, GDN_V_DIM, D), GDN_V_DIM ** -0.5),
        'mla_w_in': normal((N_MLA, D, MLA_IN), D ** -0.5),
        'mla_norm_q_lat': gain((N_MLA, MLA_Q_LORA)),
        'mla_norm_kv_lat': gain((N_MLA, MLA_KV_LORA)),
        'mla_w_uq': normal((N_MLA, MLA_Q_LORA, MLA_HEADS * MLA_QK), MLA_Q_LORA ** -0.5),
        'mla_w_ukv': normal((N_MLA, MLA_KV_LORA, MLA_HEADS * (MLA_NOPE + MLA_V)), MLA_KV_LORA ** -0.5),
        'mla_q_norm': gain((N_MLA, MLA_QK)),
        'mla_k_norm': gain((N_MLA, MLA_QK)),
        'mla_w_out': normal((N_MLA, MLA_HEADS * MLA_V, D), (MLA_HEADS * MLA_V) ** -0.5),
        'ssm_w_in': normal((N_SSM, D, SSM_IN), D ** -0.5),
        'ssm_conv_w': normal((N_SSM, SSM_CONV, SSM_CONV_DIM), SSM_CONV ** -0.5),
        'ssm_conv_b': normal((N_SSM, SSM_CONV_DIM), 0.01),
        'ssm_a_log': a_log((N_SSM, SSM_HEADS)),
        'ssm_dt_bias': dt_bias((N_SSM, SSM_HEADS)),
        'ssm_d': gain((N_SSM, SSM_HEADS)),
        'ssm_norm': gain((N_SSM, SSM_D_INNER)),
        'ssm_w_out': normal((N_SSM, SSM_D_INNER, D), SSM_D_INNER ** -0.5),
    }


def reference(x, meta_tokens, norm_mix, norm_mlp, mlp_w_up, mlp_w_down,
              gdn_w_in, gdn_conv_w, gdn_a_log, gdn_dt_bias, gdn_norm, gdn_w_out,
              mla_w_in, mla_norm_q_lat, mla_norm_kv_lat, mla_w_uq, mla_w_ukv, mla_q_norm, mla_k_norm, mla_w_out,
              ssm_w_in, ssm_conv_w, ssm_conv_b, ssm_a_log, ssm_dt_bias, ssm_d, ssm_norm, ssm_w_out):
    b = x.shape[0]
    meta = jnp.broadcast_to(meta_tokens.astype(x.dtype)[None], (b, N_META, x.shape[-1]))
    h = jnp.concatenate([meta, x], axis=1)
    cos, sin = rope_tables(h.shape[1])
    ia = ib = ic = 0
    for i in range(DEPTH):
        u = rms_norm(h, norm_mix[i])
        kind = i % N_MIXERS
        if kind == 0:
            h = h + gdn_mixer(u, gdn_w_in[ia], gdn_conv_w[ia], gdn_a_log[ia], gdn_dt_bias[ia],
                              gdn_norm[ia], gdn_w_out[ia])
            ia += 1
        elif kind == 1:
            h = h + mla_mixer(u, mla_w_in[ib], mla_norm_q_lat[ib], mla_norm_kv_lat[ib], mla_w_uq[ib],
                              mla_w_ukv[ib], mla_q_norm[ib], mla_k_norm[ib], mla_w_out[ib], cos, sin)
            ib += 1
        else:
            h = h + ssm_mixer(u, ssm_w_in[ic], ssm_conv_w[ic], ssm_conv_b[ic], ssm_a_log[ic],
                              ssm_dt_bias[ic], ssm_d[ic], ssm_norm[ic], ssm_w_out[ic])
            ic += 1
        h = h + sq_relu_mlp(rms_norm(h, norm_mlp[i]), mlp_w_up[i], mlp_w_down[i])
    return h[:, N_META:]
```

```python
import functools
import math

import jax
import jax.numpy as jnp
from jax import lax
from jax.experimental import pallas as pl
from jax.experimental.pallas import tpu as pltpu

F32 = jnp.float32
BF16 = jnp.bfloat16
HI = lax.Precision.HIGHEST

EPS = 1e-6
N_META = 16
ROW_TILE = 128
PAD = ROW_TILE - N_META
HALO = 16
CONV_W = 4

HEAD_DIM = 128
GDN_CHUNK = 64
SSM_CHUNK = 128
SSM_HEAD_DIM = 64
SSM_HPG = 8
SSM_STATE = 128
MLA_ROPE = 64
ROPE_THETA = 10000.0
ATT_BLK = 128

VMEM_LIMIT = 56 * 1024 * 1024
MAX_ROW_TILE = 1088
NT = (((1,), (1,)), ((), ()))
TN = (((0,), (0,)), ((), ()))


def _row_tile(m, cap=MAX_ROW_TILE):
    for t in range(min(m, cap) // 64 * 64, 0, -64):
        if m % t == 0:
            return t
    raise ValueError(f"row count {m} has no tile that is a multiple of 64")


def _col_tile(n, cap=1024):
    if n <= cap:
        return n
    for t in range(cap, 0, -128):
        if n % t == 0:
            return t
    raise ValueError(f"column count {n} has no tile that is a multiple of 128")


def _params(*sem):
    return pltpu.CompilerParams(dimension_semantics=sem, vmem_limit_bytes=VMEM_LIMIT)


def _bdot(a, b, dims=None):
    a = a.astype(BF16)
    b = b.astype(BF16)
    if dims is None:
        return jnp.dot(a, b, preferred_element_type=F32)
    return lax.dot_general(a, b, dims, preferred_element_type=F32)


def _hdot(a, b):
    return jnp.dot(a, b, precision=HI, preferred_element_type=F32)


def _sigmoid(x):
    return 1.0 / (1.0 + jnp.exp(-x))


def _silu(x):
    return x * _sigmoid(x)


def _softplus(x):
    return jnp.maximum(x, 0.0) + jnp.log(1.0 + jnp.exp(-jnp.abs(x)))


def _norm_matmul_kernel(x_ref, g_ref, w_ref, o_ref, xn_ref, *, act, sub):
    @pl.when(pl.program_id(1) == 0)
    def _():
        def body(r, _):
            r0 = pl.multiple_of(r * sub, sub)
            x = x_ref[pl.ds(r0, sub), :].astype(F32)
            ms = jnp.mean(x * x, axis=-1, keepdims=True)
            xn_ref[pl.ds(r0, sub), :] = (x * lax.rsqrt(ms + EPS) * g_ref[...]).astype(BF16)
            return 0

        lax.fori_loop(0, x_ref.shape[0] // sub, body, 0)

    y = jnp.dot(xn_ref[...], w_ref[...], preferred_element_type=F32)
    if act == "relu2":
        y = jnp.square(jnp.maximum(y, 0.0))
    o_ref[...] = y.astype(o_ref.dtype)


def norm_matmul(x, gain, w, *, x_col_block=0, act=None, out_dtype=BF16, name):
    m = x.shape[0]
    k, n = w.shape
    tm, tn = _row_tile(m), _col_tile(n)
    return pl.pallas_call(
        functools.partial(_norm_matmul_kernel, act=act, sub=64),
        out_shape=jax.ShapeDtypeStruct((m, n), out_dtype),
        grid=(m // tm, n // tn),
        in_specs=[
            pl.BlockSpec((tm, k), lambda i, j: (i, x_col_block)),
            pl.BlockSpec((1, k), lambda i, j: (0, 0)),
            pl.BlockSpec((k, tn), lambda i, j: (0, j)),
        ],
        out_specs=pl.BlockSpec((tm, tn), lambda i, j: (i, j)),
        scratch_shapes=[pltpu.VMEM((tm, k), BF16)],
        compiler_params=_params("parallel", "arbitrary"),
        name=name,
    )(x, gain.reshape(1, k).astype(F32), w)


def _matmul_residual_kernel(a_ref, w_ref, h_ref, o_ref):
    y = jnp.dot(a_ref[...], w_ref[...], preferred_element_type=F32)

    @pl.when(pl.program_id(2) == 0)
    def _():
        o_ref[...] = h_ref[...] + y

    @pl.when(pl.program_id(2) != 0)
    def _():
        o_ref[...] += y


def matmul_residual(a, w, h, *, name):
    m, k = a.shape
    n = w.shape[1]
    tm, tn, tk = _row_tile(m), _col_tile(n), _col_tile(k, 2048)
    return pl.pallas_call(
        _matmul_residual_kernel,
        out_shape=jax.ShapeDtypeStruct((m, n), F32),
        grid=(m // tm, n // tn, k // tk),
        in_specs=[
            pl.BlockSpec((tm, tk), lambda i, j, kk: (i, kk)),
            pl.BlockSpec((tk, tn), lambda i, j, kk: (kk, j)),
            pl.BlockSpec((tm, tn), lambda i, j, kk: (i, j)),
        ],
        out_specs=pl.BlockSpec((tm, tn), lambda i, j, kk: (i, j)),
        compiler_params=_params("parallel", "parallel", "arbitrary"),
        name=name,
    )(a, w, h)


def _conv_silu(ref, w, r0, rows, bias=None):
    cur = ref[pl.ds(r0, rows), :].astype(F32)
    halo = ref[pl.ds(pl.multiple_of(jnp.maximum(r0 - HALO, 0), HALO), HALO), :].astype(F32)
    x = jnp.concatenate([halo, cur], axis=0)
    y = cur * w[CONV_W - 1:CONV_W, :]
    for j in range(CONV_W - 1):
        y = y + pltpu.roll(x, CONV_W - 1 - j, axis=0)[HALO:, :] * w[j:j + 1, :]
    if bias is not None:
        y = y + bias
    return _silu(y)


def _tri(n, strict=False):
    r = lax.broadcasted_iota(jnp.int32, (n, n), 0)
    c = lax.broadcasted_iota(jnp.int32, (n, n), 1)
    return (r > c) if strict else (r >= c)


def _unit_lower_inverse(a, n):
    neg = -a
    eye = (lax.broadcasted_iota(jnp.int32, (n, n), 0) == lax.broadcasted_iota(jnp.int32, (n, n), 1)).astype(F32)
    t = eye + neg
    p = neg
    k = 1
    while k < n // 2:
        p = _bdot(p, p)
        t = t + _bdot(t, p)
        k *= 2
    return t


def _gdn_kernel(q_ref, k_ref, v_ref, z_ref, wq_ref, wk_ref, wv_ref, gcol_ref, grow_ref, pcol_ref, prow_ref,
                nw_ref, o_ref, s_ref, *, n_chunks):
    c_len = GDN_CHUNK
    d = HEAD_DIM
    o_ref[pl.ds(0, c_len), :] = jnp.zeros((c_len, 2 * d), o_ref.dtype)
    s_ref[...] = jnp.zeros_like(s_ref)

    wq, wk, wv = wq_ref[...], wk_ref[...], wv_ref[...]
    neg_a_col = -jnp.exp(pcol_ref[:, 0:2])
    dtb_col = pcol_ref[:, 2:4]
    neg_a_row = -jnp.exp(prow_ref[0:2, :])
    dtb_row = prow_ref[2:4, :]
    lower = _tri(c_len)
    strict = _tri(c_len, strict=True)
    lower_f = lower.astype(F32)
    upper_f = (lax.broadcasted_iota(jnp.int32, (c_len, c_len), 0)
               <= lax.broadcasted_iota(jnp.int32, (c_len, c_len), 1)).astype(F32)
    norm_w = nw_ref[...]

    def body(c, _):
        r0 = pl.multiple_of(c * c_len, c_len)
        valid_col = (r0 + lax.broadcasted_iota(jnp.int32, (c_len, 1), 0)) >= PAD
        valid_row = (r0 + lax.broadcasted_iota(jnp.int32, (1, c_len), 1)) >= PAD

        q = jnp.where(valid_col, _conv_silu(q_ref, wq, r0, c_len), 0.0)
        k = jnp.where(valid_col, _conv_silu(k_ref, wk, r0, c_len), 0.0)
        v2 = jnp.where(valid_col, _conv_silu(v_ref, wv, r0, c_len), 0.0)
        qn = q * lax.rsqrt(jnp.sum(q * q, axis=-1, keepdims=True) + EPS) * (d ** -0.5)
        kn = k * lax.rsqrt(jnp.sum(k * k, axis=-1, keepdims=True) + EPS)
        kk = _bdot(kn, kn, NT)
        qk = _bdot(qn, kn, NT)

        gc = gcol_ref[c]
        gr = grow_ref[c]
        beta_col = jnp.where(valid_col, _sigmoid(gc[:, 0:2]), 0.0)
        beta_row = jnp.where(valid_row, _sigmoid(gr[0:2, :]), 0.0)
        g_col = jnp.where(valid_col, neg_a_col * _softplus(gc[:, 2:4] + dtb_col), 0.0)
        g_row = jnp.where(valid_row, neg_a_row * _softplus(gr[2:4, :] + dtb_row), 0.0)
        cum_col = _hdot(lower_f, g_col)
        cum_row = _hdot(g_row, upper_f)
        z2 = z_ref[pl.ds(r0, c_len), :].astype(F32)

        outs = []
        for i in range(2):
            gcol = cum_col[:, i:i + 1]
            grow = cum_row[i:i + 1, :]
            bcol = beta_col[:, i:i + 1]
            brow = beta_row[i:i + 1, :]
            g_last = cum_col[c_len - 1:c_len, i:i + 1]
            decay = jnp.exp(jnp.where(lower, gcol - grow, -jnp.inf))
            a = jnp.where(strict, kk * bcol * decay, 0.0)
            t = _unit_lower_inverse(a, c_len)
            v = v2[:, i * d:(i + 1) * d]
            u = _bdot(t * brow, v)
            w = _bdot(t * (brow * jnp.exp(grow)), kn)
            s = s_ref[i]
            v_new = u - _bdot(w, s)
            o = jnp.exp(gcol) * _bdot(qn, s) + _bdot(qk * decay, v_new)
            s_ref[i] = s * jnp.exp(g_last) + _bdot(kn * jnp.exp(g_last - gcol), v_new, TN)
            o = o * lax.rsqrt(jnp.mean(o * o, axis=-1, keepdims=True) + EPS) * norm_w
            outs.append(o * _silu(z2[:, i * d:(i + 1) * d]))
        out = jnp.where(valid_col, jnp.concatenate(outs, axis=1), 0.0)
        o_ref[pl.ds(r0, c_len), :] = out.astype(o_ref.dtype)
        return 0

    lax.fori_loop(1, n_chunks, body, 0)


def gdn_core(qkvz, ba, conv_w, a_log, dt_bias, norm_w, batch):
    m = qkvz.shape[0]
    lp = m // batch
    hk = 16
    d = HEAD_DIM
    n_chunks = lp // GDN_CHUNK
    qkvz = qkvz.reshape(batch, lp, qkvz.shape[1])
    g4 = jnp.stack([ba[:, 0:32:2], ba[:, 1:32:2], ba[:, 32::2], ba[:, 33::2]], axis=-1)
    g4 = g4.reshape(batch, n_chunks, GDN_CHUNK, hk, 4)
    gcol = jnp.transpose(g4, (0, 3, 1, 2, 4))
    grow = jnp.transpose(g4, (0, 3, 1, 4, 2))
    p4 = jnp.concatenate([a_log.reshape(hk, 2), dt_bias.reshape(hk, 2)], axis=-1).astype(F32)
    pcol = p4.reshape(hk, 1, 4)
    prow = p4.reshape(hk, 4, 1)
    vb = 2 * d
    kern = functools.partial(_gdn_kernel, n_chunks=n_chunks)
    out = pl.pallas_call(
        kern,
        out_shape=jax.ShapeDtypeStruct((batch, lp, 2 * hk * d), BF16),
        grid=(batch, hk),
        in_specs=[
            pl.BlockSpec((None, lp, d), lambda b, h: (b, 0, h)),
            pl.BlockSpec((None, lp, d), lambda b, h: (b, 0, hk + h)),
            pl.BlockSpec((None, lp, vb), lambda b, h: (b, 0, hk + h)),
            pl.BlockSpec((None, lp, vb), lambda b, h: (b, 0, 2 * hk + h)),
            pl.BlockSpec((CONV_W, d), lambda b, h: (0, h)),
            pl.BlockSpec((CONV_W, d), lambda b, h: (0, hk + h)),
            pl.BlockSpec((CONV_W, vb), lambda b, h: (0, hk + h)),
            pl.BlockSpec((None, None, n_chunks, GDN_CHUNK, 4), lambda b, h: (b, h, 0, 0, 0)),
            pl.BlockSpec((None, None, n_chunks, 4, GDN_CHUNK), lambda b, h: (b, h, 0, 0, 0)),
            pl.BlockSpec((None, 1, 4), lambda b, h: (h, 0, 0)),
            pl.BlockSpec((None, 4, 1), lambda b, h: (h, 0, 0)),
            pl.BlockSpec((1, d), lambda b, h: (0, 0)),
        ],
        out_specs=pl.BlockSpec((None, lp, vb), lambda b, h: (b, 0, h)),
        scratch_shapes=[pltpu.VMEM((2, d, d), F32)],
        compiler_params=_params("parallel", "parallel"),
        name="gdn_core",
    )(qkvz, qkvz, qkvz, qkvz, conv_w, conv_w, conv_w, gcol, grow, pcol, prow, norm_w.reshape(1, d).astype(F32))
    return out.reshape(m, 2 * hk * d)


def _ssd_kernel(z_ref, x_ref, b_ref, c_ref, wx_ref, wb_ref, wc_ref, bx_ref, bb_ref, bc_ref, dcol_ref, drow_ref,
                pcol_ref, prow_ref, nw_ref, o_ref, s_ref, *, n_chunks):
    c_len = SSM_CHUNK
    hp = SSM_HEAD_DIM
    nh = SSM_HPG
    width = nh * hp
    s_ref[...] = jnp.zeros_like(s_ref)

    wx, wb, wc = wx_ref[...], wb_ref[...], wc_ref[...]
    bx, bb, bc = bx_ref[...], bb_ref[...], bc_ref[...]
    neg_a_col = -jnp.exp(pcol_ref[0:1, :])
    dtb_col = pcol_ref[1:2, :]
    d_skip = pcol_ref[2:3, :]
    neg_a_row = -jnp.exp(prow_ref[:, 0:1])
    dtb_row = prow_ref[:, 1:2]
    lower = _tri(c_len)
    lower_f = lower.astype(F32)
    upper_f = (lax.broadcasted_iota(jnp.int32, (c_len, c_len), 0)
               <= lax.broadcasted_iota(jnp.int32, (c_len, c_len), 1)).astype(F32)
    expand = (lax.shift_right_logical(lax.broadcasted_iota(jnp.int32, (nh, width), 1), int(math.log2(hp)))
              == lax.broadcasted_iota(jnp.int32, (nh, width), 0)).astype(F32)
    lane = lax.broadcasted_iota(jnp.int32, (1, 2 * hp), 1)
    d_skip_x = _hdot(d_skip, expand)
    norm_w = nw_ref[...]

    def body(c, _):
        r0 = pl.multiple_of(c * c_len, c_len)
        valid_col = (r0 + lax.broadcasted_iota(jnp.int32, (c_len, 1), 0)) >= PAD
        valid_row = (r0 + lax.broadcasted_iota(jnp.int32, (1, c_len), 1)) >= PAD
        xs = jnp.where(valid_col, _conv_silu(x_ref, wx, r0, c_len, bx), 0.0)
        bm = jnp.where(valid_col, _conv_silu(b_ref, wb, r0, c_len, bb), 0.0)
        cm = jnp.where(valid_col, _conv_silu(c_ref, wc, r0, c_len, bc), 0.0)
        dt_col = jnp.where(valid_col, _softplus(dcol_ref[c] + dtb_col), 0.0)
        dt_row = jnp.where(valid_row, _softplus(drow_ref[c] + dtb_row), 0.0)
        cum_col = _hdot(lower_f, dt_col * neg_a_col)
        cum_row = _hdot(dt_row * neg_a_row, upper_f)
        cum_last = cum_col[c_len - 1:c_len, :]

        xdt = xs * _hdot(dt_col, expand)
        cb = _bdot(cm, bm, NT)
        s = s_ref[...]
        y = _bdot(cm, s) * _hdot(jnp.exp(cum_col), expand)
        s_ref[...] = (s * _hdot(jnp.exp(cum_last), expand)
                      + _bdot(bm, xdt * _hdot(jnp.exp(cum_last - cum_col), expand), TN))
        diag = []
        for pair in range(nh // 2):
            sc = []
            for j in (2 * pair, 2 * pair + 1):
                lmat = jnp.exp(jnp.where(lower, cum_col[:, j:j + 1] - cum_row[j:j + 1, :], -jnp.inf))
                sc.append((cb * lmat).astype(BF16))
            xp = xdt[:, pair * 2 * hp:(pair + 1) * 2 * hp]
            rhs = jnp.concatenate([jnp.where(lane < hp, xp, 0.0), jnp.where(lane >= hp, xp, 0.0)], axis=0)
            diag.append(_bdot(jnp.concatenate(sc, axis=1), rhs))
        y = y + jnp.concatenate(diag, axis=1) + xs * d_skip_x
        y = y * _silu(z_ref[pl.ds(r0, c_len), :].astype(F32))
        y = y * lax.rsqrt(jnp.mean(y * y, axis=-1, keepdims=True) + EPS) * norm_w
        o_ref[pl.ds(r0, c_len), :] = jnp.where(valid_col, y, 0.0).astype(o_ref.dtype)
        return 0

    lax.fori_loop(0, n_chunks, body, 0)


def ssd_core(zx, dt_raw, conv_w, conv_b, a_log, dt_bias, d_skip, norm_w, batch):
    m = zx.shape[0]
    lp = m // batch
    ng = 8
    width = SSM_HPG * SSM_HEAD_DIM
    d_inner = ng * width
    n_chunks = lp // SSM_CHUNK
    zx = zx.reshape(batch, lp, zx.shape[1])
    d5 = dt_raw.reshape(batch, n_chunks, SSM_CHUNK, ng, SSM_HPG)
    dcol = jnp.transpose(d5, (0, 3, 1, 2, 4))
    drow = jnp.transpose(d5, (0, 3, 1, 4, 2))
    p3 = jnp.stack([a_log.reshape(ng, SSM_HPG), dt_bias.reshape(ng, SSM_HPG), d_skip.reshape(ng, SSM_HPG)],
                   axis=1).astype(F32)
    prow = jnp.transpose(p3, (0, 2, 1))
    conv_b = conv_b.reshape(1, -1).astype(F32)
    xo, bo, co = d_inner // width, d_inner // SSM_STATE, (d_inner + ng * SSM_STATE) // SSM_STATE
    kern = functools.partial(_ssd_kernel, n_chunks=n_chunks)
    out = pl.pallas_call(
        kern,
        out_shape=jax.ShapeDtypeStruct((batch, lp, d_inner), BF16),
        grid=(batch, ng),
        in_specs=[
            pl.BlockSpec((None, lp, width), lambda b, g: (b, 0, g)),
            pl.BlockSpec((None, lp, width), lambda b, g: (b, 0, xo + g)),
            pl.BlockSpec((None, lp, SSM_STATE), lambda b, g: (b, 0, 2 * bo + g)),
            pl.BlockSpec((None, lp, SSM_STATE), lambda b, g: (b, 0, bo + co + g)),
            pl.BlockSpec((CONV_W, width), lambda b, g: (0, g)),
            pl.BlockSpec((CONV_W, SSM_STATE), lambda b, g: (0, bo + g)),
            pl.BlockSpec((CONV_W, SSM_STATE), lambda b, g: (0, co + g)),
            pl.BlockSpec((1, width), lambda b, g: (0, g)),
            pl.BlockSpec((1, SSM_STATE), lambda b, g: (0, bo + g)),
            pl.BlockSpec((1, SSM_STATE), lambda b, g: (0, co + g)),
            pl.BlockSpec((None, None, n_chunks, SSM_CHUNK, SSM_HPG), lambda b, g: (b, g, 0, 0, 0)),
            pl.BlockSpec((None, None, n_chunks, SSM_HPG, SSM_CHUNK), lambda b, g: (b, g, 0, 0, 0)),
            pl.BlockSpec((None, 3, SSM_HPG), lambda b, g: (g, 0, 0)),
            pl.BlockSpec((None, SSM_HPG, 3), lambda b, g: (g, 0, 0)),
            pl.BlockSpec((1, width), lambda b, g: (0, g)),
        ],
        out_specs=pl.BlockSpec((None, lp, width), lambda b, g: (b, 0, g)),
        scratch_shapes=[pltpu.VMEM((SSM_STATE, width), F32)],
        compiler_params=_params("parallel", "parallel"),
        name="ssd_core",
    )(zx, zx, zx, zx, conv_w, conv_w, conv_w, conv_b, conv_b, conv_b, dcol, drow, p3, prow,
      norm_w.reshape(1, d_inner).astype(F32))
    return out.reshape(m, d_inner)


def _rope_tables(lp):
    inv = ROPE_THETA ** (-jnp.arange(0, MLA_ROPE, 2, dtype=F32) / MLA_ROPE)
    pos = jnp.maximum(jnp.arange(lp, dtype=F32) - PAD, 0.0)
    ang = pos[:, None] * inv[None, :]
    return jnp.tile(jnp.cos(ang), (1, 4)), jnp.tile(jnp.sin(ang), (1, 4))


def _pair_head(lane):
    return lax.shift_right_logical(lane, 5) & 1


def _rope_pair(t, cos, sin):
    lane = lax.broadcasted_iota(jnp.int32, (1, t.shape[1]), 1)
    partner = pltpu.roll(t, t.shape[1] // 2, axis=1)
    return t * cos + jnp.where(lane < t.shape[1] // 2, -partner, partner) * sin


def _qprep_kernel(q_ref, gn_ref, gp_ref, cos_ref, sin_ref, o_ref, *, n_heads, scale):
    d = HEAD_DIM
    cos, sin = cos_ref[...], sin_ref[...]
    for h in range(n_heads):
        x = q_ref[:, h * d:(h + 1) * d]
        y = x * lax.rsqrt(jnp.mean(x * x, axis=-1, keepdims=True) + EPS) * gn_ref[:, h * d:(h + 1) * d]
        o_ref[:, h * d:(h + 1) * d] = (y * scale).astype(o_ref.dtype)
    r = lax.broadcasted_iota(jnp.int32, (d, d), 0)
    c = lax.broadcasted_iota(jnp.int32, (d, d), 1)
    same_head = (_pair_head(r) == _pair_head(c)).astype(F32)
    base = n_heads * d
    for p in range(n_heads // 2):
        x = q_ref[:, base + p * d:base + (p + 1) * d]
        ms = _hdot(x * x, same_head) * (1.0 / MLA_ROPE)
        y = x * lax.rsqrt(ms + EPS) * gp_ref[:, p * d:(p + 1) * d]
        o_ref[:, base + p * d:base + (p + 1) * d] = (_rope_pair(y, cos, sin) * scale).astype(o_ref.dtype)


def _kprep_kernel(kv_ref, kpe_ref, gn_ref, gp_ref, cos_ref, sin_ref, k_ref, v_ref, pe_ref, *, n_heads):
    d = HEAD_DIM
    for h in range(n_heads):
        x = kv_ref[:, h * d:(h + 1) * d]
        y = x * lax.rsqrt(jnp.mean(x * x, axis=-1, keepdims=True) + EPS) * gn_ref[:, h * d:(h + 1) * d]
        k_ref[:, h * d:(h + 1) * d] = y.astype(k_ref.dtype)
    v_ref[...] = kv_ref[:, n_heads * d:].astype(v_ref.dtype)
    x = kpe_ref[...]
    y = x * lax.rsqrt(jnp.mean(x * x, axis=-1, keepdims=True) + EPS) * gp_ref[...]
    pe_ref[...] = _rope_pair(y, cos_ref[...], sin_ref[...]).astype(pe_ref.dtype)


def _flash_kernel(qn_ref, qpe_ref, kn_ref, kpe_ref, v_ref, o_ref):
    d = HEAD_DIM
    blk = ATT_BLK
    i = pl.program_id(2)
    lane = lax.broadcasted_iota(jnp.int32, (1, d), 1)
    qpe = qpe_ref[...]
    qf = []
    for hh in range(2):
        mine = _pair_head(lane) == hh
        qf.append(jnp.concatenate([qn_ref[:, hh * d:(hh + 1) * d], jnp.where(mine, qpe, jnp.zeros_like(qpe))],
                                  axis=1))
    qpos = i * blk + lax.broadcasted_iota(jnp.int32, (blk, 1), 0)
    q_limit = jnp.maximum(qpos, PAD)

    def body(j, carry):
        c0 = pl.multiple_of(j * blk, blk)
        kpos = c0 + lax.broadcasted_iota(jnp.int32, (1, blk), 1)
        ok = (kpos >= PAD) & (kpos <= q_limit)
        kpe = kpe_ref[pl.ds(c0, blk), :]
        new = []
        for hh in range(2):
            m, l, acc = carry[hh]
            kf = jnp.concatenate([kn_ref[pl.ds(c0, blk), hh * d:(hh + 1) * d], kpe], axis=1)
            s = lax.dot_general(qf[hh], kf, NT, preferred_element_type=F32)
            s = jnp.where(ok, s, -1e30)
            m_new = jnp.maximum(m, jnp.max(s, axis=-1, keepdims=True))
            p = jnp.exp(s - m_new)
            alpha = jnp.exp(m - m_new)
            l = alpha * l + jnp.sum(p, axis=-1, keepdims=True)
            acc = alpha * acc + jnp.dot(p.astype(BF16), v_ref[pl.ds(c0, blk), hh * d:(hh + 1) * d],
                                        preferred_element_type=F32)
            new.append((m_new, l, acc))
        return tuple(new)

    init = tuple((jnp.full((blk, 1), -1e30, F32), jnp.zeros((blk, 1), F32), jnp.zeros((blk, d), F32))
                 for _ in range(2))
    res = lax.fori_loop(0, i + 1, body, init)
    out = jnp.concatenate([acc / l for (_, l, acc) in res], axis=1)
    o_ref[...] = jnp.where(qpos >= PAD, out, 0.0).astype(o_ref.dtype)


def mla_attention(q_raw, kv_raw, kpe_src, kpe_col_block, q_norm, k_norm, batch):
    m = q_raw.shape[0]
    lp = m // batch
    nh = 16
    d = HEAD_DIM
    half = MLA_ROPE // 2
    scale = (d + MLA_ROPE) ** -0.5
    cos, sin = _rope_tables(lp)
    qn_gain = jnp.tile(q_norm[:d], nh).reshape(1, nh * d).astype(F32)
    kn_gain = jnp.tile(k_norm[:d], nh).reshape(1, nh * d).astype(F32)
    qf, qs = q_norm[d:d + half], q_norm[d + half:]
    kf, ks = k_norm[d:d + half], k_norm[d + half:]
    qp_gain = jnp.tile(jnp.concatenate([qf, qf, qs, qs]), nh // 2).reshape(1, nh // 2 * d).astype(F32)
    kp_gain = jnp.concatenate([kf, kf, ks, ks]).reshape(1, d).astype(F32)
    tp = ATT_BLK
    nt = lp // tp
    qw = q_raw.shape[1]

    q_prep = pl.pallas_call(
        functools.partial(_qprep_kernel, n_heads=nh, scale=scale),
        out_shape=jax.ShapeDtypeStruct((batch, lp, qw), BF16),
        grid=(batch, nt),
        in_specs=[
            pl.BlockSpec((None, tp, qw), lambda b, t: (b, t, 0)),
            pl.BlockSpec((1, nh * d), lambda b, t: (0, 0)),
            pl.BlockSpec((1, nh // 2 * d), lambda b, t: (0, 0)),
            pl.BlockSpec((tp, d), lambda b, t: (t, 0)),
            pl.BlockSpec((tp, d), lambda b, t: (t, 0)),
        ],
        out_specs=pl.BlockSpec((None, tp, qw), lambda b, t: (b, t, 0)),
        compiler_params=_params("parallel", "parallel"),
        name="mla_q_prep",
    )(q_raw.reshape(batch, lp, qw), qn_gain, qp_gain, cos, sin)

    kvw = kv_raw.shape[1]
    kpe3 = kpe_src.reshape(batch, lp, kpe_src.shape[1])
    k_prep, v_prep, pe_prep = pl.pallas_call(
        functools.partial(_kprep_kernel, n_heads=nh),
        out_shape=(jax.ShapeDtypeStruct((batch, lp, nh * d), BF16),
                   jax.ShapeDtypeStruct((batch, lp, nh * d), BF16),
                   jax.ShapeDtypeStruct((batch, lp, d), BF16)),
        grid=(batch, nt),
        in_specs=[
            pl.BlockSpec((None, tp, kvw), lambda b, t: (b, t, 0)),
            pl.BlockSpec((None, tp, d), lambda b, t: (b, t, kpe_col_block)),
            pl.BlockSpec((1, nh * d), lambda b, t: (0, 0)),
            pl.BlockSpec((1, d), lambda b, t: (0, 0)),
            pl.BlockSpec((tp, d), lambda b, t: (t, 0)),
            pl.BlockSpec((tp, d), lambda b, t: (t, 0)),
        ],
        out_specs=(pl.BlockSpec((None, tp, nh * d), lambda b, t: (b, t, 0)),
                   pl.BlockSpec((None, tp, nh * d), lambda b, t: (b, t, 0)),
                   pl.BlockSpec((None, tp, d), lambda b, t: (b, t, 0))),
        compiler_params=_params("parallel", "parallel"),
        name="mla_kv_prep",
    )(kv_raw.reshape(batch, lp, kvw), kpe3, kn_gain, kp_gain, cos, sin)

    np_ = nh // 2
    out = pl.pallas_call(
        _flash_kernel,
        out_shape=jax.ShapeDtypeStruct((batch, lp, nh * d), BF16),
        grid=(batch, np_, nt),
        in_specs=[
            pl.BlockSpec((None, tp, 2 * d), lambda b, p, i: (b, i, p)),
            pl.BlockSpec((None, tp, d), lambda b, p, i: (b, i, nh + p)),
            pl.BlockSpec((None, lp, 2 * d), lambda b, p, i: (b, 0, p)),
            pl.BlockSpec((None, lp, d), lambda b, p, i: (b, 0, 0)),
            pl.BlockSpec((None, lp, 2 * d), lambda b, p, i: (b, 0, p)),
        ],
        out_specs=pl.BlockSpec((None, tp, 2 * d), lambda b, p, i: (b, i, p)),
        compiler_params=_params("parallel", "parallel", "arbitrary"),
        name="mla_flash",
    )(q_prep, q_prep, k_prep, pe_prep, v_prep)
    return out.reshape(m, nh * d)


def _gdn_layer(h, gain, w_in, conv_w, a_log, dt_bias, norm_w, w_out, batch):
    n_big = w_in.shape[1] - 64
    qkvz = norm_matmul(h, gain, w_in[:, :n_big].astype(BF16), name="gdn_in_proj")
    ba = norm_matmul(h, gain, w_in[:, n_big:].astype(BF16), out_dtype=F32, name="gdn_gate_proj")
    o = gdn_core(qkvz, ba, conv_w.astype(F32), a_log, dt_bias, norm_w, batch)
    return matmul_residual(o, w_out.astype(BF16), h, name="gdn_out_proj")


def _mla_layer(h, gain, w_in, norm_q_lat, norm_kv_lat, w_uq, w_ukv, q_norm, k_norm, w_out, batch):
    nh, d, r = 16, HEAD_DIM, MLA_ROPE
    q_lora, kv_lora = norm_q_lat.shape[0], norm_kv_lat.shape[0]
    half = r // 2
    w_cq, w_ckv, w_pe = w_in[:, :q_lora], w_in[:, q_lora:q_lora + kv_lora], w_in[:, q_lora + kv_lora:]
    w_pe_t = jnp.concatenate([w_pe[:, :half], w_pe[:, :half], w_pe[:, half:], w_pe[:, half:]], axis=1)
    gap = (-(kv_lora + d)) % q_lora
    w_lat = jnp.concatenate([w_ckv, w_pe_t, jnp.zeros((w_in.shape[0], gap), w_in.dtype), w_cq], axis=1)
    lat = norm_matmul(h, gain, w_lat.astype(BF16), out_dtype=F32, name="mla_in_proj")
    wq = w_uq.reshape(q_lora, nh, d + r)
    wq_pe = wq[:, :, d:].reshape(q_lora, nh // 2, 2, 2, half)
    wq_pe = jnp.transpose(wq_pe, (0, 1, 3, 2, 4)).reshape(q_lora, nh // 2 * d)
    wq_all = jnp.concatenate([wq[:, :, :d].reshape(q_lora, nh * d), wq_pe], axis=1)
    q_raw = norm_matmul(lat, norm_q_lat, wq_all.astype(BF16), x_col_block=(kv_lora + d + gap) // q_lora,
                        out_dtype=F32, name="mla_q_up")
    wkv = w_ukv.reshape(kv_lora, nh, 2 * d)
    wkv_all = jnp.concatenate([wkv[:, :, :d].reshape(kv_lora, nh * d), wkv[:, :, d:].reshape(kv_lora, nh * d)],
                              axis=1)
    kv_raw = norm_matmul(lat, norm_kv_lat, wkv_all.astype(BF16), x_col_block=0, out_dtype=F32, name="mla_kv_up")
    o = mla_attention(q_raw, kv_raw, lat, kv_lora // d, q_norm, k_norm, batch)
    return matmul_residual(o, w_out.astype(BF16), h, name="mla_out_proj")


def _ssm_layer(h, gain, w_in, conv_w, conv_b, a_log, dt_bias, d_skip, norm_w, w_out, batch):
    n_big = w_in.shape[1] - 64
    zx = norm_matmul(h, gain, w_in[:, :n_big].astype(BF16), name="ssm_in_proj")
    dt_raw = norm_matmul(h, gain, w_in[:, n_big:].astype(BF16), out_dtype=F32, name="ssm_dt_proj")
    y = ssd_core(zx, dt_raw, conv_w.astype(F32), conv_b, a_log, dt_bias, d_skip, norm_w, batch)
    return matmul_residual(y, w_out.astype(BF16), h, name="ssm_out_proj")


def _mlp_layer(h, gain, w_up, w_down):
    act = norm_matmul(h, gain, w_up.astype(BF16), act="relu2", name="mlp_up")
    return matmul_residual(act, w_down.astype(BF16), h, name="mlp_down")


def kernel(x, meta_tokens, norm_mix, norm_mlp, mlp_w_up, mlp_w_down, gdn_w_in, gdn_conv_w, gdn_a_log, gdn_dt_bias, gdn_norm, gdn_w_out, mla_w_in, mla_norm_q_lat, mla_norm_kv_lat, mla_w_uq, mla_w_ukv, mla_q_norm, mla_k_norm, mla_w_out, ssm_w_in, ssm_conv_w, ssm_conv_b, ssm_a_log, ssm_dt_bias, ssm_d, ssm_norm, ssm_w_out):
    batch, seq, dm = x.shape
    depth = norm_mix.shape[0]
    lp = ROW_TILE + seq
    meta = jnp.broadcast_to(meta_tokens.astype(x.dtype)[None], (batch, N_META, dm))
    h = jnp.concatenate([jnp.zeros((batch, PAD, dm), x.dtype), meta, x], axis=1).reshape(batch * lp, dm)
    ia = ib = ic = 0
    for i in range(depth):
        kind = i % 3
        if kind == 0:
            h = _gdn_layer(h, norm_mix[i], gdn_w_in[ia], gdn_conv_w[ia], gdn_a_log[ia], gdn_dt_bias[ia],
                           gdn_norm[ia], gdn_w_out[ia], batch)
            ia += 1
        elif kind == 1:
            h = _mla_layer(h, norm_mix[i], mla_w_in[ib], mla_norm_q_lat[ib], mla_norm_kv_lat[ib], mla_w_uq[ib],
                           mla_w_ukv[ib], mla_q_norm[ib], mla_k_norm[ib], mla_w_out[ib], batch)
            ib += 1
        else:
            h = _ssm_layer(h, norm_mix[i], ssm_w_in[ic], ssm_conv_w[ic], ssm_conv_b[ic], ssm_a_log[ic],
                           ssm_dt_bias[ic], ssm_d[ic], ssm_norm[ic], ssm_w_out[ic], batch)
            ic += 1
        h = _mlp_layer(h, norm_mlp[i], mlp_w_up[i], mlp_w_down[i])
    return h.reshape(batch, lp, dm)[:, ROW_TILE:]
```

```python
import functools
import math

import jax
import jax.numpy as jnp
from jax import lax
from jax.experimental import pallas as pl
from jax.experimental.pallas import tpu as pltpu

F32 = jnp.float32
BF16 = jnp.bfloat16
HI = lax.Precision.HIGHEST

EPS = 1e-6
N_META = 16
ROW_TILE = 128
PAD = ROW_TILE - N_META
HALO = 16
CONV_W = 4

HEAD_DIM = 128
GDN_CHUNK = 64
SSM_CHUNK = 128
SSM_HEAD_DIM = 64
SSM_HPG = 8
SSM_STATE = 128
MLA_ROPE = 64
ROPE_THETA = 10000.0
ATT_BLK = 128
ATT_KV_BLK = 512
GDN_QK_PER_STEP = 2
GDN_UNROLL = 2

VMEM_LIMIT = 56 * 1024 * 1024
MAX_ROW_TILE = 1088
NT = (((1,), (1,)), ((), ()))
TN = (((0,), (0,)), ((), ()))


def _row_tile(m, cap=MAX_ROW_TILE):
    for t in range(min(m, cap) // 64 * 64, 0, -64):
        if m % t == 0:
            return t
    raise ValueError(f"row count {m} has no tile that is a multiple of 64")


def _col_tile(n, cap=1024):
    if n <= cap:
        return n
    for t in range(cap, 0, -128):
        if n % t == 0:
            return t
    raise ValueError(f"column count {n} has no tile that is a multiple of 128")


def _params(*sem):
    return pltpu.CompilerParams(dimension_semantics=sem, vmem_limit_bytes=VMEM_LIMIT)


def _bdot(a, b, dims=None):
    a = a.astype(BF16)
    b = b.astype(BF16)
    if dims is None:
        return jnp.dot(a, b, preferred_element_type=F32)
    return lax.dot_general(a, b, dims, preferred_element_type=F32)


def _hdot(a, b):
    return jnp.dot(a, b, precision=HI, preferred_element_type=F32)


def _sigmoid(x):
    return 1.0 / (1.0 + jnp.exp(-x))


def _silu(x):
    return x * _sigmoid(x)


def _softplus(x):
    return jnp.maximum(x, 0.0) + jnp.log(1.0 + jnp.exp(-jnp.abs(x)))


def _norm_matmul_kernel(x_ref, g_ref, w_ref, o_ref, xn_ref, *, act, sub):
    @pl.when(pl.program_id(1) == 0)
    def _():
        def body(r, _):
            r0 = pl.multiple_of(r * sub, sub)
            x = x_ref[pl.ds(r0, sub), :].astype(F32)
            ms = jnp.mean(x * x, axis=-1, keepdims=True)
            xn_ref[pl.ds(r0, sub), :] = (x * lax.rsqrt(ms + EPS) * g_ref[...]).astype(BF16)
            return 0

        lax.fori_loop(0, x_ref.shape[0] // sub, body, 0)

    y = jnp.dot(xn_ref[...], w_ref[...], preferred_element_type=F32)
    if act == "relu2":
        y = jnp.square(jnp.maximum(y, 0.0))
    o_ref[...] = y.astype(o_ref.dtype)


def norm_matmul(x, gain, w, *, x_col_block=0, act=None, out_dtype=BF16, name):
    m = x.shape[0]
    k, n = w.shape
    tm, tn = _row_tile(m), _col_tile(n)
    return pl.pallas_call(
        functools.partial(_norm_matmul_kernel, act=act, sub=64),
        out_shape=jax.ShapeDtypeStruct((m, n), out_dtype),
        grid=(m // tm, n // tn),
        in_specs=[
            pl.BlockSpec((tm, k), lambda i, j: (i, x_col_block)),
            pl.BlockSpec((1, k), lambda i, j: (0, 0)),
            pl.BlockSpec((k, tn), lambda i, j: (0, j)),
        ],
        out_specs=pl.BlockSpec((tm, tn), lambda i, j: (i, j)),
        scratch_shapes=[pltpu.VMEM((tm, k), BF16)],
        compiler_params=_params("parallel", "arbitrary"),
        name=name,
    )(x, gain.reshape(1, k).astype(F32), w)


def _matmul_residual_kernel(a_ref, w_ref, h_ref, o_ref):
    y = jnp.dot(a_ref[...], w_ref[...], preferred_element_type=F32)

    @pl.when(pl.program_id(2) == 0)
    def _():
        o_ref[...] = h_ref[...] + y

    @pl.when(pl.program_id(2) != 0)
    def _():
        o_ref[...] += y


def matmul_residual(a, w, h, *, name):
    m, k = a.shape
    n = w.shape[1]
    tm, tn, tk = _row_tile(m), _col_tile(n), _col_tile(k, 2048)
    return pl.pallas_call(
        _matmul_residual_kernel,
        out_shape=jax.ShapeDtypeStruct((m, n), F32),
        grid=(m // tm, n // tn, k // tk),
        in_specs=[
            pl.BlockSpec((tm, tk), lambda i, j, kk: (i, kk)),
            pl.BlockSpec((tk, tn), lambda i, j, kk: (kk, j)),
            pl.BlockSpec((tm, tn), lambda i, j, kk: (i, j)),
        ],
        out_specs=pl.BlockSpec((tm, tn), lambda i, j, kk: (i, j)),
        compiler_params=_params("parallel", "parallel", "arbitrary"),
        name=name,
    )(a, w, h)


def _conv_silu(ref, w, r0, rows, bias=None):
    cur = ref[pl.ds(r0, rows), :].astype(F32)
    halo = ref[pl.ds(pl.multiple_of(jnp.maximum(r0 - HALO, 0), HALO), HALO), :].astype(F32)
    x = jnp.concatenate([halo, cur], axis=0)
    y = cur * w[CONV_W - 1:CONV_W, :]
    for j in range(CONV_W - 1):
        y = y + pltpu.roll(x, CONV_W - 1 - j, axis=0)[HALO:, :] * w[j:j + 1, :]
    if bias is not None:
        y = y + bias
    return _silu(y)


def _tri(n, strict=False):
    r = lax.broadcasted_iota(jnp.int32, (n, n), 0)
    c = lax.broadcasted_iota(jnp.int32, (n, n), 1)
    return (r > c) if strict else (r >= c)


def _split_bf16(x):
    hi = x.astype(BF16)
    return hi, (x - hi.astype(F32)).astype(BF16)


def _pair_blockdiag(x, first):
    z = jnp.zeros_like(x)
    return jnp.concatenate([jnp.where(first, x, z), jnp.where(first, z, x)], axis=0)


def _pair_matmul3(a_parts, b_parts, first):
    a_hi, a_lo = a_parts
    bh = _pair_blockdiag(b_parts[0], first)
    bl = _pair_blockdiag(b_parts[1], first)
    lhs = jnp.concatenate([a_hi, a_lo], axis=1)
    rhs = jnp.concatenate([jnp.concatenate([bh, bl], axis=1),
                           jnp.concatenate([bh, jnp.zeros_like(bl)], axis=1)], axis=0)
    r = jnp.dot(lhs, rhs, preferred_element_type=F32)
    w = a_hi.shape[1]
    return r[:, :w] + r[:, w:]


def _pair_unit_lower_inverses(mats, n, first):
    row = lax.broadcasted_iota(jnp.int32, (n, 2 * n), 0)
    col = lax.broadcasted_iota(jnp.int32, (n, 2 * n), 1) & (n - 1)
    eye = (row == col).astype(F32)
    ps = [-a for a in mats]
    ts = [eye + p for p in ps]
    splits = [_split_bf16(p) for p in ps]
    ps = [_pair_matmul3(s, s, first) for s in splits]
    k = 2
    while k < n // 2:
        splits = [_split_bf16(p) for p in ps]
        stacked = []
        for t, s in zip(ts, splits):
            t_hi, t_lo = _split_bf16(t)
            lhs = (jnp.concatenate([t_hi, s[0]], axis=0), jnp.concatenate([t_lo, s[1]], axis=0))
            stacked.append(_pair_matmul3(lhs, s, first))
        ts = [t + r[:n] for t, r in zip(ts, stacked)]
        ps = [r[n:] for r in stacked]
        k *= 2
    return [t + _pair_matmul3(_split_bf16(t), _split_bf16(p), first) for t, p in zip(ts, ps)]


def _gdn_kernel(q_ref, k_ref, v_ref, z_ref, wq_ref, wk_ref, wv_ref, gcol_ref, grow_ref, pcol_ref, prow_ref,
                nw_ref, o_ref, s_ref, wq_s, u_s, oi_s, kd_s, eg_s, *, n_chunks):
    c_len = GDN_CHUNK
    d = HEAD_DIM
    nq = GDN_QK_PER_STEP
    nv = 2 * nq
    wq, wk, wv = wq_ref[...], wk_ref[...], wv_ref[...]
    neg_a_col = -jnp.exp(pcol_ref[:, 0:nv])
    dtb_col = pcol_ref[:, nv:2 * nv]
    neg_a_pair = -jnp.exp(prow_ref[0:nq, :])
    dtb_pair = prow_ref[nq:2 * nq, :]
    pair_row = lax.broadcasted_iota(jnp.int32, (c_len, 2 * c_len), 0)
    pair_lane = lax.broadcasted_iota(jnp.int32, (c_len, 2 * c_len), 1)
    pair_col = pair_lane & (c_len - 1)
    first = pair_lane < c_len
    lower_pair = pair_row >= pair_col
    strict_pair = pair_row > pair_col
    lower_f = _tri(c_len).astype(F32)
    r2 = lax.broadcasted_iota(jnp.int32, (2 * c_len, 2 * c_len), 0)
    c2 = lax.broadcasted_iota(jnp.int32, (2 * c_len, 2 * c_len), 1)
    upper_pair = (((r2 < c_len) == (c2 < c_len)) & ((r2 & (c_len - 1)) <= (c2 & (c_len - 1)))).astype(F32)
    norm_w = nw_ref[...]

    def load_chunk(c):
        r0 = pl.multiple_of(c * c_len, c_len)
        valid_col = (r0 + lax.broadcasted_iota(jnp.int32, (c_len, 1), 0)) >= PAD
        valid_pair = (r0 + pair_col[0:1, :]) >= PAD
        gc = gcol_ref[c]
        g_col = jnp.where(valid_col, neg_a_col * _softplus(gc[:, nv:2 * nv] + dtb_col), 0.0)
        g_pair = jnp.where(valid_pair, neg_a_pair * _softplus(grow_ref[c] + dtb_pair), 0.0)
        return dict(
            c=c,
            q=jnp.where(valid_col, _conv_silu(q_ref, wq, r0, c_len), 0.0),
            k=jnp.where(valid_col, _conv_silu(k_ref, wk, r0, c_len), 0.0),
            v=jnp.where(valid_col, _conv_silu(v_ref, wv, r0, c_len), 0.0),
            beta=jnp.where(valid_col, _sigmoid(gc[:, 0:nv]), 0.0),
            cum_col=_hdot(lower_f, g_col),
            cum_pair=_hdot(g_pair, upper_pair),
        )

    def prepare_step(it, _):
        chunks = [load_chunk(it * GDN_UNROLL + j) for j in range(GDN_UNROLL)]
        probs = [dict(ch=ch, a=a) for ch in chunks for a in range(nq)]
        for pr in probs:
            ch, a = pr["ch"], pr["a"]
            q = ch["q"][:, a * d:(a + 1) * d]
            k = ch["k"][:, a * d:(a + 1) * d]
            pr["qn"] = q * lax.rsqrt(jnp.sum(q * q, axis=-1, keepdims=True) + EPS) * (d ** -0.5)
            pr["kn"] = k * lax.rsqrt(jnp.sum(k * k, axis=-1, keepdims=True) + EPS)
        for pr in probs:
            kn_b = pr["kn"].astype(BF16)
            both = lax.dot_general(jnp.concatenate([kn_b, pr["qn"].astype(BF16)], axis=0),
                                   jnp.concatenate([kn_b, kn_b], axis=0), NT, preferred_element_type=F32)
            pr["kk"], pr["qk"] = both[:c_len], both[c_len:]
        for pr in probs:
            ch, a = pr["ch"], pr["a"]
            h0, h1 = 2 * a, 2 * a + 1
            gcol = jnp.where(first, ch["cum_col"][:, h0:h0 + 1], ch["cum_col"][:, h1:h1 + 1])
            bcol = jnp.where(first, ch["beta"][:, h0:h0 + 1], ch["beta"][:, h1:h1 + 1])
            pr["decay"] = jnp.exp(jnp.where(lower_pair, gcol - ch["cum_pair"][a:a + 1, :], -jnp.inf))
            pr["amat"] = jnp.where(strict_pair, pr["kk"] * bcol * pr["decay"], 0.0)
        ts = _pair_unit_lower_inverses([pr["amat"] for pr in probs], c_len, first)
        for pr, t in zip(probs, ts):
            ch, a = pr["ch"], pr["a"]
            rhs = []
            for vh in (2 * a, 2 * a + 1):
                b1 = ch["beta"][:, vh:vh + 1]
                rhs.append(jnp.concatenate([ch["v"][:, vh * d:(vh + 1) * d] * b1,
                                            pr["kn"] * (b1 * jnp.exp(ch["cum_col"][:, vh:vh + 1]))], axis=1))
            zero = jnp.zeros_like(rhs[0])
            pr["uw"] = _bdot(t, jnp.concatenate([jnp.concatenate([rhs[0], zero], axis=1),
                                                 jnp.concatenate([zero, rhs[1]], axis=1)], axis=0))
        for pr in probs:
            uw = pr["uw"]
            zero = jnp.zeros_like(uw[:, :2 * d])
            pr["ow"] = _bdot(pr["qk"] * pr["decay"], jnp.concatenate(
                [jnp.concatenate([uw[:, :2 * d], zero], axis=1),
                 jnp.concatenate([zero, uw[:, 2 * d:]], axis=1)], axis=0))
        for pr in probs:
            ch, a, uw, ow = pr["ch"], pr["a"], pr["uw"], pr["ow"]
            c = ch["c"]
            for i, vh in enumerate((2 * a, 2 * a + 1)):
                gcol1 = ch["cum_col"][:, vh:vh + 1]
                g_last = ch["cum_col"][c_len - 1:c_len, vh:vh + 1]
                u, w = uw[:, 2 * i * d:(2 * i + 1) * d], uw[:, (2 * i + 1) * d:(2 * i + 2) * d]
                oi, qw = ow[:, 2 * i * d:(2 * i + 1) * d], ow[:, (2 * i + 1) * d:(2 * i + 2) * d]
                wq_s[c, vh, pl.ds(0, c_len), :] = w.astype(BF16)
                wq_s[c, vh, pl.ds(c_len, c_len), :] = (pr["qn"] * jnp.exp(gcol1) - qw).astype(BF16)
                u_s[c, vh] = u
                oi_s[c, vh] = oi
                kd_s[c, vh] = (pr["kn"] * jnp.exp(g_last - gcol1)).astype(BF16)
                eg_s[c, vh] = jnp.broadcast_to(jnp.exp(g_last), (8, d))
        return 0

    lax.fori_loop(0, n_chunks // GDN_UNROLL, prepare_step, 0)

    o_ref[pl.ds(0, c_len), :] = jnp.zeros((c_len, nv * d), o_ref.dtype)
    s_ref[...] = jnp.zeros_like(s_ref)

    def recur(c, _):
        r0 = pl.multiple_of(c * c_len, c_len)
        valid_col = (r0 + lax.broadcasted_iota(jnp.int32, (c_len, 1), 0)) >= PAD
        z_all = z_ref[pl.ds(r0, c_len), :].astype(F32)
        heads = range(nv)
        s = [s_ref[vh] for vh in heads]
        sq = [jnp.dot(wq_s[c, vh], s[vh].astype(BF16), preferred_element_type=F32) for vh in heads]
        v_new = [u_s[c, vh] - sq[vh][:c_len] for vh in heads]
        upd = [_bdot(kd_s[c, vh], v_new[vh], TN) for vh in heads]
        outs = []
        for vh in heads:
            s_ref[vh] = s[vh] * eg_s[c, vh][0:1, :] + upd[vh]
            o = oi_s[c, vh] + sq[vh][c_len:]
            o = o * lax.rsqrt(jnp.mean(o * o, axis=-1, keepdims=True) + EPS) * norm_w
            outs.append(o * _silu(z_all[:, vh * d:(vh + 1) * d]))
        out = jnp.where(valid_col, jnp.concatenate(outs, axis=1), 0.0)
        o_ref[pl.ds(r0, c_len), :] = out.astype(o_ref.dtype)
        return 0

    lax.fori_loop(1, n_chunks, recur, 0)


def gdn_core(qkvz, ba, conv_w, a_log, dt_bias, norm_w, batch):
    m = qkvz.shape[0]
    lp = m // batch
    hk = 16
    d = HEAD_DIM
    nq = GDN_QK_PER_STEP
    nv = 2 * nq
    ng = hk // nq
    n_chunks = lp // GDN_CHUNK
    assert n_chunks % GDN_UNROLL == 0
    qkvz = qkvz.reshape(batch, lp, qkvz.shape[1])
    b_log = ba[:, :2 * hk].reshape(batch, n_chunks, GDN_CHUNK, ng, nv)
    a_log_t = ba[:, 2 * hk:].reshape(batch, n_chunks, GDN_CHUNK, ng, nq, 2)
    gcol = jnp.transpose(jnp.concatenate([b_log, a_log_t.reshape(b_log.shape)], axis=-1), (0, 3, 1, 2, 4))
    grow = jnp.transpose(a_log_t, (0, 3, 1, 4, 5, 2)).reshape(batch, ng, n_chunks, nq, 2 * GDN_CHUNK)
    p = jnp.concatenate([a_log.reshape(ng, nv), dt_bias.reshape(ng, nv)], axis=-1).astype(F32)
    pcol = p.reshape(ng, 1, 2 * nv)
    prow = jnp.concatenate([jnp.repeat(a_log.reshape(ng, nq, 2), GDN_CHUNK, axis=-1),
                            jnp.repeat(dt_bias.reshape(ng, nq, 2), GDN_CHUNK, axis=-1)], axis=1).astype(F32)
    qb, vb = nq * d, nv * d
    kern = functools.partial(_gdn_kernel, n_chunks=n_chunks)
    out = pl.pallas_call(
        kern,
        out_shape=jax.ShapeDtypeStruct((batch, lp, 2 * hk * d), BF16),
        grid=(batch, ng),
        in_specs=[
            pl.BlockSpec((None, lp, qb), lambda b, h: (b, 0, h)),
            pl.BlockSpec((None, lp, qb), lambda b, h: (b, 0, ng + h)),
            pl.BlockSpec((None, lp, vb), lambda b, h: (b, 0, ng + h)),
            pl.BlockSpec((None, lp, vb), lambda b, h: (b, 0, 2 * ng + h)),
            pl.BlockSpec((CONV_W, qb), lambda b, h: (0, h)),
            pl.BlockSpec((CONV_W, qb), lambda b, h: (0, ng + h)),
            pl.BlockSpec((CONV_W, vb), lambda b, h: (0, ng + h)),
            pl.BlockSpec((None, None, n_chunks, GDN_CHUNK, 2 * nv), lambda b, h: (b, h, 0, 0, 0)),
            pl.BlockSpec((None, None, n_chunks, nq, 2 * GDN_CHUNK), lambda b, h: (b, h, 0, 0, 0)),
            pl.BlockSpec((None, 1, 2 * nv), lambda b, h: (h, 0, 0)),
            pl.BlockSpec((None, 2 * nq, 2 * GDN_CHUNK), lambda b, h: (h, 0, 0)),
            pl.BlockSpec((1, d), lambda b, h: (0, 0)),
        ],
        out_specs=pl.BlockSpec((None, lp, vb), lambda b, h: (b, 0, h)),
        scratch_shapes=[
            pltpu.VMEM((nv, d, d), F32),
            pltpu.VMEM((n_chunks, nv, 2 * GDN_CHUNK, d), BF16),
            pltpu.VMEM((n_chunks, nv, GDN_CHUNK, d), F32),
            pltpu.VMEM((n_chunks, nv, GDN_CHUNK, d), F32),
            pltpu.VMEM((n_chunks, nv, GDN_CHUNK, d), BF16),
            pltpu.VMEM((n_chunks, nv, 8, d), F32),
        ],
        compiler_params=_params("parallel", "parallel"),
        name="gdn_core",
    )(qkvz, qkvz, qkvz, qkvz, conv_w, conv_w, conv_w, gcol, grow, pcol, prow, norm_w.reshape(1, d).astype(F32))
    return out.reshape(m, 2 * hk * d)


def _ssd_kernel(z_ref, x_ref, b_ref, c_ref, wx_ref, wb_ref, wc_ref, bx_ref, bb_ref, bc_ref, dcol_ref, drow_ref,
                pcol_ref, prow_ref, nw_ref, o_ref, s_ref, *, n_chunks):
    c_len = SSM_CHUNK
    hp = SSM_HEAD_DIM
    nh = SSM_HPG
    width = nh * hp
    s_ref[...] = jnp.zeros_like(s_ref)

    wx, wb, wc = wx_ref[...], wb_ref[...], wc_ref[...]
    bx, bb, bc = bx_ref[...], bb_ref[...], bc_ref[...]
    neg_a_col = -jnp.exp(pcol_ref[0:1, :])
    dtb_col = pcol_ref[1:2, :]
    d_skip = pcol_ref[2:3, :]
    neg_a_row = -jnp.exp(prow_ref[:, 0:1])
    dtb_row = prow_ref[:, 1:2]
    lower = _tri(c_len)
    lower_f = lower.astype(F32)
    upper_f = (lax.broadcasted_iota(jnp.int32, (c_len, c_len), 0)
               <= lax.broadcasted_iota(jnp.int32, (c_len, c_len), 1)).astype(F32)
    expand = (lax.shift_right_logical(lax.broadcasted_iota(jnp.int32, (nh, width), 1), int(math.log2(hp)))
              == lax.broadcasted_iota(jnp.int32, (nh, width), 0)).astype(F32)
    lane = lax.broadcasted_iota(jnp.int32, (1, 2 * hp), 1)
    d_skip_x = _hdot(d_skip, expand)
    norm_w = nw_ref[...]

    def body(c, _):
        r0 = pl.multiple_of(c * c_len, c_len)
        valid_col = (r0 + lax.broadcasted_iota(jnp.int32, (c_len, 1), 0)) >= PAD
        valid_row = (r0 + lax.broadcasted_iota(jnp.int32, (1, c_len), 1)) >= PAD
        xs = jnp.where(valid_col, _conv_silu(x_ref, wx, r0, c_len, bx), 0.0)
        bm = jnp.where(valid_col, _conv_silu(b_ref, wb, r0, c_len, bb), 0.0)
        cm = jnp.where(valid_col, _conv_silu(c_ref, wc, r0, c_len, bc), 0.0)
        dt_col = jnp.where(valid_col, _softplus(dcol_ref[c] + dtb_col), 0.0)
        dt_row = jnp.where(valid_row, _softplus(drow_ref[c] + dtb_row), 0.0)
        cum_col = _hdot(lower_f, dt_col * neg_a_col)
        cum_row = _hdot(dt_row * neg_a_row, upper_f)
        cum_last = cum_col[c_len - 1:c_len, :]

        xdt = xs * _hdot(dt_col, expand)
        cb = _bdot(cm, bm, NT)
        s = s_ref[...]
        y = _bdot(cm, s) * _hdot(jnp.exp(cum_col), expand)
        s_ref[...] = (s * _hdot(jnp.exp(cum_last), expand)
                      + _bdot(bm, xdt * _hdot(jnp.exp(cum_last - cum_col), expand), TN))
        diag = []
        for pair in range(nh // 2):
            sc = []
            for j in (2 * pair, 2 * pair + 1):
                lmat = jnp.exp(jnp.where(lower, cum_col[:, j:j + 1] - cum_row[j:j + 1, :], -jnp.inf))
                sc.append((cb * lmat).astype(BF16))
            xp = xdt[:, pair * 2 * hp:(pair + 1) * 2 * hp]
            rhs = jnp.concatenate([jnp.where(lane < hp, xp, 0.0), jnp.where(lane >= hp, xp, 0.0)], axis=0)
            diag.append(_bdot(jnp.concatenate(sc, axis=1), rhs))
        y = y + jnp.concatenate(diag, axis=1) + xs * d_skip_x
        y = y * _silu(z_ref[pl.ds(r0, c_len), :].astype(F32))
        y = y * lax.rsqrt(jnp.mean(y * y, axis=-1, keepdims=True) + EPS) * norm_w
        o_ref[pl.ds(r0, c_len), :] = jnp.where(valid_col, y, 0.0).astype(o_ref.dtype)
        return 0

    lax.fori_loop(0, n_chunks, body, 0)


def ssd_core(zx, dt_raw, conv_w, conv_b, a_log, dt_bias, d_skip, norm_w, batch):
    m = zx.shape[0]
    lp = m // batch
    ng = 8
    width = SSM_HPG * SSM_HEAD_DIM
    d_inner = ng * width
    n_chunks = lp // SSM_CHUNK
    zx = zx.reshape(batch, lp, zx.shape[1])
    d5 = dt_raw.reshape(batch, n_chunks, SSM_CHUNK, ng, SSM_HPG)
    dcol = jnp.transpose(d5, (0, 3, 1, 2, 4))
    drow = jnp.transpose(d5, (0, 3, 1, 4, 2))
    p3 = jnp.stack([a_log.reshape(ng, SSM_HPG), dt_bias.reshape(ng, SSM_HPG), d_skip.reshape(ng, SSM_HPG)],
                   axis=1).astype(F32)
    prow = jnp.transpose(p3, (0, 2, 1))
    conv_b = conv_b.reshape(1, -1).astype(F32)
    xo, bo, co = d_inner // width, d_inner // SSM_STATE, (d_inner + ng * SSM_STATE) // SSM_STATE
    kern = functools.partial(_ssd_kernel, n_chunks=n_chunks)
    out = pl.pallas_call(
        kern,
        out_shape=jax.ShapeDtypeStruct((batch, lp, d_inner), BF16),
        grid=(batch, ng),
        in_specs=[
            pl.BlockSpec((None, lp, width), lambda b, g: (b, 0, g)),
            pl.BlockSpec((None, lp, width), lambda b, g: (b, 0, xo + g)),
            pl.BlockSpec((None, lp, SSM_STATE), lambda b, g: (b, 0, 2 * bo + g)),
            pl.BlockSpec((None, lp, SSM_STATE), lambda b, g: (b, 0, bo + co + g)),
            pl.BlockSpec((CONV_W, width), lambda b, g: (0, g)),
            pl.BlockSpec((CONV_W, SSM_STATE), lambda b, g: (0, bo + g)),
            pl.BlockSpec((CONV_W, SSM_STATE), lambda b, g: (0, co + g)),
            pl.BlockSpec((1, width), lambda b, g: (0, g)),
            pl.BlockSpec((1, SSM_STATE), lambda b, g: (0, bo + g)),
            pl.BlockSpec((1, SSM_STATE), lambda b, g: (0, co + g)),
            pl.BlockSpec((None, None, n_chunks, SSM_CHUNK, SSM_HPG), lambda b, g: (b, g, 0, 0, 0)),
            pl.BlockSpec((None, None, n_chunks, SSM_HPG, SSM_CHUNK), lambda b, g: (b, g, 0, 0, 0)),
            pl.BlockSpec((None, 3, SSM_HPG), lambda b, g: (g, 0, 0)),
            pl.BlockSpec((None, SSM_HPG, 3), lambda b, g: (g, 0, 0)),
            pl.BlockSpec((1, width), lambda b, g: (0, g)),
        ],
        out_specs=pl.BlockSpec((None, lp, width), lambda b, g: (b, 0, g)),
        scratch_shapes=[pltpu.VMEM((SSM_STATE, width), F32)],
        compiler_params=_params("parallel", "parallel"),
        name="ssd_core",
    )(zx, zx, zx, zx, conv_w, conv_w, conv_w, conv_b, conv_b, conv_b, dcol, drow, p3, prow,
      norm_w.reshape(1, d_inner).astype(F32))
    return out.reshape(m, d_inner)


def _rope_tables(lp):
    inv = ROPE_THETA ** (-jnp.arange(0, MLA_ROPE, 2, dtype=F32) / MLA_ROPE)
    pos = jnp.maximum(jnp.arange(lp, dtype=F32) - PAD, 0.0)
    ang = pos[:, None] * inv[None, :]
    return jnp.tile(jnp.cos(ang), (1, 4)), jnp.tile(jnp.sin(ang), (1, 4))


def _pair_head(lane):
    return lax.shift_right_logical(lane, 5) & 1


def _rope_pair(t, cos, sin):
    lane = lax.broadcasted_iota(jnp.int32, (1, t.shape[1]), 1)
    partner = pltpu.roll(t, t.shape[1] // 2, axis=1)
    return t * cos + jnp.where(lane < t.shape[1] // 2, -partner, partner) * sin


def _qprep_kernel(q_ref, gn_ref, gp_ref, cos_ref, sin_ref, o_ref, *, n_heads, scale):
    d = HEAD_DIM
    cos, sin = cos_ref[...], sin_ref[...]
    for h in range(n_heads):
        x = q_ref[:, h * d:(h + 1) * d]
        y = x * lax.rsqrt(jnp.mean(x * x, axis=-1, keepdims=True) + EPS) * gn_ref[:, h * d:(h + 1) * d]
        o_ref[:, h * d:(h + 1) * d] = (y * scale).astype(o_ref.dtype)
    r = lax.broadcasted_iota(jnp.int32, (d, d), 0)
    c = lax.broadcasted_iota(jnp.int32, (d, d), 1)
    same_head = (_pair_head(r) == _pair_head(c)).astype(F32)
    base = n_heads * d
    for p in range(n_heads // 2):
        x = q_ref[:, base + p * d:base + (p + 1) * d]
        ms = _hdot(x * x, same_head) * (1.0 / MLA_ROPE)
        y = x * lax.rsqrt(ms + EPS) * gp_ref[:, p * d:(p + 1) * d]
        o_ref[:, base + p * d:base + (p + 1) * d] = (_rope_pair(y, cos, sin) * scale).astype(o_ref.dtype)


def _kprep_kernel(kv_ref, kpe_ref, gn_ref, gp_ref, cos_ref, sin_ref, k_ref, v_ref, pe_ref, *, n_heads):
    d = HEAD_DIM
    for h in range(n_heads):
        x = kv_ref[:, h * d:(h + 1) * d]
        y = x * lax.rsqrt(jnp.mean(x * x, axis=-1, keepdims=True) + EPS) * gn_ref[:, h * d:(h + 1) * d]
        k_ref[:, h * d:(h + 1) * d] = y.astype(k_ref.dtype)
    v_ref[...] = kv_ref[:, n_heads * d:].astype(v_ref.dtype)
    x = kpe_ref[...]
    y = x * lax.rsqrt(jnp.mean(x * x, axis=-1, keepdims=True) + EPS) * gp_ref[...]
    pe_ref[...] = _rope_pair(y, cos_ref[...], sin_ref[...]).astype(pe_ref.dtype)


def _flash_kernel(qn_ref, qpe_ref, kn_ref, kpe_ref, v_ref, o_ref, *, kv_blk):
    d = HEAD_DIM
    blk = ATT_BLK
    lp = kn_ref.shape[0]
    i = pl.program_id(2)
    lane = lax.broadcasted_iota(jnp.int32, (1, d), 1)
    qpe = qpe_ref[...]
    qf = []
    for hh in range(2):
        mine = _pair_head(lane) == hh
        qf.append(jnp.concatenate([qn_ref[:, hh * d:(hh + 1) * d], jnp.where(mine, qpe, jnp.zeros_like(qpe))],
                                  axis=1))
    qpos = i * blk + lax.broadcasted_iota(jnp.int32, (blk, 1), 0)
    q_limit = jnp.maximum(qpos, PAD)

    def body(j, carry):
        c0 = pl.multiple_of(jnp.minimum(j * kv_blk, lp - kv_blk), blk)
        kpos = c0 + lax.broadcasted_iota(jnp.int32, (1, kv_blk), 1)
        ok = (kpos >= jnp.maximum(j * kv_blk, PAD)) & (kpos <= q_limit)
        kpe = kpe_ref[pl.ds(c0, kv_blk), :]
        scores = []
        for hh in range(2):
            kf = jnp.concatenate([kn_ref[pl.ds(c0, kv_blk), hh * d:(hh + 1) * d], kpe], axis=1)
            scores.append(lax.dot_general(qf[hh], kf, NT, preferred_element_type=F32))
        stats = []
        for hh in range(2):
            m, l, _ = carry[hh]
            s = jnp.where(ok, scores[hh], -1e30)
            m_new = jnp.maximum(m, jnp.max(s, axis=-1, keepdims=True))
            p = jnp.exp(s - m_new)
            alpha = jnp.exp(m - m_new)
            stats.append((m_new, alpha * l + jnp.sum(p, axis=-1, keepdims=True), alpha, p.astype(BF16)))
        new = []
        for hh in range(2):
            m_new, l, alpha, p = stats[hh]
            pv = jnp.dot(p, v_ref[pl.ds(c0, kv_blk), hh * d:(hh + 1) * d], preferred_element_type=F32)
            new.append((m_new, l, alpha * carry[hh][2] + pv))
        return tuple(new)

    init = tuple((jnp.full((blk, 1), -1e30, F32), jnp.zeros((blk, 1), F32), jnp.zeros((blk, d), F32))
                 for _ in range(2))
    n_kv = lax.div(i * blk + blk + kv_blk - 1, kv_blk)
    res = lax.fori_loop(0, n_kv, body, init)
    out = jnp.concatenate([acc / l for (_, l, acc) in res], axis=1)
    o_ref[...] = jnp.where(qpos >= PAD, out, 0.0).astype(o_ref.dtype)


def mla_attention(q_raw, kv_raw, kpe_src, kpe_col_block, q_norm, k_norm, batch):
    m = q_raw.shape[0]
    lp = m // batch
    nh = 16
    d = HEAD_DIM
    half = MLA_ROPE // 2
    scale = (d + MLA_ROPE) ** -0.5
    cos, sin = _rope_tables(lp)
    qn_gain = jnp.tile(q_norm[:d], nh).reshape(1, nh * d).astype(F32)
    kn_gain = jnp.tile(k_norm[:d], nh).reshape(1, nh * d).astype(F32)
    qf, qs = q_norm[d:d + half], q_norm[d + half:]
    kf, ks = k_norm[d:d + half], k_norm[d + half:]
    qp_gain = jnp.tile(jnp.concatenate([qf, qf, qs, qs]), nh // 2).reshape(1, nh // 2 * d).astype(F32)
    kp_gain = jnp.concatenate([kf, kf, ks, ks]).reshape(1, d).astype(F32)
    tp = ATT_BLK
    nt = lp // tp
    qw = q_raw.shape[1]

    q_prep = pl.pallas_call(
        functools.partial(_qprep_kernel, n_heads=nh, scale=scale),
        out_shape=jax.ShapeDtypeStruct((batch, lp, qw), BF16),
        grid=(batch, nt),
        in_specs=[
            pl.BlockSpec((None, tp, qw), lambda b, t: (b, t, 0)),
            pl.BlockSpec((1, nh * d), lambda b, t: (0, 0)),
            pl.BlockSpec((1, nh // 2 * d), lambda b, t: (0, 0)),
            pl.BlockSpec((tp, d), lambda b, t: (t, 0)),
            pl.BlockSpec((tp, d), lambda b, t: (t, 0)),
        ],
        out_specs=pl.BlockSpec((None, tp, qw), lambda b, t: (b, t, 0)),
        compiler_params=_params("parallel", "parallel"),
        name="mla_q_prep",
    )(q_raw.reshape(batch, lp, qw), qn_gain, qp_gain, cos, sin)

    kvw = kv_raw.shape[1]
    kpe3 = kpe_src.reshape(batch, lp, kpe_src.shape[1])
    k_prep, v_prep, pe_prep = pl.pallas_call(
        functools.partial(_kprep_kernel, n_heads=nh),
        out_shape=(jax.ShapeDtypeStruct((batch, lp, nh * d), BF16),
                   jax.ShapeDtypeStruct((batch, lp, nh * d), BF16),
                   jax.ShapeDtypeStruct((batch, lp, d), BF16)),
        grid=(batch, nt),
        in_specs=[
            pl.BlockSpec((None, tp, kvw), lambda b, t: (b, t, 0)),
            pl.BlockSpec((None, tp, d), lambda b, t: (b, t, kpe_col_block)),
            pl.BlockSpec((1, nh * d), lambda b, t: (0, 0)),
            pl.BlockSpec((1, d), lambda b, t: (0, 0)),
            pl.BlockSpec((tp, d), lambda b, t: (t, 0)),
            pl.BlockSpec((tp, d), lambda b, t: (t, 0)),
        ],
        out_specs=(pl.BlockSpec((None, tp, nh * d), lambda b, t: (b, t, 0)),
                   pl.BlockSpec((None, tp, nh * d), lambda b, t: (b, t, 0)),
                   pl.BlockSpec((None, tp, d), lambda b, t: (b, t, 0))),
        compiler_params=_params("parallel", "parallel"),
        name="mla_kv_prep",
    )(kv_raw.reshape(batch, lp, kvw), kpe3, kn_gain, kp_gain, cos, sin)

    np_ = nh // 2
    out = pl.pallas_call(
        functools.partial(_flash_kernel, kv_blk=min(ATT_KV_BLK, lp)),
        out_shape=jax.ShapeDtypeStruct((batch, lp, nh * d), BF16),
        grid=(batch, np_, nt),
        in_specs=[
            pl.BlockSpec((None, tp, 2 * d), lambda b, p, i: (b, i, p)),
            pl.BlockSpec((None, tp, d), lambda b, p, i: (b, i, nh + p)),
            pl.BlockSpec((None, lp, 2 * d), lambda b, p, i: (b, 0, p)),
            pl.BlockSpec((None, lp, d), lambda b, p, i: (b, 0, 0)),
            pl.BlockSpec((None, lp, 2 * d), lambda b, p, i: (b, 0, p)),
        ],
        out_specs=pl.BlockSpec((None, tp, 2 * d), lambda b, p, i: (b, i, p)),
        compiler_params=_params("parallel", "parallel", "arbitrary"),
        name="mla_flash",
    )(q_prep, q_prep, k_prep, pe_prep, v_prep)
    return out.reshape(m, nh * d)


def _gdn_layer(h, gain, w_in, conv_w, a_log, dt_bias, norm_w, w_out, batch):
    n_big = w_in.shape[1] - 64
    qkvz = norm_matmul(h, gain, w_in[:, :n_big].astype(BF16), name="gdn_in_proj")
    ba = norm_matmul(h, gain, w_in[:, n_big:].astype(BF16), out_dtype=F32, name="gdn_gate_proj")
    o = gdn_core(qkvz, ba, conv_w.astype(F32), a_log, dt_bias, norm_w, batch)
    return matmul_residual(o, w_out.astype(BF16), h, name="gdn_out_proj")


def _mla_layer(h, gain, w_in, norm_q_lat, norm_kv_lat, w_uq, w_ukv, q_norm, k_norm, w_out, batch):
    nh, d, r = 16, HEAD_DIM, MLA_ROPE
    q_lora, kv_lora = norm_q_lat.shape[0], norm_kv_lat.shape[0]
    half = r // 2
    w_cq, w_ckv, w_pe = w_in[:, :q_lora], w_in[:, q_lora:q_lora + kv_lora], w_in[:, q_lora + kv_lora:]
    w_pe_t = jnp.concatenate([w_pe[:, :half], w_pe[:, :half], w_pe[:, half:], w_pe[:, half:]], axis=1)
    gap = (-(kv_lora + d)) % q_lora
    w_lat = jnp.concatenate([w_ckv, w_pe_t, jnp.zeros((w_in.shape[0], gap), w_in.dtype), w_cq], axis=1)
    lat = norm_matmul(h, gain, w_lat.astype(BF16), out_dtype=F32, name="mla_in_proj")
    wq = w_uq.reshape(q_lora, nh, d + r)
    wq_pe = wq[:, :, d:].reshape(q_lora, nh // 2, 2, 2, half)
    wq_pe = jnp.transpose(wq_pe, (0, 1, 3, 2, 4)).reshape(q_lora, nh // 2 * d)
    wq_all = jnp.concatenate([wq[:, :, :d].reshape(q_lora, nh * d), wq_pe], axis=1)
    q_raw = norm_matmul(lat, norm_q_lat, wq_all.astype(BF16), x_col_block=(kv_lora + d + gap) // q_lora,
                        out_dtype=F32, name="mla_q_up")
    wkv = w_ukv.reshape(kv_lora, nh, 2 * d)
    wkv_all = jnp.concatenate([wkv[:, :, :d].reshape(kv_lora, nh * d), wkv[:, :, d:].reshape(kv_lora, nh * d)],
                              axis=1)
    kv_raw = norm_matmul(lat, norm_kv_lat, wkv_all.astype(BF16), x_col_block=0, out_dtype=F32, name="mla_kv_up")
    o = mla_attention(q_raw, kv_raw, lat, kv_lora // d, q_norm, k_norm, batch)
    return matmul_residual(o, w_out.astype(BF16), h, name="mla_out_proj")


def _ssm_layer(h, gain, w_in, conv_w, conv_b, a_log, dt_bias, d_skip, norm_w, w_out, batch):
    n_big = w_in.shape[1] - 64
    zx = norm_matmul(h, gain, w_in[:, :n_big].astype(BF16), name="ssm_in_proj")
    dt_raw = norm_matmul(h, gain, w_in[:, n_big:].astype(BF16), out_dtype=F32, name="ssm_dt_proj")
    y = ssd_core(zx, dt_raw, conv_w.astype(F32), conv_b, a_log, dt_bias, d_skip, norm_w, batch)
    return matmul_residual(y, w_out.astype(BF16), h, name="ssm_out_proj")


def _mlp_layer(h, gain, w_up, w_down):
    act = norm_matmul(h, gain, w_up.astype(BF16), act="relu2", name="mlp_up")
    return matmul_residual(act, w_down.astype(BF16), h, name="mlp_down")


def kernel(x, meta_tokens, norm_mix, norm_mlp, mlp_w_up, mlp_w_down, gdn_w_in, gdn_conv_w, gdn_a_log, gdn_dt_bias, gdn_norm, gdn_w_out, mla_w_in, mla_norm_q_lat, mla_norm_kv_lat, mla_w_uq, mla_w_ukv, mla_q_norm, mla_k_norm, mla_w_out, ssm_w_in, ssm_conv_w, ssm_conv_b, ssm_a_log, ssm_dt_bias, ssm_d, ssm_norm, ssm_w_out):
    batch, seq, dm = x.shape
    depth = norm_mix.shape[0]
    lp = ROW_TILE + seq
    meta = jnp.broadcast_to(meta_tokens.astype(x.dtype)[None], (batch, N_META, dm))
    h = jnp.concatenate([jnp.zeros((batch, PAD, dm), x.dtype), meta, x], axis=1).reshape(batch * lp, dm)
    ia = ib = ic = 0
    for i in range(depth):
        kind = i % 3
        if kind == 0:
            h = _gdn_layer(h, norm_mix[i], gdn_w_in[ia], gdn_conv_w[ia], gdn_a_log[ia], gdn_dt_bias[ia],
                           gdn_norm[ia], gdn_w_out[ia], batch)
            ia += 1
        elif kind == 1:
            h = _mla_layer(h, norm_mix[i], mla_w_in[ib], mla_norm_q_lat[ib], mla_norm_kv_lat[ib], mla_w_uq[ib],
                           mla_w_ukv[ib], mla_q_norm[ib], mla_k_norm[ib], mla_w_out[ib], batch)
            ib += 1
        else:
            h = _ssm_layer(h, norm_mix[i], ssm_w_in[ic], ssm_conv_w[ic], ssm_conv_b[ic], ssm_a_log[ic],
                           ssm_dt_bias[ic], ssm_d[ic], ssm_norm[ic], ssm_w_out[ic], batch)
            ic += 1
        h = _mlp_layer(h, norm_mlp[i], mlp_w_up[i], mlp_w_down[i])
    return h.reshape(batch, lp, dm)[:, ROW_TILE:]
```

```python
import functools
import math

import jax
import jax.numpy as jnp
from jax import lax
from jax.experimental import pallas as pl
from jax.experimental.pallas import tpu as pltpu

F32 = jnp.float32
BF16 = jnp.bfloat16
HI = lax.Precision.HIGHEST

EPS = 1e-6
N_META = 16
ROW_TILE = 128
PAD = ROW_TILE - N_META
HALO = 16
CONV_W = 4

HEAD_DIM = 128
GDN_CHUNK = 64
SSM_CHUNK = 128
SSM_HEAD_DIM = 64
SSM_HPG = 8
SSM_STATE = 128
MLA_ROPE = 64
ROPE_THETA = 10000.0
ATT_PREP_ROWS = 128
ATT_MAX_Q_ROWS = 320
ATT_KV_BLK = 512
GDN_QK_PER_STEP = 2
GDN_MAX_UNROLL = 3

VMEM_LIMIT = 56 * 1024 * 1024
MAX_ROW_TILE = 1088
MAX_K_TILE = 2048
NT = (((1,), (1,)), ((), ()))
TN = (((0,), (0,)), ((), ()))


def _row_tile(m, cap=MAX_ROW_TILE):
    for t in range(min(m, cap) // 64 * 64, 0, -64):
        if m % t == 0:
            return t
    raise ValueError(f"row count {m} has no tile that is a multiple of 64")


def _col_tile(n, cap=1024):
    if n <= cap:
        return n
    for t in range(cap, 0, -128):
        if n % t == 0:
            return t
    raise ValueError(f"column count {n} has no tile that is a multiple of 128")


def _params(*sem):
    return pltpu.CompilerParams(dimension_semantics=sem, vmem_limit_bytes=VMEM_LIMIT)


def _bdot(a, b, dims=None):
    a = a.astype(BF16)
    b = b.astype(BF16)
    if dims is None:
        return jnp.dot(a, b, preferred_element_type=F32)
    return lax.dot_general(a, b, dims, preferred_element_type=F32)


def _hdot(a, b):
    return jnp.dot(a, b, precision=HI, preferred_element_type=F32)


def _sigmoid(x):
    return 0.5 + 0.5 * jnp.tanh(0.5 * x)


def _silu(x):
    return x * _sigmoid(x)


def _softplus(x):
    return jnp.maximum(x, 0.0) + jnp.log(1.0 + jnp.exp(-jnp.abs(x)))


def _norm_matmul_kernel(x_ref, g_ref, w_ref, *rest, act, sub, side):
    ws_ref = rest[0] if side else None
    o_ref = rest[1] if side else rest[0]
    os_ref = rest[2] if side else None
    xn_ref = rest[-1]

    @pl.when(pl.program_id(1) == 0)
    def _():
        def body(r, _):
            r0 = pl.multiple_of(r * sub, sub)
            x = x_ref[pl.ds(r0, sub), :].astype(F32)
            ms = jnp.mean(x * x, axis=-1, keepdims=True)
            xn_ref[pl.ds(r0, sub), :] = (x * lax.rsqrt(ms + EPS) * g_ref[...]).astype(BF16)
            return 0

        lax.fori_loop(0, x_ref.shape[0] // sub, body, 0)
        if side:
            os_ref[...] = jnp.dot(xn_ref[...], ws_ref[...], preferred_element_type=F32)

    y = jnp.dot(xn_ref[...], w_ref[...], preferred_element_type=F32)
    if act == "relu2":
        y = jnp.square(jnp.maximum(y, 0.0))
    o_ref[...] = y.astype(o_ref.dtype)


def norm_matmul(x, gain, w, *, layer=0, n=None, side_w=None, x_col_block=0, act=None, out_dtype=BF16, name):
    if w.ndim == 2:
        w = w[None]
    m = x.shape[0]
    k = w.shape[1]
    n = w.shape[2] if n is None else n
    tm, tn = _row_tile(m), _col_tile(n)
    side = side_w is not None
    in_specs = [
        pl.BlockSpec((tm, k), lambda i, j: (i, x_col_block)),
        pl.BlockSpec((1, k), lambda i, j: (0, 0)),
        pl.BlockSpec((None, k, tn), lambda i, j: (layer, 0, j)),
    ]
    out_shape = [jax.ShapeDtypeStruct((m, n), out_dtype)]
    out_specs = [pl.BlockSpec((tm, tn), lambda i, j: (i, j))]
    args = [x, gain.reshape(1, k).astype(F32), w]
    if side:
        ns = side_w.shape[1]
        in_specs.append(pl.BlockSpec((k, ns), lambda i, j: (0, 0)))
        out_shape.append(jax.ShapeDtypeStruct((m, ns), F32))
        out_specs.append(pl.BlockSpec((tm, ns), lambda i, j: (i, 0)))
        args.append(side_w)
    out = pl.pallas_call(
        functools.partial(_norm_matmul_kernel, act=act, sub=64, side=side),
        out_shape=out_shape,
        grid=(m // tm, n // tn),
        in_specs=in_specs,
        out_specs=out_specs,
        scratch_shapes=[pltpu.VMEM((tm, k), BF16)],
        compiler_params=_params("parallel", "arbitrary"),
        name=name,
    )(*args)
    return tuple(out) if side else out[0]


def _matmul_residual_kernel(a_ref, w_ref, h_ref, o_ref):
    @pl.when(pl.program_id(2) == 0)
    def _():
        o_ref[...] = h_ref[...]

    o_ref[...] += jnp.dot(a_ref[...], w_ref[...], preferred_element_type=F32)


def matmul_residual(a, w, h, *, layer=0, name):
    if w.ndim == 2:
        w = w[None]
    m, k = a.shape
    n = w.shape[2]
    tm, tn, tk = _row_tile(m), _col_tile(n), _col_tile(k, MAX_K_TILE)
    return pl.pallas_call(
        _matmul_residual_kernel,
        out_shape=jax.ShapeDtypeStruct((m, n), F32),
        grid=(m // tm, n // tn, k // tk),
        in_specs=[
            pl.BlockSpec((tm, tk), lambda i, j, kk: (i, kk)),
            pl.BlockSpec((None, tk, tn), lambda i, j, kk: (layer, kk, j)),
            pl.BlockSpec((tm, tn), lambda i, j, kk: (i, j)),
        ],
        out_specs=pl.BlockSpec((tm, tn), lambda i, j, kk: (i, j)),
        compiler_params=_params("parallel", "parallel", "arbitrary"),
        name=name,
    )(a, w, h)


def _conv_silu(ref, w, r0, rows, bias=None):
    cur = ref[pl.ds(r0, rows), :].astype(F32)
    halo = ref[pl.ds(pl.multiple_of(jnp.maximum(r0 - HALO, 0), HALO), HALO), :].astype(F32)
    x = jnp.concatenate([halo, cur], axis=0)
    y = cur * w[CONV_W - 1:CONV_W, :]
    for j in range(CONV_W - 1):
        y = y + pltpu.roll(x, CONV_W - 1 - j, axis=0)[HALO:, :] * w[j:j + 1, :]
    if bias is not None:
        y = y + bias
    return _silu(y)


def _tri(n, strict=False):
    r = lax.broadcasted_iota(jnp.int32, (n, n), 0)
    c = lax.broadcasted_iota(jnp.int32, (n, n), 1)
    return (r > c) if strict else (r >= c)


def _split_bf16(x):
    hi = x.astype(BF16)
    return hi, (x - hi.astype(F32)).astype(BF16)


def _pair_blockdiag(x, first):
    z = jnp.zeros_like(x)
    return jnp.concatenate([jnp.where(first, x, z), jnp.where(first, z, x)], axis=0)


def _pair_matmul3(a_parts, b_parts, first):
    a_hi, a_lo = a_parts
    bh = _pair_blockdiag(b_parts[0], first)
    bl = _pair_blockdiag(b_parts[1], first)
    lhs = jnp.concatenate([a_hi, a_lo], axis=1)
    rhs = jnp.concatenate([jnp.concatenate([bh, bl], axis=1),
                           jnp.concatenate([bh, jnp.zeros_like(bl)], axis=1)], axis=0)
    r = jnp.dot(lhs, rhs, preferred_element_type=F32)
    w = a_hi.shape[1]
    return r[:, :w] + r[:, w:]


def _pair_unit_lower_inverses(mats, n, first):
    row = lax.broadcasted_iota(jnp.int32, (n, 2 * n), 0)
    col = lax.broadcasted_iota(jnp.int32, (n, 2 * n), 1) & (n - 1)
    eye = (row == col).astype(F32)
    ps = [-a for a in mats]
    ts = [eye + p for p in ps]
    splits = [_split_bf16(p) for p in ps]
    ps = [_pair_matmul3(s, s, first) for s in splits]
    k = 2
    while k < n // 2:
        splits = [_split_bf16(p) for p in ps]
        stacked = []
        for t, s in zip(ts, splits):
            t_hi, t_lo = _split_bf16(t)
            lhs = (jnp.concatenate([t_hi, s[0]], axis=0), jnp.concatenate([t_lo, s[1]], axis=0))
            stacked.append(_pair_matmul3(lhs, s, first))
        ts = [t + r[:n] for t, r in zip(ts, stacked)]
        ps = [r[n:] for r in stacked]
        k *= 2
    return [t + _pair_matmul3(_split_bf16(t), _split_bf16(p), first) for t, p in zip(ts, ps)]


def _gdn_kernel(q_ref, k_ref, v_ref, z_ref, wq_ref, wk_ref, wv_ref, gcol_ref, grow_ref, pcol_ref, prow_ref,
                nw_ref, o_ref, s_ref, wq_s, u_s, oi_s, kd_s, eg_s, *, n_chunks, first_chunk, unroll):
    c_len = GDN_CHUNK
    d = HEAD_DIM
    nq = GDN_QK_PER_STEP
    nv = 2 * nq
    wq, wk, wv = wq_ref[...], wk_ref[...], wv_ref[...]
    neg_a_col = -jnp.exp(pcol_ref[:, 0:nv])
    dtb_col = pcol_ref[:, nv:2 * nv]
    neg_a_pair = -jnp.exp(prow_ref[0:nq, :])
    dtb_pair = prow_ref[nq:2 * nq, :]
    pair_row = lax.broadcasted_iota(jnp.int32, (c_len, 2 * c_len), 0)
    pair_lane = lax.broadcasted_iota(jnp.int32, (c_len, 2 * c_len), 1)
    pair_col = pair_lane & (c_len - 1)
    first = pair_lane < c_len
    lower_pair = pair_row >= pair_col
    strict_pair = pair_row > pair_col
    lower_f = _tri(c_len).astype(F32)
    r2 = lax.broadcasted_iota(jnp.int32, (2 * c_len, 2 * c_len), 0)
    c2 = lax.broadcasted_iota(jnp.int32, (2 * c_len, 2 * c_len), 1)
    upper_pair = (((r2 < c_len) == (c2 < c_len)) & ((r2 & (c_len - 1)) <= (c2 & (c_len - 1)))).astype(F32)
    norm_w = nw_ref[...]

    def load_chunk(c):
        r0 = pl.multiple_of(c * c_len, c_len)
        valid_col = (r0 + lax.broadcasted_iota(jnp.int32, (c_len, 1), 0)) >= PAD
        valid_pair = (r0 + pair_col[0:1, :]) >= PAD
        gc = gcol_ref[c]
        g_col = jnp.where(valid_col, neg_a_col * _softplus(gc[:, nv:2 * nv] + dtb_col), 0.0)
        g_pair = jnp.where(valid_pair, neg_a_pair * _softplus(grow_ref[c] + dtb_pair), 0.0)
        return dict(
            c=c,
            q=jnp.where(valid_col, _conv_silu(q_ref, wq, r0, c_len), 0.0),
            k=jnp.where(valid_col, _conv_silu(k_ref, wk, r0, c_len), 0.0),
            v=jnp.where(valid_col, _conv_silu(v_ref, wv, r0, c_len), 0.0),
            beta=jnp.where(valid_col, _sigmoid(gc[:, 0:nv]), 0.0),
            cum_col=_hdot(lower_f, g_col),
            cum_pair=_hdot(g_pair, upper_pair),
        )

    def prepare_step(it, _):
        chunks = [load_chunk(first_chunk + it * unroll + j) for j in range(unroll)]
        probs = [dict(ch=ch, a=a) for ch in chunks for a in range(nq)]
        for pr in probs:
            ch, a = pr["ch"], pr["a"]
            q = ch["q"][:, a * d:(a + 1) * d]
            k = ch["k"][:, a * d:(a + 1) * d]
            pr["qn"] = q * lax.rsqrt(jnp.sum(q * q, axis=-1, keepdims=True) + EPS) * (d ** -0.5)
            pr["kn"] = k * lax.rsqrt(jnp.sum(k * k, axis=-1, keepdims=True) + EPS)
        for pr in probs:
            kn_b = pr["kn"].astype(BF16)
            both = lax.dot_general(jnp.concatenate([kn_b, pr["qn"].astype(BF16)], axis=0),
                                   jnp.concatenate([kn_b, kn_b], axis=0), NT, preferred_element_type=F32)
            pr["kk"], pr["qk"] = both[:c_len], both[c_len:]
        for pr in probs:
            ch, a = pr["ch"], pr["a"]
            h0, h1 = 2 * a, 2 * a + 1
            gcol = jnp.where(first, ch["cum_col"][:, h0:h0 + 1], ch["cum_col"][:, h1:h1 + 1])
            bcol = jnp.where(first, ch["beta"][:, h0:h0 + 1], ch["beta"][:, h1:h1 + 1])
            pr["decay"] = jnp.exp(jnp.where(lower_pair, gcol - ch["cum_pair"][a:a + 1, :], -jnp.inf))
            pr["amat"] = jnp.where(strict_pair, pr["kk"] * bcol * pr["decay"], 0.0)
        ts = _pair_unit_lower_inverses([pr["amat"] for pr in probs], c_len, first)
        for pr, t in zip(probs, ts):
            ch, a = pr["ch"], pr["a"]
            rhs = []
            for vh in (2 * a, 2 * a + 1):
                b1 = ch["beta"][:, vh:vh + 1]
                rhs.append(jnp.concatenate([ch["v"][:, vh * d:(vh + 1) * d] * b1,
                                            pr["kn"] * (b1 * jnp.exp(ch["cum_col"][:, vh:vh + 1]))], axis=1))
            zero = jnp.zeros_like(rhs[0])
            pr["uw"] = _bdot(t, jnp.concatenate([jnp.concatenate([rhs[0], zero], axis=1),
                                                 jnp.concatenate([zero, rhs[1]], axis=1)], axis=0))
        for pr in probs:
            uw = pr["uw"]
            zero = jnp.zeros_like(uw[:, :2 * d])
            pr["ow"] = _bdot(pr["qk"] * pr["decay"], jnp.concatenate(
                [jnp.concatenate([uw[:, :2 * d], zero], axis=1),
                 jnp.concatenate([zero, uw[:, 2 * d:]], axis=1)], axis=0))
        for pr in probs:
            ch, a, uw, ow = pr["ch"], pr["a"], pr["uw"], pr["ow"]
            c = ch["c"]
            for i, vh in enumerate((2 * a, 2 * a + 1)):
                gcol1 = ch["cum_col"][:, vh:vh + 1]
                g_last = ch["cum_col"][c_len - 1:c_len, vh:vh + 1]
                u, w = uw[:, 2 * i * d:(2 * i + 1) * d], uw[:, (2 * i + 1) * d:(2 * i + 2) * d]
                oi, qw = ow[:, 2 * i * d:(2 * i + 1) * d], ow[:, (2 * i + 1) * d:(2 * i + 2) * d]
                wq_s[c, vh, pl.ds(0, c_len), :] = w.astype(BF16)
                wq_s[c, vh, pl.ds(c_len, c_len), :] = (pr["qn"] * jnp.exp(gcol1) - qw).astype(BF16)
                u_s[c, vh] = u
                oi_s[c, vh] = oi
                kd_s[c, vh] = (pr["kn"] * jnp.exp(g_last - gcol1)).astype(BF16)
                eg_s[c, vh] = jnp.broadcast_to(jnp.exp(g_last), (8, d))
        return 0

    lax.fori_loop(0, (n_chunks - first_chunk) // unroll, prepare_step, 0)

    o_ref[pl.ds(0, c_len), :] = jnp.zeros((c_len, nv * d), o_ref.dtype)
    s_ref[...] = jnp.zeros_like(s_ref)

    def recur(c, _):
        r0 = pl.multiple_of(c * c_len, c_len)
        valid_col = (r0 + lax.broadcasted_iota(jnp.int32, (c_len, 1), 0)) >= PAD
        z_all = z_ref[pl.ds(r0, c_len), :].astype(F32)
        heads = range(nv)
        s = [s_ref[vh] for vh in heads]
        sq = [jnp.dot(wq_s[c, vh], s[vh].astype(BF16), preferred_element_type=F32) for vh in heads]
        v_new = [u_s[c, vh] - sq[vh][:c_len] for vh in heads]
        upd = [_bdot(kd_s[c, vh], v_new[vh], TN) for vh in heads]
        outs = []
        for vh in heads:
            s_ref[vh] = s[vh] * eg_s[c, vh][0:1, :] + upd[vh]
            o = oi_s[c, vh] + sq[vh][c_len:]
            o = o * lax.rsqrt(jnp.mean(o * o, axis=-1, keepdims=True) + EPS) * norm_w
            outs.append(o * _silu(z_all[:, vh * d:(vh + 1) * d]))
        out = jnp.where(valid_col, jnp.concatenate(outs, axis=1), 0.0)
        o_ref[pl.ds(r0, c_len), :] = out.astype(o_ref.dtype)
        return 0

    lax.fori_loop(1, n_chunks, recur, 0)


def gdn_core(qkvz, ba, conv_w, a_log, dt_bias, norm_w, batch):
    m = qkvz.shape[0]
    lp = m // batch
    hk = 16
    d = HEAD_DIM
    nq = GDN_QK_PER_STEP
    nv = 2 * nq
    ng = hk // nq
    n_chunks = lp // GDN_CHUNK
    first_chunk, unroll = next((f, u) for u in range(GDN_MAX_UNROLL, 0, -1) for f in (1, 0)
                               if (n_chunks - f) % u == 0)
    qkvz = qkvz.reshape(batch, lp, qkvz.shape[1])
    b_log = ba[:, :2 * hk].reshape(batch, n_chunks, GDN_CHUNK, ng, nv)
    a_log_t = ba[:, 2 * hk:].reshape(batch, n_chunks, GDN_CHUNK, ng, nq, 2)
    gcol = jnp.transpose(jnp.concatenate([b_log, a_log_t.reshape(b_log.shape)], axis=-1), (0, 3, 1, 2, 4))
    grow = jnp.transpose(a_log_t, (0, 3, 1, 4, 5, 2)).reshape(batch, ng, n_chunks, nq, 2 * GDN_CHUNK)
    p = jnp.concatenate([a_log.reshape(ng, nv), dt_bias.reshape(ng, nv)], axis=-1).astype(F32)
    pcol = p.reshape(ng, 1, 2 * nv)
    prow = jnp.concatenate([jnp.repeat(a_log.reshape(ng, nq, 2), GDN_CHUNK, axis=-1),
                            jnp.repeat(dt_bias.reshape(ng, nq, 2), GDN_CHUNK, axis=-1)], axis=1).astype(F32)
    qb, vb = nq * d, nv * d
    kern = functools.partial(_gdn_kernel, n_chunks=n_chunks, first_chunk=first_chunk, unroll=unroll)
    out = pl.pallas_call(
        kern,
        out_shape=jax.ShapeDtypeStruct((batch, lp, 2 * hk * d), BF16),
        grid=(batch, ng),
        in_specs=[
            pl.BlockSpec((None, lp, qb), lambda b, h: (b, 0, h)),
            pl.BlockSpec((None, lp, qb), lambda b, h: (b, 0, ng + h)),
            pl.BlockSpec((None, lp, vb), lambda b, h: (b, 0, ng + h)),
            pl.BlockSpec((None, lp, vb), lambda b, h: (b, 0, 2 * ng + h)),
            pl.BlockSpec((CONV_W, qb), lambda b, h: (0, h)),
            pl.BlockSpec((CONV_W, qb), lambda b, h: (0, ng + h)),
            pl.BlockSpec((CONV_W, vb), lambda b, h: (0, ng + h)),
            pl.BlockSpec((None, None, n_chunks, GDN_CHUNK, 2 * nv), lambda b, h: (b, h, 0, 0, 0)),
            pl.BlockSpec((None, None, n_chunks, nq, 2 * GDN_CHUNK), lambda b, h: (b, h, 0, 0, 0)),
            pl.BlockSpec((None, 1, 2 * nv), lambda b, h: (h, 0, 0)),
            pl.BlockSpec((None, 2 * nq, 2 * GDN_CHUNK), lambda b, h: (h, 0, 0)),
            pl.BlockSpec((1, d), lambda b, h: (0, 0)),
        ],
        out_specs=pl.BlockSpec((None, lp, vb), lambda b, h: (b, 0, h)),
        scratch_shapes=[
            pltpu.VMEM((nv, d, d), F32),
            pltpu.VMEM((n_chunks, nv, 2 * GDN_CHUNK, d), BF16),
            pltpu.VMEM((n_chunks, nv, GDN_CHUNK, d), F32),
            pltpu.VMEM((n_chunks, nv, GDN_CHUNK, d), F32),
            pltpu.VMEM((n_chunks, nv, GDN_CHUNK, d), BF16),
            pltpu.VMEM((n_chunks, nv, 8, d), F32),
        ],
        compiler_params=_params("parallel", "parallel"),
        name="gdn_core",
    )(qkvz, qkvz, qkvz, qkvz, conv_w, conv_w, conv_w, gcol, grow, pcol, prow, norm_w.reshape(1, d).astype(F32))
    return out.reshape(m, 2 * hk * d)


def _ssd_kernel(z_ref, x_ref, b_ref, c_ref, wx_ref, wb_ref, wc_ref, bx_ref, bb_ref, bc_ref, dcol_ref, drow_ref,
                pcol_ref, prow_ref, nw_ref, o_ref, s_ref, *, n_chunks):
    c_len = SSM_CHUNK
    hp = SSM_HEAD_DIM
    nh = SSM_HPG
    width = nh * hp
    s_ref[...] = jnp.zeros_like(s_ref)

    wx, wb, wc = wx_ref[...], wb_ref[...], wc_ref[...]
    bx, bb, bc = bx_ref[...], bb_ref[...], bc_ref[...]
    neg_a_col = -jnp.exp(pcol_ref[0:1, :])
    dtb_col = pcol_ref[1:2, :]
    d_skip = pcol_ref[2:3, :]
    neg_a_row = -jnp.exp(prow_ref[:, 0:1])
    dtb_row = prow_ref[:, 1:2]
    lower = _tri(c_len)
    lower_f = lower.astype(F32)
    upper_f = (lax.broadcasted_iota(jnp.int32, (c_len, c_len), 0)
               <= lax.broadcasted_iota(jnp.int32, (c_len, c_len), 1)).astype(F32)
    lane = lax.broadcasted_iota(jnp.int32, (1, 2 * hp), 1)

    def spread(cols):
        return jnp.concatenate([jnp.where(lane < hp, cols[:, 2 * t:2 * t + 1], cols[:, 2 * t + 1:2 * t + 2])
                                for t in range(nh // 2)], axis=1)

    d_skip_x = spread(d_skip)
    norm_w = nw_ref[...]

    def body(c, _):
        r0 = pl.multiple_of(c * c_len, c_len)
        valid_col = (r0 + lax.broadcasted_iota(jnp.int32, (c_len, 1), 0)) >= PAD
        valid_row = (r0 + lax.broadcasted_iota(jnp.int32, (1, c_len), 1)) >= PAD
        xs = jnp.where(valid_col, _conv_silu(x_ref, wx, r0, c_len, bx), 0.0)
        bm = jnp.where(valid_col, _conv_silu(b_ref, wb, r0, c_len, bb), 0.0)
        cm = jnp.where(valid_col, _conv_silu(c_ref, wc, r0, c_len, bc), 0.0)
        dt_col = jnp.where(valid_col, _softplus(dcol_ref[c] + dtb_col), 0.0)
        dt_row = jnp.where(valid_row, _softplus(drow_ref[c] + dtb_row), 0.0)
        cum_col = _hdot(lower_f, dt_col * neg_a_col)
        cum_row = _hdot(dt_row * neg_a_row, upper_f)
        cum_last = cum_col[c_len - 1:c_len, :]

        xdt = xs * spread(dt_col)
        cb = _bdot(cm, bm, NT)
        s = s_ref[...]
        y = _bdot(cm, s) * spread(jnp.exp(cum_col))
        s_ref[...] = s * spread(jnp.exp(cum_last)) + _bdot(bm, xdt * spread(jnp.exp(cum_last - cum_col)), TN)
        diag = []
        for pair in range(nh // 2):
            sc = []
            for j in (2 * pair, 2 * pair + 1):
                lmat = jnp.exp(jnp.where(lower, cum_col[:, j:j + 1] - cum_row[j:j + 1, :], -jnp.inf))
                sc.append((cb * lmat).astype(BF16))
            xp = xdt[:, pair * 2 * hp:(pair + 1) * 2 * hp]
            rhs = jnp.concatenate([jnp.where(lane < hp, xp, 0.0), jnp.where(lane >= hp, xp, 0.0)], axis=0)
            diag.append(_bdot(jnp.concatenate(sc, axis=1), rhs))
        y = y + jnp.concatenate(diag, axis=1) + xs * d_skip_x
        y = y * _silu(z_ref[pl.ds(r0, c_len), :].astype(F32))
        y = y * lax.rsqrt(jnp.mean(y * y, axis=-1, keepdims=True) + EPS) * norm_w
        o_ref[pl.ds(r0, c_len), :] = jnp.where(valid_col, y, 0.0).astype(o_ref.dtype)
        return 0

    lax.fori_loop(0, n_chunks, body, 0)


def ssd_core(zx, dt_raw, conv_w, conv_b, a_log, dt_bias, d_skip, norm_w, batch):
    m = zx.shape[0]
    lp = m // batch
    ng = 8
    width = SSM_HPG * SSM_HEAD_DIM
    d_inner = ng * width
    n_chunks = lp // SSM_CHUNK
    zx = zx.reshape(batch, lp, zx.shape[1])
    d5 = dt_raw.reshape(batch, n_chunks, SSM_CHUNK, ng, SSM_HPG)
    dcol = jnp.transpose(d5, (0, 3, 1, 2, 4))
    drow = jnp.transpose(d5, (0, 3, 1, 4, 2))
    p3 = jnp.stack([a_log.reshape(ng, SSM_HPG), dt_bias.reshape(ng, SSM_HPG), d_skip.reshape(ng, SSM_HPG)],
                   axis=1).astype(F32)
    prow = jnp.transpose(p3, (0, 2, 1))
    conv_b = conv_b.reshape(1, -1).astype(F32)
    xo, bo, co = d_inner // width, d_inner // SSM_STATE, (d_inner + ng * SSM_STATE) // SSM_STATE
    kern = functools.partial(_ssd_kernel, n_chunks=n_chunks)
    out = pl.pallas_call(
        kern,
        out_shape=jax.ShapeDtypeStruct((batch, lp, d_inner), BF16),
        grid=(batch, ng),
        in_specs=[
            pl.BlockSpec((None, lp, width), lambda b, g: (b, 0, g)),
            pl.BlockSpec((None, lp, width), lambda b, g: (b, 0, xo + g)),
            pl.BlockSpec((None, lp, SSM_STATE), lambda b, g: (b, 0, 2 * bo + g)),
            pl.BlockSpec((None, lp, SSM_STATE), lambda b, g: (b, 0, bo + co + g)),
            pl.BlockSpec((CONV_W, width), lambda b, g: (0, g)),
            pl.BlockSpec((CONV_W, SSM_STATE), lambda b, g: (0, bo + g)),
            pl.BlockSpec((CONV_W, SSM_STATE), lambda b, g: (0, co + g)),
            pl.BlockSpec((1, width), lambda b, g: (0, g)),
            pl.BlockSpec((1, SSM_STATE), lambda b, g: (0, bo + g)),
            pl.BlockSpec((1, SSM_STATE), lambda b, g: (0, co + g)),
            pl.BlockSpec((None, None, n_chunks, SSM_CHUNK, SSM_HPG), lambda b, g: (b, g, 0, 0, 0)),
            pl.BlockSpec((None, None, n_chunks, SSM_HPG, SSM_CHUNK), lambda b, g: (b, g, 0, 0, 0)),
            pl.BlockSpec((None, 3, SSM_HPG), lambda b, g: (g, 0, 0)),
            pl.BlockSpec((None, SSM_HPG, 3), lambda b, g: (g, 0, 0)),
            pl.BlockSpec((1, width), lambda b, g: (0, g)),
        ],
        out_specs=pl.BlockSpec((None, lp, width), lambda b, g: (b, 0, g)),
        scratch_shapes=[pltpu.VMEM((SSM_STATE, width), F32)],
        compiler_params=_params("parallel", "parallel"),
        name="ssd_core",
    )(zx, zx, zx, zx, conv_w, conv_w, conv_w, conv_b, conv_b, conv_b, dcol, drow, p3, prow,
      norm_w.reshape(1, d_inner).astype(F32))
    return out.reshape(m, d_inner)


def _rope_tables(lp):
    inv = ROPE_THETA ** (-jnp.arange(0, MLA_ROPE, 2, dtype=F32) / MLA_ROPE)
    pos = jnp.maximum(jnp.arange(lp, dtype=F32) - PAD, 0.0)
    ang = pos[:, None] * inv[None, :]
    return jnp.tile(jnp.cos(ang), (1, 4)), jnp.tile(jnp.sin(ang), (1, 4))


def _pair_head(lane):
    return lax.shift_right_logical(lane, 5) & 1


def _rope_pair(t, cos, sin):
    lane = lax.broadcasted_iota(jnp.int32, (1, t.shape[1]), 1)
    partner = pltpu.roll(t, t.shape[1] // 2, axis=1)
    return t * cos + jnp.where(lane < t.shape[1] // 2, -partner, partner) * sin


def _qprep_kernel(q_ref, gn_ref, gp_ref, cos_ref, sin_ref, o_ref, *, n_heads, scale):
    d = HEAD_DIM
    cos, sin = cos_ref[...], sin_ref[...]
    for h in range(n_heads):
        x = q_ref[:, h * d:(h + 1) * d]
        y = x * lax.rsqrt(jnp.mean(x * x, axis=-1, keepdims=True) + EPS) * gn_ref[:, h * d:(h + 1) * d]
        o_ref[:, h * d:(h + 1) * d] = (y * scale).astype(o_ref.dtype)
    r = lax.broadcasted_iota(jnp.int32, (d, d), 0)
    c = lax.broadcasted_iota(jnp.int32, (d, d), 1)
    same_head = (_pair_head(r) == _pair_head(c)).astype(F32)
    base = n_heads * d
    for p in range(n_heads // 2):
        x = q_ref[:, base + p * d:base + (p + 1) * d]
        ms = _hdot(x * x, same_head) * (1.0 / MLA_ROPE)
        y = x * lax.rsqrt(ms + EPS) * gp_ref[:, p * d:(p + 1) * d]
        o_ref[:, base + p * d:base + (p + 1) * d] = (_rope_pair(y, cos, sin) * scale).astype(o_ref.dtype)


def _kprep_kernel(kv_ref, kpe_ref, gn_ref, gp_ref, cos_ref, sin_ref, k_ref, v_ref, pe_ref, *, n_heads):
    d = HEAD_DIM
    for h in range(n_heads):
        x = kv_ref[:, h * d:(h + 1) * d]
        y = x * lax.rsqrt(jnp.mean(x * x, axis=-1, keepdims=True) + EPS) * gn_ref[:, h * d:(h + 1) * d]
        k_ref[:, h * d:(h + 1) * d] = y.astype(k_ref.dtype)
    v_ref[...] = kv_ref[:, n_heads * d:].astype(v_ref.dtype)
    x = kpe_ref[...]
    y = x * lax.rsqrt(jnp.mean(x * x, axis=-1, keepdims=True) + EPS) * gp_ref[...]
    pe_ref[...] = _rope_pair(y, cos_ref[...], sin_ref[...]).astype(pe_ref.dtype)


def _flash_kernel(qn_ref, qpe_ref, kn_ref, kpe_ref, v_ref, o_ref, *, kv_blk):
    d = HEAD_DIM
    blk = qn_ref.shape[0]
    lp = kn_ref.shape[0]
    i = pl.program_id(2)
    lane = lax.broadcasted_iota(jnp.int32, (1, d), 1)
    qpe = qpe_ref[...]
    qf = []
    for hh in range(2):
        mine = _pair_head(lane) == hh
        qf.append(jnp.concatenate([qn_ref[:, hh * d:(hh + 1) * d], jnp.where(mine, qpe, jnp.zeros_like(qpe))],
                                  axis=1))
    qpos = i * blk + lax.broadcasted_iota(jnp.int32, (blk, 1), 0)
    q_limit = jnp.maximum(qpos, PAD)

    def body(j, carry):
        c0 = pl.multiple_of(jnp.minimum(j * kv_blk, lp - kv_blk), ROW_TILE)
        kpos = c0 + lax.broadcasted_iota(jnp.int32, (1, kv_blk), 1)
        ok = (kpos >= jnp.maximum(j * kv_blk, PAD)) & (kpos <= q_limit)
        kpe = kpe_ref[pl.ds(c0, kv_blk), :]
        scores = []
        for hh in range(2):
            kf = jnp.concatenate([kn_ref[pl.ds(c0, kv_blk), hh * d:(hh + 1) * d], kpe], axis=1)
            scores.append(lax.dot_general(qf[hh], kf, NT, preferred_element_type=F32))
        stats = []
        for hh in range(2):
            m, l, _ = carry[hh]
            s = jnp.where(ok, scores[hh], -1e30)
            m_new = jnp.maximum(m, jnp.max(s, axis=-1, keepdims=True))
            p = jnp.exp(s - m_new)
            alpha = jnp.exp(m - m_new)
            stats.append((m_new, alpha * l + jnp.sum(p, axis=-1, keepdims=True), alpha, p.astype(BF16)))
        new = []
        for hh in range(2):
            m_new, l, alpha, p = stats[hh]
            pv = jnp.dot(p, v_ref[pl.ds(c0, kv_blk), hh * d:(hh + 1) * d], preferred_element_type=F32)
            new.append((m_new, l, alpha * carry[hh][2] + pv))
        return tuple(new)

    init = tuple((jnp.full((blk, 1), -1e30, F32), jnp.zeros((blk, 1), F32), jnp.zeros((blk, d), F32))
                 for _ in range(2))
    n_kv = lax.div(i * blk + blk + kv_blk - 1, kv_blk)
    res = lax.fori_loop(0, n_kv, body, init)
    out = jnp.concatenate([acc / l for (_, l, acc) in res], axis=1)
    o_ref[...] = jnp.where(qpos >= PAD, out, 0.0).astype(o_ref.dtype)


def mla_attention(q_raw, kv_raw, kpe_src, kpe_col_block, q_norm, k_norm, batch):
    m = q_raw.shape[0]
    lp = m // batch
    nh = 16
    d = HEAD_DIM
    half = MLA_ROPE // 2
    scale = (d + MLA_ROPE) ** -0.5
    cos, sin = _rope_tables(lp)
    qn_gain = jnp.tile(q_norm[:d], nh).reshape(1, nh * d).astype(F32)
    kn_gain = jnp.tile(k_norm[:d], nh).reshape(1, nh * d).astype(F32)
    qf, qs = q_norm[d:d + half], q_norm[d + half:]
    kf, ks = k_norm[d:d + half], k_norm[d + half:]
    qp_gain = jnp.tile(jnp.concatenate([qf, qf, qs, qs]), nh // 2).reshape(1, nh // 2 * d).astype(F32)
    kp_gain = jnp.concatenate([kf, kf, ks, ks]).reshape(1, d).astype(F32)
    tp = ATT_PREP_ROWS
    nt = lp // tp
    qw = q_raw.shape[1]

    q_prep = pl.pallas_call(
        functools.partial(_qprep_kernel, n_heads=nh, scale=scale),
        out_shape=jax.ShapeDtypeStruct((batch, lp, qw), BF16),
        grid=(batch, nt),
        in_specs=[
            pl.BlockSpec((None, tp, qw), lambda b, t: (b, t, 0)),
            pl.BlockSpec((1, nh * d), lambda b, t: (0, 0)),
            pl.BlockSpec((1, nh // 2 * d), lambda b, t: (0, 0)),
            pl.BlockSpec((tp, d), lambda b, t: (t, 0)),
            pl.BlockSpec((tp, d), lambda b, t: (t, 0)),
        ],
        out_specs=pl.BlockSpec((None, tp, qw), lambda b, t: (b, t, 0)),
        compiler_params=_params("parallel", "parallel"),
        name="mla_q_prep",
    )(q_raw.reshape(batch, lp, qw), qn_gain, qp_gain, cos, sin)

    kvw = kv_raw.shape[1]
    kpe3 = kpe_src.reshape(batch, lp, kpe_src.shape[1])
    k_prep, v_prep, pe_prep = pl.pallas_call(
        functools.partial(_kprep_kernel, n_heads=nh),
        out_shape=(jax.ShapeDtypeStruct((batch, lp, nh * d), BF16),
                   jax.ShapeDtypeStruct((batch, lp, nh * d), BF16),
                   jax.ShapeDtypeStruct((batch, lp, d), BF16)),
        grid=(batch, nt),
        in_specs=[
            pl.BlockSpec((None, tp, kvw), lambda b, t: (b, t, 0)),
            pl.BlockSpec((None, tp, d), lambda b, t: (b, t, kpe_col_block)),
            pl.BlockSpec((1, nh * d), lambda b, t: (0, 0)),
            pl.BlockSpec((1, d), lambda b, t: (0, 0)),
            pl.BlockSpec((tp, d), lambda b, t: (t, 0)),
            pl.BlockSpec((tp, d), lambda b, t: (t, 0)),
        ],
        out_specs=(pl.BlockSpec((None, tp, nh * d), lambda b, t: (b, t, 0)),
                   pl.BlockSpec((None, tp, nh * d), lambda b, t: (b, t, 0)),
                   pl.BlockSpec((None, tp, d), lambda b, t: (b, t, 0))),
        compiler_params=_params("parallel", "parallel"),
        name="mla_kv_prep",
    )(kv_raw.reshape(batch, lp, kvw), kpe3, kn_gain, kp_gain, cos, sin)

    np_ = nh // 2
    tq = next(t for t in range(min(lp, ATT_MAX_Q_ROWS) // 16 * 16, 0, -16) if lp % t == 0)
    out = pl.pallas_call(
        functools.partial(_flash_kernel, kv_blk=min(ATT_KV_BLK, lp)),
        out_shape=jax.ShapeDtypeStruct((batch, lp, nh * d), BF16),
        grid=(batch, np_, lp // tq),
        in_specs=[
            pl.BlockSpec((None, tq, 2 * d), lambda b, p, i: (b, i, p)),
            pl.BlockSpec((None, tq, d), lambda b, p, i: (b, i, nh + p)),
            pl.BlockSpec((None, lp, 2 * d), lambda b, p, i: (b, 0, p)),
            pl.BlockSpec((None, lp, d), lambda b, p, i: (b, 0, 0)),
            pl.BlockSpec((None, lp, 2 * d), lambda b, p, i: (b, 0, p)),
        ],
        out_specs=pl.BlockSpec((None, tq, 2 * d), lambda b, p, i: (b, i, p)),
        compiler_params=_params("parallel", "parallel", "arbitrary"),
        name="mla_flash",
    )(q_prep, q_prep, k_prep, pe_prep, v_prep)
    return out.reshape(m, nh * d)


def _gdn_layer(h, gain, w_in_all, layer, w_gate, conv_w, a_log, dt_bias, norm_w, w_out_all, batch):
    qkvz, ba = norm_matmul(h, gain, w_in_all, layer=layer, n=w_in_all.shape[2] - w_gate.shape[1],
                           side_w=w_gate.astype(BF16), name="gdn_in_proj")
    o = gdn_core(qkvz, ba, conv_w.astype(F32), a_log, dt_bias, norm_w, batch)
    return matmul_residual(o, w_out_all, h, layer=layer, name="gdn_out_proj")


def _mla_layer(h, gain, w_in, norm_q_lat, norm_kv_lat, w_uq, w_ukv, q_norm, k_norm, w_out_all, layer, batch):
    nh, d, r = 16, HEAD_DIM, MLA_ROPE
    q_lora, kv_lora = norm_q_lat.shape[0], norm_kv_lat.shape[0]
    half = r // 2
    w_cq, w_ckv, w_pe = w_in[:, :q_lora], w_in[:, q_lora:q_lora + kv_lora], w_in[:, q_lora + kv_lora:]
    w_pe_t = jnp.concatenate([w_pe[:, :half], w_pe[:, :half], w_pe[:, half:], w_pe[:, half:]], axis=1)
    gap = (-(kv_lora + d)) % q_lora
    w_lat = jnp.concatenate([w_ckv, w_pe_t, jnp.zeros((w_in.shape[0], gap), w_in.dtype), w_cq], axis=1)
    lat = norm_matmul(h, gain, w_lat.astype(BF16), out_dtype=F32, name="mla_in_proj")
    wq = w_uq.reshape(q_lora, nh, d + r)
    wq_pe = wq[:, :, d:].reshape(q_lora, nh // 2, 2, 2, half)
    wq_pe = jnp.transpose(wq_pe, (0, 1, 3, 2, 4)).reshape(q_lora, nh // 2 * d)
    wq_all = jnp.concatenate([wq[:, :, :d].reshape(q_lora, nh * d), wq_pe], axis=1)
    q_raw = norm_matmul(lat, norm_q_lat, wq_all.astype(BF16), x_col_block=(kv_lora + d + gap) // q_lora,
                        out_dtype=F32, name="mla_q_up")
    wkv = w_ukv.reshape(kv_lora, nh, 2 * d)
    wkv_all = jnp.concatenate([wkv[:, :, :d].reshape(kv_lora, nh * d), wkv[:, :, d:].reshape(kv_lora, nh * d)],
                              axis=1)
    kv_raw = norm_matmul(lat, norm_kv_lat, wkv_all.astype(BF16), x_col_block=0, out_dtype=F32, name="mla_kv_up")
    o = mla_attention(q_raw, kv_raw, lat, kv_lora // d, q_norm, k_norm, batch)
    return matmul_residual(o, w_out_all, h, layer=layer, name="mla_out_proj")


def _ssm_layer(h, gain, w_in_all, layer, w_dt, conv_w, conv_b, a_log, dt_bias, d_skip, norm_w, w_out_all, batch):
    zx, dt_raw = norm_matmul(h, gain, w_in_all, layer=layer, n=w_in_all.shape[2] - w_dt.shape[1],
                             side_w=w_dt.astype(BF16), name="ssm_in_proj")
    y = ssd_core(zx, dt_raw, conv_w.astype(F32), conv_b, a_log, dt_bias, d_skip, norm_w, batch)
    return matmul_residual(y, w_out_all, h, layer=layer, name="ssm_out_proj")


def _mlp_layer(h, gain, w_up_all, w_down_all, layer):
    act = norm_matmul(h, gain, w_up_all, layer=layer, act="relu2", name="mlp_up")
    return matmul_residual(act, w_down_all, h, layer=layer, name="mlp_down")


def kernel(x, meta_tokens, norm_mix, norm_mlp, mlp_w_up, mlp_w_down, gdn_w_in, gdn_conv_w, gdn_a_log, gdn_dt_bias, gdn_norm, gdn_w_out, mla_w_in, mla_norm_q_lat, mla_norm_kv_lat, mla_w_uq, mla_w_ukv, mla_q_norm, mla_k_norm, mla_w_out, ssm_w_in, ssm_conv_w, ssm_conv_b, ssm_a_log, ssm_dt_bias, ssm_d, ssm_norm, ssm_w_out):
    batch, seq, dm = x.shape
    depth = norm_mix.shape[0]
    lp = ROW_TILE + seq
    meta = jnp.broadcast_to(meta_tokens.astype(x.dtype)[None], (batch, N_META, dm))
    h = jnp.concatenate([jnp.zeros((batch, PAD, dm), x.dtype), meta, x], axis=1).reshape(batch * lp, dm)
    up_b, down_b = mlp_w_up.astype(BF16), mlp_w_down.astype(BF16)
    gdn_in_b, gdn_out_b = gdn_w_in.astype(BF16), gdn_w_out.astype(BF16)
    ssm_in_b, ssm_out_b = ssm_w_in.astype(BF16), ssm_w_out.astype(BF16)
    mla_out_b = mla_w_out.astype(BF16)
    n_gate = 2 * gdn_a_log.shape[1]
    n_dt = ssm_a_log.shape[1]
    ia = ib = ic = 0
    for i in range(depth):
        kind = i % 3
        if kind == 0:
            h = _gdn_layer(h, norm_mix[i], gdn_in_b, ia, gdn_w_in[ia][:, gdn_w_in.shape[2] - n_gate:], gdn_conv_w[ia],
                           gdn_a_log[ia], gdn_dt_bias[ia], gdn_norm[ia], gdn_out_b, batch)
            ia += 1
        elif kind == 1:
            h = _mla_layer(h, norm_mix[i], mla_w_in[ib], mla_norm_q_lat[ib], mla_norm_kv_lat[ib], mla_w_uq[ib],
                           mla_w_ukv[ib], mla_q_norm[ib], mla_k_norm[ib], mla_out_b, ib, batch)
            ib += 1
        else:
            h = _ssm_layer(h, norm_mix[i], ssm_in_b, ic, ssm_w_in[ic][:, ssm_w_in.shape[2] - n_dt:], ssm_conv_w[ic],
                           ssm_conv_b[ic], ssm_a_log[ic], ssm_dt_bias[ic], ssm_d[ic], ssm_norm[ic], ssm_out_b, batch)
            ic += 1
        h = _mlp_layer(h, norm_mlp[i], up_b, down_b, i)
    return h.reshape(batch, lp, dm)[:, ROW_TILE:]
```

```python
import functools
import math

import jax
import jax.numpy as jnp
from jax import lax
from jax.experimental import pallas as pl
from jax.experimental.pallas import tpu as pltpu

F32 = jnp.float32
BF16 = jnp.bfloat16
HI = lax.Precision.HIGHEST

EPS = 1e-6
N_META = 16
ROW_TILE = 128
PAD = ROW_TILE - N_META
LANES = 128
HALO = 16
CONV_W = 4

HEAD_DIM = 128
GDN_CHUNK = 64
SSM_CHUNK = 128
SSM_HEAD_DIM = 64
SSM_HPG = 8
SSM_STATE = 128
MLA_ROPE = 64
ROPE_THETA = 10000.0
ATT_PREP_ROWS = 128
ATT_MAX_Q_ROWS = 320
ATT_KV_BLK = 512
GDN_QK_PER_STEP = 2
GDN_MAX_UNROLL = 3

VMEM_LIMIT = 56 * 1024 * 1024
MAX_ROW_TILE = 1088
MAX_K_TILE = 2048
NT = (((1,), (1,)), ((), ()))
TN = (((0,), (0,)), ((), ()))


def _row_tile(m, cap=MAX_ROW_TILE):
    for t in range(min(m, cap) // 64 * 64, 0, -64):
        if m % t == 0:
            return t
    raise ValueError(f"row count {m} has no tile that is a multiple of 64")


def _col_tile(n, cap=1024):
    if n <= cap:
        return n
    for t in range(cap, 0, -128):
        if n % t == 0:
            return t
    raise ValueError(f"column count {n} has no tile that is a multiple of 128")


def _params(*sem):
    return pltpu.CompilerParams(dimension_semantics=sem, vmem_limit_bytes=VMEM_LIMIT)


def _bdot(a, b, dims=None):
    a = a.astype(BF16)
    b = b.astype(BF16)
    if dims is None:
        return jnp.dot(a, b, preferred_element_type=F32)
    return lax.dot_general(a, b, dims, preferred_element_type=F32)


def _hdot(a, b):
    return jnp.dot(a, b, precision=HI, preferred_element_type=F32)


def _sigmoid(x):
    return 0.5 + 0.5 * jnp.tanh(0.5 * x)


def _silu(x):
    return x * _sigmoid(x)


def _softplus(x):
    return jnp.maximum(x, 0.0) + jnp.log(1.0 + jnp.exp(-jnp.abs(x)))


def _norm_matmul_kernel(x_ref, g_ref, w_ref, o_ref, xn_ref, *, sub):
    @pl.when(pl.program_id(1) == 0)
    def _():
        def body(r, _):
            r0 = pl.multiple_of(r * sub, sub)
            x = x_ref[pl.ds(r0, sub), :].astype(F32)
            ms = jnp.mean(x * x, axis=-1, keepdims=True)
            xn_ref[pl.ds(r0, sub), :] = (x * lax.rsqrt(ms + EPS) * g_ref[...]).astype(BF16)
            return 0

        lax.fori_loop(0, x_ref.shape[0] // sub, body, 0)

    o_ref[...] = jnp.dot(xn_ref[...], w_ref[...], preferred_element_type=F32).astype(o_ref.dtype)


def norm_matmul(x, gain, w, *, x_col_block=0, out_dtype=BF16, name):
    m = x.shape[0]
    k, n = w.shape
    tm, tn = _row_tile(m), _col_tile(n)
    return pl.pallas_call(
        functools.partial(_norm_matmul_kernel, sub=64),
        out_shape=jax.ShapeDtypeStruct((m, n), out_dtype),
        grid=(m // tm, n // tn),
        in_specs=[
            pl.BlockSpec((tm, k), lambda i, j: (i, x_col_block)),
            pl.BlockSpec((1, k), lambda i, j: (0, 0)),
            pl.BlockSpec((k, tn), lambda i, j: (0, j)),
        ],
        out_specs=pl.BlockSpec((tm, tn), lambda i, j: (i, j)),
        scratch_shapes=[pltpu.VMEM((tm, k), BF16)],
        compiler_params=_params("parallel", "arbitrary"),
        name=name,
    )(x, gain.reshape(1, k).astype(F32), w)


def _stream_stats(h_ref, hb_ref, ss_ref, sub):
    def body(r, _):
        r0 = pl.multiple_of(r * sub, sub)
        x = h_ref[pl.ds(r0, sub), :]
        hb_ref[pl.ds(r0, sub), :] = x.astype(BF16)
        ss_ref[pl.ds(r0, sub), :] = jnp.broadcast_to(jnp.sum(x * x, axis=-1, keepdims=True), (sub, ss_ref.shape[1]))
        return 0

    lax.fori_loop(0, h_ref.shape[0] // sub, body, 0)


def _matmul_residual_kernel(a_ref, w_ref, h_ref, o_ref, hb_ref, ss_ref, *, sub):
    @pl.when(pl.program_id(2) == 0)
    def _():
        o_ref[...] = h_ref[...]

    o_ref[...] += jnp.dot(a_ref[...], w_ref[...], preferred_element_type=F32)

    @pl.when(pl.program_id(2) == pl.num_programs(2) - 1)
    def _():
        _stream_stats(o_ref, hb_ref, ss_ref, sub)


def _stream_out(m, n, tm, tn, index):
    shapes = (jax.ShapeDtypeStruct((m, n), F32), jax.ShapeDtypeStruct((m, n), BF16),
              jax.ShapeDtypeStruct((n // tn, m, LANES), F32))
    specs = (pl.BlockSpec((tm, tn), lambda *g: index(*g)),
             pl.BlockSpec((tm, tn), lambda *g: index(*g)),
             pl.BlockSpec((None, tm, LANES), lambda *g: (index(*g)[1], index(*g)[0], 0)))
    return shapes, specs


def matmul_residual(a, w, h, *, layer=0, name):
    if w.ndim == 2:
        w = w[None]
    m, k = a.shape
    n = w.shape[2]
    tm, tn, tk = _row_tile(m), _col_tile(n), _col_tile(k, MAX_K_TILE)
    shapes, specs = _stream_out(m, n, tm, tn, lambda i, j, kk: (i, j))
    return pl.pallas_call(
        functools.partial(_matmul_residual_kernel, sub=64),
        out_shape=shapes,
        grid=(m // tm, n // tn, k // tk),
        in_specs=[
            pl.BlockSpec((tm, tk), lambda i, j, kk: (i, kk)),
            pl.BlockSpec((None, tk, tn), lambda i, j, kk: (layer, kk, j)),
            pl.BlockSpec((tm, tn), lambda i, j, kk: (i, j)),
        ],
        out_specs=specs,
        compiler_params=_params("parallel", "parallel", "arbitrary"),
        name=name,
    )(a, w, h)


def _stream_entry_kernel(h_ref, hb_ref, ss_ref, *, sub):
    _stream_stats(h_ref, hb_ref, ss_ref, sub)


def stream_entry(h):
    m, n = h.shape
    tm, tn = _row_tile(m), _col_tile(n)
    shapes, specs = _stream_out(m, n, tm, tn, lambda i, j: (i, j))
    return pl.pallas_call(
        functools.partial(_stream_entry_kernel, sub=64),
        out_shape=shapes[1:],
        grid=(m // tm, n // tn),
        in_specs=[pl.BlockSpec((tm, tn), lambda i, j: (i, j))],
        out_specs=specs[1:],
        compiler_params=_params("parallel", "parallel"),
        name="stream_entry",
    )(h)


def _scaled_matmul_kernel(x_ref, ss_ref, g_ref, w_ref, o_ref, wb_ref, *, act, inv_k, sub):
    @pl.when(pl.program_id(1) == 0)
    def _():
        def body(r, _):
            r0 = pl.multiple_of(r * sub, sub)
            wb_ref[pl.ds(r0, sub), :] = (w_ref[pl.ds(r0, sub), :] * g_ref[pl.ds(r0, sub), :]).astype(BF16)
            return 0

        lax.fori_loop(0, w_ref.shape[0] // sub, body, 0)

    ss = ss_ref[0]
    for t in range(1, ss_ref.shape[0]):
        ss = ss + ss_ref[t]
    y = jnp.dot(x_ref[...], wb_ref[...], preferred_element_type=F32) * lax.rsqrt(ss[:, 0:1] * inv_k + EPS)
    if act == "relu2":
        y = jnp.square(jnp.maximum(y, 0.0))
    o_ref[...] = y.astype(o_ref.dtype)


def scaled_matmul(hb, ss, gain, w, *, layer=0, n=None, act=None, out_dtype=BF16, name):
    if w.ndim == 2:
        w = w[None]
    m, k = hb.shape
    n = w.shape[2] if n is None else n
    tm, tn = _row_tile(m), _col_tile(n)
    return pl.pallas_call(
        functools.partial(_scaled_matmul_kernel, act=act, inv_k=1.0 / k, sub=256),
        out_shape=jax.ShapeDtypeStruct((m, n), out_dtype),
        grid=(n // tn, m // tm),
        in_specs=[
            pl.BlockSpec((tm, k), lambda j, i: (i, 0)),
            pl.BlockSpec((ss.shape[0], tm, LANES), lambda j, i: (0, i, 0)),
            pl.BlockSpec((k, 1), lambda j, i: (0, 0)),
            pl.BlockSpec((None, k, tn), lambda j, i: (layer, 0, j)),
        ],
        out_specs=pl.BlockSpec((tm, tn), lambda j, i: (i, j)),
        scratch_shapes=[pltpu.VMEM((k, tn), BF16)],
        compiler_params=_params("arbitrary", "arbitrary"),
        name=name,
    )(hb, ss, gain.reshape(k, 1).astype(F32), w)


def _conv_silu(ref, w, r0, rows, bias=None):
    cur = ref[pl.ds(r0, rows), :].astype(F32)
    halo = ref[pl.ds(pl.multiple_of(jnp.maximum(r0 - HALO, 0), HALO), HALO), :].astype(F32)
    x = jnp.concatenate([halo, cur], axis=0)
    y = cur * w[CONV_W - 1:CONV_W, :]
    for j in range(CONV_W - 1):
        y = y + pltpu.roll(x, CONV_W - 1 - j, axis=0)[HALO:, :] * w[j:j + 1, :]
    if bias is not None:
        y = y + bias
    return _silu(y)


def _tri(n, strict=False):
    r = lax.broadcasted_iota(jnp.int32, (n, n), 0)
    c = lax.broadcasted_iota(jnp.int32, (n, n), 1)
    return (r > c) if strict else (r >= c)


def _split_bf16(x):
    hi = x.astype(BF16)
    return hi, (x - hi.astype(F32)).astype(BF16)


def _pair_blockdiag(x, first):
    z = jnp.zeros_like(x)
    return jnp.concatenate([jnp.where(first, x, z), jnp.where(first, z, x)], axis=0)


def _pair_matmul3(a_parts, b_parts, first):
    a_hi, a_lo = a_parts
    bh = _pair_blockdiag(b_parts[0], first)
    bl = _pair_blockdiag(b_parts[1], first)
    lhs = jnp.concatenate([a_hi, a_lo], axis=1)
    rhs = jnp.concatenate([jnp.concatenate([bh, bl], axis=1),
                           jnp.concatenate([bh, jnp.zeros_like(bl)], axis=1)], axis=0)
    r = jnp.dot(lhs, rhs, preferred_element_type=F32)
    w = a_hi.shape[1]
    return r[:, :w] + r[:, w:]


def _pair_unit_lower_inverses(mats, n, first, between):
    row = lax.broadcasted_iota(jnp.int32, (n, 2 * n), 0)
    col = lax.broadcasted_iota(jnp.int32, (n, 2 * n), 1) & (n - 1)
    eye = (row == col).astype(F32)
    ps = [-a for a in mats]
    ts = [eye + p for p in ps]
    splits = [_split_bf16(p) for p in ps]
    ps = [_pair_matmul3(s, s, first) for s in splits]
    between()
    k = 2
    while k < n // 2:
        splits = [_split_bf16(p) for p in ps]
        stacked = []
        for t, s in zip(ts, splits):
            t_hi, t_lo = _split_bf16(t)
            lhs = (jnp.concatenate([t_hi, s[0]], axis=0), jnp.concatenate([t_lo, s[1]], axis=0))
            stacked.append(_pair_matmul3(lhs, s, first))
        ts = [t + r[:n] for t, r in zip(ts, stacked)]
        ps = [r[n:] for r in stacked]
        between()
        k *= 2
    return [t + _pair_matmul3(_split_bf16(t), _split_bf16(p), first) for t, p in zip(ts, ps)]


def _gdn_kernel(q_ref, k_ref, v_ref, z_ref, wq_ref, wk_ref, wv_ref, gcol_ref, grow_ref, pcol_ref, prow_ref,
                nw_ref, o_ref, s_ref, wq_s, u_s, oi_s, kd_s, eg_s, *, n_chunks, first_chunk, unroll):
    c_len = GDN_CHUNK
    d = HEAD_DIM
    nq = GDN_QK_PER_STEP
    nv = 2 * nq
    wq, wk, wv = wq_ref[...], wk_ref[...], wv_ref[...]
    neg_a_col = -jnp.exp(pcol_ref[:, 0:nv])
    dtb_col = pcol_ref[:, nv:2 * nv]
    neg_a_pair = -jnp.exp(prow_ref[0:nq, :])
    dtb_pair = prow_ref[nq:2 * nq, :]
    pair_row = lax.broadcasted_iota(jnp.int32, (c_len, 2 * c_len), 0)
    pair_lane = lax.broadcasted_iota(jnp.int32, (c_len, 2 * c_len), 1)
    pair_col = pair_lane & (c_len - 1)
    first = pair_lane < c_len
    lower_pair = pair_row >= pair_col
    strict_pair = pair_row > pair_col
    lower_f = _tri(c_len).astype(F32)
    r2 = lax.broadcasted_iota(jnp.int32, (2 * c_len, 2 * c_len), 0)
    c2 = lax.broadcasted_iota(jnp.int32, (2 * c_len, 2 * c_len), 1)
    upper_pair = (((r2 < c_len) == (c2 < c_len)) & ((r2 & (c_len - 1)) <= (c2 & (c_len - 1)))).astype(F32)
    norm_w = nw_ref[...]

    def load_chunk(c):
        r0 = pl.multiple_of(c * c_len, c_len)
        valid_col = (r0 + lax.broadcasted_iota(jnp.int32, (c_len, 1), 0)) >= PAD
        valid_pair = (r0 + pair_col[0:1, :]) >= PAD
        gc = gcol_ref[c]
        g_col = jnp.where(valid_col, neg_a_col * _softplus(gc[:, nv:2 * nv] + dtb_col), 0.0)
        g_pair = jnp.where(valid_pair, neg_a_pair * _softplus(grow_ref[c] + dtb_pair), 0.0)
        return dict(
            c=c,
            q=jnp.where(valid_col, _conv_silu(q_ref, wq, r0, c_len), 0.0),
            k=jnp.where(valid_col, _conv_silu(k_ref, wk, r0, c_len), 0.0),
            v=jnp.where(valid_col, _conv_silu(v_ref, wv, r0, c_len), 0.0),
            beta=jnp.where(valid_col, _sigmoid(gc[:, 0:nv]), 0.0),
            cum_col=_hdot(lower_f, g_col),
            cum_pair=_hdot(g_pair, upper_pair),
        )

    def prepare_group(c_first, between):
        chunks = [load_chunk(c_first + j) for j in range(unroll)]
        probs = [dict(ch=ch, a=a) for ch in chunks for a in range(nq)]
        for pr in probs:
            ch, a = pr["ch"], pr["a"]
            q = ch["q"][:, a * d:(a + 1) * d]
            k = ch["k"][:, a * d:(a + 1) * d]
            pr["qn"] = q * lax.rsqrt(jnp.sum(q * q, axis=-1, keepdims=True) + EPS) * (d ** -0.5)
            pr["kn"] = k * lax.rsqrt(jnp.sum(k * k, axis=-1, keepdims=True) + EPS)
        for pr in probs:
            kn_b = pr["kn"].astype(BF16)
            both = lax.dot_general(jnp.concatenate([kn_b, pr["qn"].astype(BF16)], axis=0),
                                   jnp.concatenate([kn_b, kn_b], axis=0), NT, preferred_element_type=F32)
            pr["kk"], pr["qk"] = both[:c_len], both[c_len:]
        between()
        for pr in probs:
            ch, a = pr["ch"], pr["a"]
            h0, h1 = 2 * a, 2 * a + 1
            gcol = jnp.where(first, ch["cum_col"][:, h0:h0 + 1], ch["cum_col"][:, h1:h1 + 1])
            bcol = jnp.where(first, ch["beta"][:, h0:h0 + 1], ch["beta"][:, h1:h1 + 1])
            pr["decay"] = jnp.exp(jnp.where(lower_pair, gcol - ch["cum_pair"][a:a + 1, :], -jnp.inf))
            pr["amat"] = jnp.where(strict_pair, pr["kk"] * bcol * pr["decay"], 0.0)
        ts = _pair_unit_lower_inverses([pr["amat"] for pr in probs], c_len, first, between)
        for pr, t in zip(probs, ts):
            ch, a = pr["ch"], pr["a"]
            rhs = []
            for vh in (2 * a, 2 * a + 1):
                b1 = ch["beta"][:, vh:vh + 1]
                rhs.append(jnp.concatenate([ch["v"][:, vh * d:(vh + 1) * d] * b1,
                                            pr["kn"] * (b1 * jnp.exp(ch["cum_col"][:, vh:vh + 1]))], axis=1))
            zero = jnp.zeros_like(rhs[0])
            pr["uw"] = _bdot(t, jnp.concatenate([jnp.concatenate([rhs[0], zero], axis=1),
                                                 jnp.concatenate([zero, rhs[1]], axis=1)], axis=0))
        for pr in probs:
            uw = pr["uw"]
            zero = jnp.zeros_like(uw[:, :2 * d])
            pr["ow"] = _bdot(pr["qk"] * pr["decay"], jnp.concatenate(
                [jnp.concatenate([uw[:, :2 * d], zero], axis=1),
                 jnp.concatenate([zero, uw[:, 2 * d:]], axis=1)], axis=0))
        for pr in probs:
            ch, a, uw, ow = pr["ch"], pr["a"], pr["uw"], pr["ow"]
            c = ch["c"]
            for i, vh in enumerate((2 * a, 2 * a + 1)):
                gcol1 = ch["cum_col"][:, vh:vh + 1]
                g_last = ch["cum_col"][c_len - 1:c_len, vh:vh + 1]
                u, w = uw[:, 2 * i * d:(2 * i + 1) * d], uw[:, (2 * i + 1) * d:(2 * i + 2) * d]
                oi, qw = ow[:, 2 * i * d:(2 * i + 1) * d], ow[:, (2 * i + 1) * d:(2 * i + 2) * d]
                wq_s[c, vh, pl.ds(0, c_len), :] = w.astype(BF16)
                wq_s[c, vh, pl.ds(c_len, c_len), :] = (pr["qn"] * jnp.exp(gcol1) - qw).astype(BF16)
                u_s[c, vh] = u
                oi_s[c, vh] = oi
                kd_s[c, vh] = (pr["kn"] * jnp.exp(g_last - gcol1)).astype(BF16)
                eg_s[c, vh] = jnp.broadcast_to(jnp.exp(g_last), (8, d))

    def recur_group(c_first):
        heads = range(nv)
        state = dict(s=[s_ref[vh] for vh in heads])

        def products(c):
            state["sq"] = [jnp.dot(wq_s[c, vh], state["s"][vh].astype(BF16), preferred_element_type=F32)
                           for vh in heads]

        def update(c, last):
            r0 = pl.multiple_of(c * c_len, c_len)
            valid_col = (r0 + lax.broadcasted_iota(jnp.int32, (c_len, 1), 0)) >= PAD
            z_all = z_ref[pl.ds(r0, c_len), :].astype(F32)
            sq = state["sq"]
            upd = [_bdot(kd_s[c, vh], u_s[c, vh] - sq[vh][:c_len], TN) for vh in heads]
            state["s"] = [state["s"][vh] * eg_s[c, vh][0:1, :] + upd[vh] for vh in heads]
            outs = []
            for vh in heads:
                o = oi_s[c, vh] + sq[vh][c_len:]
                o = o * lax.rsqrt(jnp.mean(o * o, axis=-1, keepdims=True) + EPS) * norm_w
                outs.append(o * _silu(z_all[:, vh * d:(vh + 1) * d]))
            out = jnp.where(valid_col, jnp.concatenate(outs, axis=1), 0.0)
            o_ref[pl.ds(r0, c_len), :] = out.astype(o_ref.dtype)
            if last:
                for vh in heads:
                    s_ref[vh] = state["s"][vh]

        steps = []
        for j in range(unroll):
            steps.append(functools.partial(products, c_first + j))
            steps.append(functools.partial(update, c_first + j, j == unroll - 1))
        return steps

    def run_between(steps):
        pending = list(steps)

        def between():
            if pending:
                pending.pop(0)()
        return between, pending

    if first_chunk:
        o_ref[pl.ds(0, first_chunk * c_len), :] = jnp.zeros((first_chunk * c_len, nv * d), o_ref.dtype)
    s_ref[...] = jnp.zeros_like(s_ref)
    n_groups = (n_chunks - first_chunk) // unroll

    prepare_group(first_chunk, lambda: None)

    def group_step(g, _):
        between, pending = run_between(recur_group(first_chunk + (g - 1) * unroll))
        prepare_group(first_chunk + g * unroll, between)
        for step in pending:
            step()
        return 0

    lax.fori_loop(1, n_groups, group_step, 0)
    for step in recur_group(first_chunk + (n_groups - 1) * unroll):
        step()


def gdn_core(qkvz, ba, conv_w, a_log, dt_bias, norm_w, batch):
    m = qkvz.shape[0]
    lp = m // batch
    hk = 16
    d = HEAD_DIM
    nq = GDN_QK_PER_STEP
    nv = 2 * nq
    ng = hk // nq
    n_chunks = lp // GDN_CHUNK
    first_chunk, unroll = next((f, u) for u in range(GDN_MAX_UNROLL, 0, -1) for f in (1, 0)
                               if (n_chunks - f) % u == 0)
    qkvz = qkvz.reshape(batch, lp, qkvz.shape[1])
    b_log = ba[:, :2 * hk].reshape(batch, n_chunks, GDN_CHUNK, ng, nv)
    a_log_t = ba[:, 2 * hk:].reshape(batch, n_chunks, GDN_CHUNK, ng, nq, 2)
    gcol = jnp.transpose(jnp.concatenate([b_log, a_log_t.reshape(b_log.shape)], axis=-1), (0, 3, 1, 2, 4))
    grow = jnp.transpose(a_log_t, (0, 3, 1, 4, 5, 2)).reshape(batch, ng, n_chunks, nq, 2 * GDN_CHUNK)
    p = jnp.concatenate([a_log.reshape(ng, nv), dt_bias.reshape(ng, nv)], axis=-1).astype(F32)
    pcol = p.reshape(ng, 1, 2 * nv)
    prow = jnp.concatenate([jnp.repeat(a_log.reshape(ng, nq, 2), GDN_CHUNK, axis=-1),
                            jnp.repeat(dt_bias.reshape(ng, nq, 2), GDN_CHUNK, axis=-1)], axis=1).astype(F32)
    qb, vb = nq * d, nv * d
    kern = functools.partial(_gdn_kernel, n_chunks=n_chunks, first_chunk=first_chunk, unroll=unroll)
    out = pl.pallas_call(
        kern,
        out_shape=jax.ShapeDtypeStruct((batch, lp, 2 * hk * d), BF16),
        grid=(batch, ng),
        in_specs=[
            pl.BlockSpec((None, lp, qb), lambda b, h: (b, 0, h)),
            pl.BlockSpec((None, lp, qb), lambda b, h: (b, 0, ng + h)),
            pl.BlockSpec((None, lp, vb), lambda b, h: (b, 0, ng + h)),
            pl.BlockSpec((None, lp, vb), lambda b, h: (b, 0, 2 * ng + h)),
            pl.BlockSpec((CONV_W, qb), lambda b, h: (0, h)),
            pl.BlockSpec((CONV_W, qb), lambda b, h: (0, ng + h)),
            pl.BlockSpec((CONV_W, vb), lambda b, h: (0, ng + h)),
            pl.BlockSpec((None, None, n_chunks, GDN_CHUNK, 2 * nv), lambda b, h: (b, h, 0, 0, 0)),
            pl.BlockSpec((None, None, n_chunks, nq, 2 * GDN_CHUNK), lambda b, h: (b, h, 0, 0, 0)),
            pl.BlockSpec((None, 1, 2 * nv), lambda b, h: (h, 0, 0)),
            pl.BlockSpec((None, 2 * nq, 2 * GDN_CHUNK), lambda b, h: (h, 0, 0)),
            pl.BlockSpec((1, d), lambda b, h: (0, 0)),
        ],
        out_specs=pl.BlockSpec((None, lp, vb), lambda b, h: (b, 0, h)),
        scratch_shapes=[
            pltpu.VMEM((nv, d, d), F32),
            pltpu.VMEM((n_chunks, nv, 2 * GDN_CHUNK, d), BF16),
            pltpu.VMEM((n_chunks, nv, GDN_CHUNK, d), F32),
            pltpu.VMEM((n_chunks, nv, GDN_CHUNK, d), F32),
            pltpu.VMEM((n_chunks, nv, GDN_CHUNK, d), BF16),
            pltpu.VMEM((n_chunks, nv, 8, d), F32),
        ],
        compiler_params=_params("parallel", "parallel"),
        name="gdn_core",
    )(qkvz, qkvz, qkvz, qkvz, conv_w, conv_w, conv_w, gcol, grow, pcol, prow, norm_w.reshape(1, d).astype(F32))
    return out.reshape(m, 2 * hk * d)


def _ssd_kernel(z_ref, x_ref, b_ref, c_ref, wx_ref, wb_ref, wc_ref, bx_ref, bb_ref, bc_ref, dcol_ref, drow_ref,
                pcol_ref, prow_ref, nw_ref, o_ref, s_ref, *, n_chunks):
    c_len = SSM_CHUNK
    hp = SSM_HEAD_DIM
    nh = SSM_HPG
    width = nh * hp
    s_ref[...] = jnp.zeros_like(s_ref)

    wx, wb, wc = wx_ref[...], wb_ref[...], wc_ref[...]
    bx, bb, bc = bx_ref[...], bb_ref[...], bc_ref[...]
    neg_a_col = -jnp.exp(pcol_ref[0:1, :])
    dtb_col = pcol_ref[1:2, :]
    d_skip = pcol_ref[2:3, :]
    neg_a_row = -jnp.exp(prow_ref[:, 0:1])
    dtb_row = prow_ref[:, 1:2]
    lower = _tri(c_len)
    lower_f = lower.astype(F32)
    upper_f = (lax.broadcasted_iota(jnp.int32, (c_len, c_len), 0)
               <= lax.broadcasted_iota(jnp.int32, (c_len, c_len), 1)).astype(F32)
    lane = lax.broadcasted_iota(jnp.int32, (1, 2 * hp), 1)

    def spread(cols):
        return jnp.concatenate([jnp.where(lane < hp, cols[:, 2 * t:2 * t + 1], cols[:, 2 * t + 1:2 * t + 2])
                                for t in range(nh // 2)], axis=1)

    d_skip_x = spread(d_skip)
    norm_w = nw_ref[...]

    def body(c, _):
        r0 = pl.multiple_of(c * c_len, c_len)
        valid_col = (r0 + lax.broadcasted_iota(jnp.int32, (c_len, 1), 0)) >= PAD
        valid_row = (r0 + lax.broadcasted_iota(jnp.int32, (1, c_len), 1)) >= PAD
        xs = jnp.where(valid_col, _conv_silu(x_ref, wx, r0, c_len, bx), 0.0)
        bm = jnp.where(valid_col, _conv_silu(b_ref, wb, r0, c_len, bb), 0.0)
        cm = jnp.where(valid_col, _conv_silu(c_ref, wc, r0, c_len, bc), 0.0)
        dt_col = jnp.where(valid_col, _softplus(dcol_ref[c] + dtb_col), 0.0)
        dt_row = jnp.where(valid_row, _softplus(drow_ref[c] + dtb_row), 0.0)
        cum_col = _hdot(lower_f, dt_col * neg_a_col)
        cum_row = _hdot(dt_row * neg_a_row, upper_f)
        cum_last = cum_col[c_len - 1:c_len, :]

        xdt = xs * spread(dt_col)
        cb = _bdot(cm, bm, NT)
        s = s_ref[...]
        y = _bdot(cm, s) * spread(jnp.exp(cum_col))
        s_ref[...] = s * spread(jnp.exp(cum_last)) + _bdot(bm, xdt * spread(jnp.exp(cum_last - cum_col)), TN)
        diag = []
        for pair in range(nh // 2):
            sc = []
            for j in (2 * pair, 2 * pair + 1):
                lmat = jnp.exp(jnp.where(lower, cum_col[:, j:j + 1] - cum_row[j:j + 1, :], -jnp.inf))
                sc.append((cb * lmat).astype(BF16))
            xp = xdt[:, pair * 2 * hp:(pair + 1) * 2 * hp]
            rhs = jnp.concatenate([jnp.where(lane < hp, xp, 0.0), jnp.where(lane >= hp, xp, 0.0)], axis=0)
            diag.append(_bdot(jnp.concatenate(sc, axis=1), rhs))
        y = y + jnp.concatenate(diag, axis=1) + xs * d_skip_x
        y = y * _silu(z_ref[pl.ds(r0, c_len), :].astype(F32))
        y = y * lax.rsqrt(jnp.mean(y * y, axis=-1, keepdims=True) + EPS) * norm_w
        o_ref[pl.ds(r0, c_len), :] = jnp.where(valid_col, y, 0.0).astype(o_ref.dtype)
        return 0

    lax.fori_loop(0, n_chunks, body, 0)


def ssd_core(zx, dt_raw, conv_w, conv_b, a_log, dt_bias, d_skip, norm_w, batch):
    m = zx.shape[0]
    lp = m // batch
    ng = 8
    width = SSM_HPG * SSM_HEAD_DIM
    d_inner = ng * width
    n_chunks = lp // SSM_CHUNK
    zx = zx.reshape(batch, lp, zx.shape[1])
    d5 = dt_raw.reshape(batch, n_chunks, SSM_CHUNK, ng, SSM_HPG)
    dcol = jnp.transpose(d5, (0, 3, 1, 2, 4))
    drow = jnp.transpose(d5, (0, 3, 1, 4, 2))
    p3 = jnp.stack([a_log.reshape(ng, SSM_HPG), dt_bias.reshape(ng, SSM_HPG), d_skip.reshape(ng, SSM_HPG)],
                   axis=1).astype(F32)
    prow = jnp.transpose(p3, (0, 2, 1))
    conv_b = conv_b.reshape(1, -1).astype(F32)
    xo, bo, co = d_inner // width, d_inner // SSM_STATE, (d_inner + ng * SSM_STATE) // SSM_STATE
    kern = functools.partial(_ssd_kernel, n_chunks=n_chunks)
    out = pl.pallas_call(
        kern,
        out_shape=jax.ShapeDtypeStruct((batch, lp, d_inner), BF16),
        grid=(batch, ng),
        in_specs=[
            pl.BlockSpec((None, lp, width), lambda b, g: (b, 0, g)),
            pl.BlockSpec((None, lp, width), lambda b, g: (b, 0, xo + g)),
            pl.BlockSpec((None, lp, SSM_STATE), lambda b, g: (b, 0, 2 * bo + g)),
            pl.BlockSpec((None, lp, SSM_STATE), lambda b, g: (b, 0, bo + co + g)),
            pl.BlockSpec((CONV_W, width), lambda b, g: (0, g)),
            pl.BlockSpec((CONV_W, SSM_STATE), lambda b, g: (0, bo + g)),
            pl.BlockSpec((CONV_W, SSM_STATE), lambda b, g: (0, co + g)),
            pl.BlockSpec((1, width), lambda b, g: (0, g)),
            pl.BlockSpec((1, SSM_STATE), lambda b, g: (0, bo + g)),
            pl.BlockSpec((1, SSM_STATE), lambda b, g: (0, co + g)),
            pl.BlockSpec((None, None, n_chunks, SSM_CHUNK, SSM_HPG), lambda b, g: (b, g, 0, 0, 0)),
            pl.BlockSpec((None, None, n_chunks, SSM_HPG, SSM_CHUNK), lambda b, g: (b, g, 0, 0, 0)),
            pl.BlockSpec((None, 3, SSM_HPG), lambda b, g: (g, 0, 0)),
            pl.BlockSpec((None, SSM_HPG, 3), lambda b, g: (g, 0, 0)),
            pl.BlockSpec((1, width), lambda b, g: (0, g)),
        ],
        out_specs=pl.BlockSpec((None, lp, width), lambda b, g: (b, 0, g)),
        scratch_shapes=[pltpu.VMEM((SSM_STATE, width), F32)],
        compiler_params=_params("parallel", "parallel"),
        name="ssd_core",
    )(zx, zx, zx, zx, conv_w, conv_w, conv_w, conv_b, conv_b, conv_b, dcol, drow, p3, prow,
      norm_w.reshape(1, d_inner).astype(F32))
    return out.reshape(m, d_inner)


def _rope_tables(lp):
    inv = ROPE_THETA ** (-jnp.arange(0, MLA_ROPE, 2, dtype=F32) / MLA_ROPE)
    pos = jnp.maximum(jnp.arange(lp, dtype=F32) - PAD, 0.0)
    ang = pos[:, None] * inv[None, :]
    return jnp.tile(jnp.cos(ang), (1, 4)), jnp.tile(jnp.sin(ang), (1, 4))


def _pair_head(lane):
    return lax.shift_right_logical(lane, 5) & 1


def _rope_pair(t, cos, sin):
    lane = lax.broadcasted_iota(jnp.int32, (1, t.shape[1]), 1)
    partner = pltpu.roll(t, t.shape[1] // 2, axis=1)
    return t * cos + jnp.where(lane < t.shape[1] // 2, -partner, partner) * sin


def _qprep_kernel(q_ref, gn_ref, gp_ref, cos_ref, sin_ref, o_ref, *, n_heads, scale):
    d = HEAD_DIM
    cos, sin = cos_ref[...], sin_ref[...]
    for h in range(n_heads):
        x = q_ref[:, h * d:(h + 1) * d]
        y = x * lax.rsqrt(jnp.mean(x * x, axis=-1, keepdims=True) + EPS) * gn_ref[:, h * d:(h + 1) * d]
        o_ref[:, h * d:(h + 1) * d] = (y * scale).astype(o_ref.dtype)
    r = lax.broadcasted_iota(jnp.int32, (d, d), 0)
    c = lax.broadcasted_iota(jnp.int32, (d, d), 1)
    same_head = (_pair_head(r) == _pair_head(c)).astype(F32)
    base = n_heads * d
    for p in range(n_heads // 2):
        x = q_ref[:, base + p * d:base + (p + 1) * d]
        ms = _hdot(x * x, same_head) * (1.0 / MLA_ROPE)
        y = x * lax.rsqrt(ms + EPS) * gp_ref[:, p * d:(p + 1) * d]
        o_ref[:, base + p * d:base + (p + 1) * d] = (_rope_pair(y, cos, sin) * scale).astype(o_ref.dtype)


def _kprep_kernel(kv_ref, kpe_ref, gn_ref, gp_ref, cos_ref, sin_ref, k_ref, v_ref, pe_ref, *, n_heads):
    d = HEAD_DIM
    for h in range(n_heads):
        x = kv_ref[:, h * d:(h + 1) * d]
        y = x * lax.rsqrt(jnp.mean(x * x, axis=-1, keepdims=True) + EPS) * gn_ref[:, h * d:(h + 1) * d]
        k_ref[:, h * d:(h + 1) * d] = y.astype(k_ref.dtype)
    v_ref[...] = kv_ref[:, n_heads * d:].astype(v_ref.dtype)
    x = kpe_ref[...]
    y = x * lax.rsqrt(jnp.mean(x * x, axis=-1, keepdims=True) + EPS) * gp_ref[...]
    pe_ref[...] = _rope_pair(y, cos_ref[...], sin_ref[...]).astype(pe_ref.dtype)


def _flash_kernel(qn_ref, qpe_ref, kn_ref, kpe_ref, v_ref, o_ref, *, kv_blk):
    d = HEAD_DIM
    blk = qn_ref.shape[0]
    lp = kn_ref.shape[0]
    i = pl.program_id(2)
    lane = lax.broadcasted_iota(jnp.int32, (1, d), 1)
    qpe = qpe_ref[...]
    qf = []
    for hh in range(2):
        mine = _pair_head(lane) == hh
        qf.append(jnp.concatenate([qn_ref[:, hh * d:(hh + 1) * d], jnp.where(mine, qpe, jnp.zeros_like(qpe))],
                                  axis=1))
    qpos = i * blk + lax.broadcasted_iota(jnp.int32, (blk, 1), 0)
    q_limit = jnp.maximum(qpos, PAD)

    def body(j, carry, masked):
        c0 = pl.multiple_of(jnp.minimum(PAD + j * kv_blk, lp - kv_blk), HALO)
        kpe = kpe_ref[pl.ds(c0, kv_blk), :]
        scores = []
        for hh in range(2):
            kf = jnp.concatenate([kn_ref[pl.ds(c0, kv_blk), hh * d:(hh + 1) * d], kpe], axis=1)
            scores.append(lax.dot_general(qf[hh], kf, NT, preferred_element_type=F32))
        if masked:
            kpos = c0 + lax.broadcasted_iota(jnp.int32, (1, kv_blk), 1)
            ok = (kpos >= PAD + j * kv_blk) & (kpos <= q_limit)
            scores = [jnp.where(ok, s, -1e30) for s in scores]
        stats = []
        for hh in range(2):
            m, l, _ = carry[hh]
            s = scores[hh]
            m_new = jnp.maximum(m, jnp.max(s, axis=-1, keepdims=True))
            p = jnp.exp(s - m_new)
            alpha = jnp.exp(m - m_new)
            stats.append((m_new, alpha * l + jnp.sum(p, axis=-1, keepdims=True), alpha, p.astype(BF16)))
        new = []
        for hh in range(2):
            m_new, l, alpha, p = stats[hh]
            pv = jnp.dot(p, v_ref[pl.ds(c0, kv_blk), hh * d:(hh + 1) * d], preferred_element_type=F32)
            new.append((m_new, l, alpha * carry[hh][2] + pv))
        return tuple(new)

    init = tuple((jnp.full((blk, 1), -1e30, F32), jnp.zeros((blk, 1), F32), jnp.zeros((blk, d), F32))
                 for _ in range(2))
    n_free = lax.div(jnp.maximum(i * blk + 1 - PAD, 0), kv_blk)
    n_kv = lax.div(i * blk + blk - PAD + kv_blk - 1, kv_blk)
    res = lax.fori_loop(0, n_free, functools.partial(body, masked=False), init)
    res = lax.fori_loop(n_free, n_kv, functools.partial(body, masked=True), res)
    out = jnp.concatenate([acc / l for (_, l, acc) in res], axis=1)
    o_ref[...] = jnp.where(qpos >= PAD, out, 0.0).astype(o_ref.dtype)


def mla_attention(q_raw, kv_raw, kpe_src, kpe_col_block, q_norm, k_norm, batch):
    m = q_raw.shape[0]
    lp = m // batch
    nh = 16
    d = HEAD_DIM
    half = MLA_ROPE // 2
    scale = (d + MLA_ROPE) ** -0.5
    cos, sin = _rope_tables(lp)
    qn_gain = jnp.tile(q_norm[:d], nh).reshape(1, nh * d).astype(F32)
    kn_gain = jnp.tile(k_norm[:d], nh).reshape(1, nh * d).astype(F32)
    qf, qs = q_norm[d:d + half], q_norm[d + half:]
    kf, ks = k_norm[d:d + half], k_norm[d + half:]
    qp_gain = jnp.tile(jnp.concatenate([qf, qf, qs, qs]), nh // 2).reshape(1, nh // 2 * d).astype(F32)
    kp_gain = jnp.concatenate([kf, kf, ks, ks]).reshape(1, d).astype(F32)
    tp = ATT_PREP_ROWS
    nt = lp // tp
    qw = q_raw.shape[1]

    q_prep = pl.pallas_call(
        functools.partial(_qprep_kernel, n_heads=nh, scale=scale),
        out_shape=jax.ShapeDtypeStruct((batch, lp, qw), BF16),
        grid=(batch, nt),
        in_specs=[
            pl.BlockSpec((None, tp, qw), lambda b, t: (b, t, 0)),
            pl.BlockSpec((1, nh * d), lambda b, t: (0, 0)),
            pl.BlockSpec((1, nh // 2 * d), lambda b, t: (0, 0)),
            pl.BlockSpec((tp, d), lambda b, t: (t, 0)),
            pl.BlockSpec((tp, d), lambda b, t: (t, 0)),
        ],
        out_specs=pl.BlockSpec((None, tp, qw), lambda b, t: (b, t, 0)),
        compiler_params=_params("parallel", "parallel"),
        name="mla_q_prep",
    )(q_raw.reshape(batch, lp, qw), qn_gain, qp_gain, cos, sin)

    kvw = kv_raw.shape[1]
    kpe3 = kpe_src.reshape(batch, lp, kpe_src.shape[1])
    k_prep, v_prep, pe_prep = pl.pallas_call(
        functools.partial(_kprep_kernel, n_heads=nh),
        out_shape=(jax.ShapeDtypeStruct((batch, lp, nh * d), BF16),
                   jax.ShapeDtypeStruct((batch, lp, nh * d), BF16),
                   jax.ShapeDtypeStruct((batch, lp, d), BF16)),
        grid=(batch, nt),
        in_specs=[
            pl.BlockSpec((None, tp, kvw), lambda b, t: (b, t, 0)),
            pl.BlockSpec((None, tp, d), lambda b, t: (b, t, kpe_col_block)),
            pl.BlockSpec((1, nh * d), lambda b, t: (0, 0)),
            pl.BlockSpec((1, d), lambda b, t: (0, 0)),
            pl.BlockSpec((tp, d), lambda b, t: (t, 0)),
            pl.BlockSpec((tp, d), lambda b, t: (t, 0)),
        ],
        out_specs=(pl.BlockSpec((None, tp, nh * d), lambda b, t: (b, t, 0)),
                   pl.BlockSpec((None, tp, nh * d), lambda b, t: (b, t, 0)),
                   pl.BlockSpec((None, tp, d), lambda b, t: (b, t, 0))),
        compiler_params=_params("parallel", "parallel"),
        name="mla_kv_prep",
    )(kv_raw.reshape(batch, lp, kvw), kpe3, kn_gain, kp_gain, cos, sin)

    np_ = nh // 2
    tq = next(t for t in range(min(lp, ATT_MAX_Q_ROWS) // 16 * 16, 0, -16) if lp % t == 0)
    out = pl.pallas_call(
        functools.partial(_flash_kernel, kv_blk=min(ATT_KV_BLK, lp)),
        out_shape=jax.ShapeDtypeStruct((batch, lp, nh * d), BF16),
        grid=(batch, np_, lp // tq),
        in_specs=[
            pl.BlockSpec((None, tq, 2 * d), lambda b, p, i: (b, i, p)),
            pl.BlockSpec((None, tq, d), lambda b, p, i: (b, i, nh + p)),
            pl.BlockSpec((None, lp, 2 * d), lambda b, p, i: (b, 0, p)),
            pl.BlockSpec((None, lp, d), lambda b, p, i: (b, 0, 0)),
            pl.BlockSpec((None, lp, 2 * d), lambda b, p, i: (b, 0, p)),
        ],
        out_specs=pl.BlockSpec((None, tq, 2 * d), lambda b, p, i: (b, i, p)),
        compiler_params=_params("parallel", "parallel", "arbitrary"),
        name="mla_flash",
    )(q_prep, q_prep, k_prep, pe_prep, v_prep)
    return out.reshape(m, nh * d)


def _gdn_layer(stream, gain, w_in_all, layer, conv_w, a_log, dt_bias, norm_w, w_out_all, batch):
    h, hb, ss = stream
    n_gate = 2 * a_log.shape[0]
    n_big = w_in_all.shape[2] - n_gate
    qkvz = scaled_matmul(hb, ss, gain, w_in_all, layer=layer, n=n_big, name="gdn_in_proj")
    ba = scaled_matmul(hb, ss, gain, w_in_all[layer][:, n_big:], out_dtype=F32, name="gdn_gate_proj")
    o = gdn_core(qkvz, ba, conv_w.astype(F32), a_log, dt_bias, norm_w, batch)
    return matmul_residual(o, w_out_all, h, layer=layer, name="gdn_out_proj")


def _mla_layer(stream, gain, w_in, norm_q_lat, norm_kv_lat, w_uq, w_ukv, q_norm, k_norm, w_out_all, layer, batch):
    h, hb, ss = stream
    nh, d, r = 16, HEAD_DIM, MLA_ROPE
    q_lora, kv_lora = norm_q_lat.shape[0], norm_kv_lat.shape[0]
    half = r // 2
    w_cq, w_ckv, w_pe = w_in[:, :q_lora], w_in[:, q_lora:q_lora + kv_lora], w_in[:, q_lora + kv_lora:]
    w_pe_t = jnp.concatenate([w_pe[:, :half], w_pe[:, :half], w_pe[:, half:], w_pe[:, half:]], axis=1)
    gap = (-(kv_lora + d)) % q_lora
    w_lat = jnp.concatenate([w_ckv, w_pe_t, jnp.zeros((w_in.shape[0], gap), w_in.dtype), w_cq], axis=1)
    lat = scaled_matmul(hb, ss, gain, w_lat, out_dtype=F32, name="mla_in_proj")
    wq = w_uq.reshape(q_lora, nh, d + r)
    wq_pe = wq[:, :, d:].reshape(q_lora, nh // 2, 2, 2, half)
    wq_pe = jnp.transpose(wq_pe, (0, 1, 3, 2, 4)).reshape(q_lora, nh // 2 * d)
    wq_all = jnp.concatenate([wq[:, :, :d].reshape(q_lora, nh * d), wq_pe], axis=1)
    q_raw = norm_matmul(lat, norm_q_lat, wq_all.astype(BF16), x_col_block=(kv_lora + d + gap) // q_lora,
                        out_dtype=F32, name="mla_q_up")
    wkv = w_ukv.reshape(kv_lora, nh, 2 * d)
    wkv_all = jnp.concatenate([wkv[:, :, :d].reshape(kv_lora, nh * d), wkv[:, :, d:].reshape(kv_lora, nh * d)],
                              axis=1)
    kv_raw = norm_matmul(lat, norm_kv_lat, wkv_all.astype(BF16), x_col_block=0, out_dtype=F32, name="mla_kv_up")
    o = mla_attention(q_raw, kv_raw, lat, kv_lora // d, q_norm, k_norm, batch)
    return matmul_residual(o, w_out_all, h, layer=layer, name="mla_out_proj")


def _ssm_layer(stream, gain, w_in_all, layer, conv_w, conv_b, a_log, dt_bias, d_skip, norm_w, w_out_all, batch):
    h, hb, ss = stream
    n_dt = a_log.shape[0]
    n_big = w_in_all.shape[2] - n_dt
    zx = scaled_matmul(hb, ss, gain, w_in_all, layer=layer, n=n_big, name="ssm_in_proj")
    dt_raw = scaled_matmul(hb, ss, gain, w_in_all[layer][:, n_big:], out_dtype=F32, name="ssm_dt_proj")
    y = ssd_core(zx, dt_raw, conv_w.astype(F32), conv_b, a_log, dt_bias, d_skip, norm_w, batch)
    return matmul_residual(y, w_out_all, h, layer=layer, name="ssm_out_proj")


def _mlp_layer(stream, gain, w_up_all, w_down_all, layer):
    h, hb, ss = stream
    act = scaled_matmul(hb, ss, gain, w_up_all, layer=layer, act="relu2", name="mlp_up")
    return matmul_residual(act, w_down_all, h, layer=layer, name="mlp_down")


def kernel(x, meta_tokens, norm_mix, norm_mlp, mlp_w_up, mlp_w_down, gdn_w_in, gdn_conv_w, gdn_a_log, gdn_dt_bias, gdn_norm, gdn_w_out, mla_w_in, mla_norm_q_lat, mla_norm_kv_lat, mla_w_uq, mla_w_ukv, mla_q_norm, mla_k_norm, mla_w_out, ssm_w_in, ssm_conv_w, ssm_conv_b, ssm_a_log, ssm_dt_bias, ssm_d, ssm_norm, ssm_w_out):
    batch, seq, dm = x.shape
    depth = norm_mix.shape[0]
    lp = ROW_TILE + seq
    meta = jnp.broadcast_to(meta_tokens.astype(x.dtype)[None], (batch, N_META, dm))
    h = jnp.concatenate([jnp.zeros((batch, PAD, dm), x.dtype), meta, x], axis=1).reshape(batch * lp, dm)
    stream = (h,) + tuple(stream_entry(h))
    down_b, gdn_out_b, ssm_out_b, mla_out_b = (w.astype(BF16) for w in (mlp_w_down, gdn_w_out, ssm_w_out, mla_w_out))
    ia = ib = ic = 0
    for i in range(depth):
        kind = i % 3
        if kind == 0:
            stream = _gdn_layer(stream, norm_mix[i], gdn_w_in, ia, gdn_conv_w[ia], gdn_a_log[ia], gdn_dt_bias[ia],
                                gdn_norm[ia], gdn_out_b, batch)
            ia += 1
        elif kind == 1:
            stream = _mla_layer(stream, norm_mix[i], mla_w_in[ib], mla_norm_q_lat[ib], mla_norm_kv_lat[ib],
                                mla_w_uq[ib], mla_w_ukv[ib], mla_q_norm[ib], mla_k_norm[ib], mla_out_b, ib, batch)
            ib += 1
        else:
            stream = _ssm_layer(stream, norm_mix[i], ssm_w_in, ic, ssm_conv_w[ic], ssm_conv_b[ic], ssm_a_log[ic],
                                ssm_dt_bias[ic], ssm_d[ic], ssm_norm[ic], ssm_out_b, batch)
            ic += 1
        stream = _mlp_layer(stream, norm_mlp[i], mlp_w_up, down_b, i)
    return stream[0].reshape(batch, lp, dm)[:, ROW_TILE:]
```

```python
import functools
import math

import jax
import jax.numpy as jnp
from jax import lax
from jax.experimental import pallas as pl
from jax.experimental.pallas import tpu as pltpu

F32 = jnp.float32
BF16 = jnp.bfloat16
HI = lax.Precision.HIGHEST

EPS = 1e-6
N_META = 16
ROW_TILE = 128
PAD = ROW_TILE - N_META
LANES = 128
HALO = 16
CONV_W = 4

HEAD_DIM = 128
GDN_CHUNK = 64
SSM_CHUNK = 128
SSM_HEAD_DIM = 64
SSM_HPG = 8
SSM_STATE = 128
MLA_ROPE = 64
ROPE_THETA = 10000.0
ATT_PREP_ROWS = 128
ATT_MAX_Q_ROWS = 320
ATT_KV_BLK = 512
GDN_QK_PER_STEP = 2
GDN_MAX_UNROLL = 3

VMEM_LIMIT = 56 * 1024 * 1024
MAX_ROW_TILE = 1088
MAX_K_TILE = 2048
NT = (((1,), (1,)), ((), ()))
TN = (((0,), (0,)), ((), ()))


def _row_tile(m, cap=MAX_ROW_TILE):
    for t in range(min(m, cap) // 64 * 64, 0, -64):
        if m % t == 0:
            return t
    raise ValueError(f"row count {m} has no tile that is a multiple of 64")


def _col_tile(n, cap=1024):
    if n <= cap:
        return n
    for t in range(cap, 0, -128):
        if n % t == 0:
            return t
    raise ValueError(f"column count {n} has no tile that is a multiple of 128")


def _params(*sem):
    return pltpu.CompilerParams(dimension_semantics=sem, vmem_limit_bytes=VMEM_LIMIT)


def _bdot(a, b, dims=None):
    a = a.astype(BF16)
    b = b.astype(BF16)
    if dims is None:
        return jnp.dot(a, b, preferred_element_type=F32)
    return lax.dot_general(a, b, dims, preferred_element_type=F32)


def _hdot(a, b):
    return jnp.dot(a, b, precision=HI, preferred_element_type=F32)


def _sigmoid(x):
    return 0.5 + 0.5 * jnp.tanh(0.5 * x)


def _silu(x):
    return x * _sigmoid(x)


def _softplus(x):
    return jnp.maximum(x, 0.0) + jnp.log(1.0 + jnp.exp(-jnp.abs(x)))


def _norm_matmul_kernel(x_ref, g_ref, w_ref, o_ref, xn_ref, *, sub):
    @pl.when(pl.program_id(1) == 0)
    def _():
        def body(r, _):
            r0 = pl.multiple_of(r * sub, sub)
            x = x_ref[pl.ds(r0, sub), :].astype(F32)
            ms = jnp.mean(x * x, axis=-1, keepdims=True)
            xn_ref[pl.ds(r0, sub), :] = (x * lax.rsqrt(ms + EPS) * g_ref[...]).astype(BF16)
            return 0

        lax.fori_loop(0, x_ref.shape[0] // sub, body, 0)

    o_ref[...] = jnp.dot(xn_ref[...], w_ref[...], preferred_element_type=F32).astype(o_ref.dtype)


def norm_matmul(x, gain, w, *, x_col_block=0, out_dtype=BF16, name):
    m = x.shape[0]
    k, n = w.shape
    tm, tn = _row_tile(m), _col_tile(n)
    return pl.pallas_call(
        functools.partial(_norm_matmul_kernel, sub=64),
        out_shape=jax.ShapeDtypeStruct((m, n), out_dtype),
        grid=(m // tm, n // tn),
        in_specs=[
            pl.BlockSpec((tm, k), lambda i, j: (i, x_col_block)),
            pl.BlockSpec((1, k), lambda i, j: (0, 0)),
            pl.BlockSpec((k, tn), lambda i, j: (0, j)),
        ],
        out_specs=pl.BlockSpec((tm, tn), lambda i, j: (i, j)),
        scratch_shapes=[pltpu.VMEM((tm, k), BF16)],
        compiler_params=_params("parallel", "arbitrary"),
        name=name,
    )(x, gain.reshape(1, k).astype(F32), w)


def _stats_rows(tm, cap=272):
    return next(t for t in range(min(tm, cap) // 16 * 16, 0, -16) if tm % t == 0)


def _stream_stats(h_ref, hb_ref, ss_ref, sub):
    def body(r, _):
        r0 = pl.multiple_of(r * sub, sub)
        x = h_ref[pl.ds(r0, sub), :]
        hb_ref[pl.ds(r0, sub), :] = x.astype(BF16)
        ss_ref[pl.ds(r0, sub), :] = jnp.broadcast_to(jnp.sum(x * x, axis=-1, keepdims=True), (sub, ss_ref.shape[1]))
        return 0

    lax.fori_loop(0, h_ref.shape[0] // sub, body, 0)


def _matmul_residual_kernel(a_ref, w_ref, h_ref, o_ref, hb_ref, ss_ref, *, sub):
    @pl.when(pl.program_id(2) == 0)
    def _():
        o_ref[...] = h_ref[...]

    o_ref[...] += jnp.dot(a_ref[...], w_ref[...], preferred_element_type=F32)

    @pl.when(pl.program_id(2) == pl.num_programs(2) - 1)
    def _():
        _stream_stats(o_ref, hb_ref, ss_ref, sub)


def _stream_out(m, n, tm, tn, index):
    shapes = (jax.ShapeDtypeStruct((m, n), F32), jax.ShapeDtypeStruct((m, n), BF16),
              jax.ShapeDtypeStruct((n // tn, m, LANES), F32))
    specs = (pl.BlockSpec((tm, tn), lambda *g: index(*g)),
             pl.BlockSpec((tm, tn), lambda *g: index(*g)),
             pl.BlockSpec((None, tm, LANES), lambda *g: (index(*g)[1], index(*g)[0], 0)))
    return shapes, specs


def matmul_residual(a, w, h, *, layer=0, name):
    if w.ndim == 2:
        w = w[None]
    m, k = a.shape
    n = w.shape[2]
    tm, tn, tk = _row_tile(m), _col_tile(n), _col_tile(k, MAX_K_TILE)
    shapes, specs = _stream_out(m, n, tm, tn, lambda i, j, kk: (i, j))
    return pl.pallas_call(
        functools.partial(_matmul_residual_kernel, sub=_stats_rows(tm)),
        out_shape=shapes,
        grid=(m // tm, n // tn, k // tk),
        in_specs=[
            pl.BlockSpec((tm, tk), lambda i, j, kk: (i, kk)),
            pl.BlockSpec((None, tk, tn), lambda i, j, kk: (layer, kk, j)),
            pl.BlockSpec((tm, tn), lambda i, j, kk: (i, j)),
        ],
        out_specs=specs,
        compiler_params=_params("parallel", "parallel", "arbitrary"),
        name=name,
    )(a, w, h)


def _stream_entry_kernel(h_ref, hb_ref, ss_ref, *, sub):
    _stream_stats(h_ref, hb_ref, ss_ref, sub)


def stream_entry(h):
    m, n = h.shape
    tm, tn = _row_tile(m), _col_tile(n)
    shapes, specs = _stream_out(m, n, tm, tn, lambda i, j: (i, j))
    return pl.pallas_call(
        functools.partial(_stream_entry_kernel, sub=_stats_rows(tm)),
        out_shape=shapes[1:],
        grid=(m // tm, n // tn),
        in_specs=[pl.BlockSpec((tm, tn), lambda i, j: (i, j))],
        out_specs=specs[1:],
        compiler_params=_params("parallel", "parallel"),
        name="stream_entry",
    )(h)


def _scaled_matmul_kernel(x_ref, ss_ref, *rest, act, inv_k, sub, prescaled):
    if prescaled:
        wb_ref, o_ref = rest
    else:
        g_ref, w_ref, o_ref, wb_ref = rest

        @pl.when(pl.program_id(1) == 0)
        def _():
            def body(r, _):
                r0 = pl.multiple_of(r * sub, sub)
                wb_ref[pl.ds(r0, sub), :] = (w_ref[pl.ds(r0, sub), :] * g_ref[pl.ds(r0, sub), :]).astype(BF16)
                return 0

            lax.fori_loop(0, w_ref.shape[0] // sub, body, 0)

    ss = ss_ref[0]
    for t in range(1, ss_ref.shape[0]):
        ss = ss + ss_ref[t]
    y = jnp.dot(x_ref[...], wb_ref[...], preferred_element_type=F32) * lax.rsqrt(ss[:, 0:1] * inv_k + EPS)
    if act == "relu2":
        y = jnp.square(jnp.maximum(y, 0.0))
    o_ref[...] = y.astype(o_ref.dtype)


def scaled_matmul(hb, ss, gain, w, *, layer=0, n=None, act=None, out_dtype=BF16, name):
    if w.ndim == 2:
        w = w[None]
    m, k = hb.shape
    n = w.shape[2] if n is None else n
    tm, tn = _row_tile(m), _col_tile(n)
    prescaled = gain is None
    assert prescaled == (w.dtype == BF16)
    in_specs = [
        pl.BlockSpec((tm, k), lambda j, i: (i, 0)),
        pl.BlockSpec((ss.shape[0], tm, LANES), lambda j, i: (0, i, 0)),
    ]
    args = [hb, ss]
    if not prescaled:
        in_specs.append(pl.BlockSpec((k, 1), lambda j, i: (0, 0)))
        args.append(gain.reshape(k, 1).astype(F32))
    in_specs.append(pl.BlockSpec((None, k, tn), lambda j, i: (layer, 0, j)))
    args.append(w)
    return pl.pallas_call(
        functools.partial(_scaled_matmul_kernel, act=act, inv_k=1.0 / k, sub=256, prescaled=prescaled),
        out_shape=jax.ShapeDtypeStruct((m, n), out_dtype),
        grid=(n // tn, m // tm),
        in_specs=in_specs,
        out_specs=pl.BlockSpec((tm, tn), lambda j, i: (i, j)),
        scratch_shapes=[] if prescaled else [pltpu.VMEM((k, tn), BF16)],
        compiler_params=_params("arbitrary", "arbitrary"),
        name=name,
    )(*args)


def _conv_silu(ref, w, r0, rows, bias=None):
    cur = ref[pl.ds(r0, rows), :].astype(F32)
    halo = ref[pl.ds(pl.multiple_of(jnp.maximum(r0 - HALO, 0), HALO), HALO), :].astype(F32)
    x = jnp.concatenate([halo, cur], axis=0)
    y = cur * w[CONV_W - 1:CONV_W, :]
    for j in range(CONV_W - 1):
        y = y + pltpu.roll(x, CONV_W - 1 - j, axis=0)[HALO:, :] * w[j:j + 1, :]
    if bias is not None:
        y = y + bias
    return _silu(y)


def _tri(n, strict=False):
    r = lax.broadcasted_iota(jnp.int32, (n, n), 0)
    c = lax.broadcasted_iota(jnp.int32, (n, n), 1)
    return (r > c) if strict else (r >= c)


def _split_bf16(x):
    hi = x.astype(BF16)
    return hi, (x - hi.astype(F32)).astype(BF16)


def _pair_blockdiag(x, first):
    z = jnp.zeros_like(x)
    return jnp.concatenate([jnp.where(first, x, z), jnp.where(first, z, x)], axis=0)


def _pair_matmul3(a_parts, b_parts, first):
    a_hi, a_lo = a_parts
    bh = _pair_blockdiag(b_parts[0], first)
    bl = _pair_blockdiag(b_parts[1], first)
    lhs = jnp.concatenate([a_hi, a_lo], axis=1)
    rhs = jnp.concatenate([jnp.concatenate([bh, bl], axis=1),
                           jnp.concatenate([bh, jnp.zeros_like(bl)], axis=1)], axis=0)
    r = jnp.dot(lhs, rhs, preferred_element_type=F32)
    w = a_hi.shape[1]
    return r[:, :w] + r[:, w:]


def _pair_unit_lower_inverses(mats, n, first, between):
    row = lax.broadcasted_iota(jnp.int32, (n, 2 * n), 0)
    col = lax.broadcasted_iota(jnp.int32, (n, 2 * n), 1) & (n - 1)
    eye = (row == col).astype(F32)
    ps = [-a for a in mats]
    ts = [eye + p for p in ps]
    splits = [_split_bf16(p) for p in ps]
    ps = [_pair_matmul3(s, s, first) for s in splits]
    between()
    k = 2
    while k < n // 2:
        splits = [_split_bf16(p) for p in ps]
        stacked = []
        for t, s in zip(ts, splits):
            t_hi, t_lo = _split_bf16(t)
            lhs = (jnp.concatenate([t_hi, s[0]], axis=0), jnp.concatenate([t_lo, s[1]], axis=0))
            stacked.append(_pair_matmul3(lhs, s, first))
        ts = [t + r[:n] for t, r in zip(ts, stacked)]
        ps = [r[n:] for r in stacked]
        between()
        k *= 2
    return [t + _pair_matmul3(_split_bf16(t), _split_bf16(p), first) for t, p in zip(ts, ps)]


def _gdn_kernel(q_ref, k_ref, v_ref, z_ref, wq_ref, wk_ref, wv_ref, gcol_ref, grow_ref, pcol_ref, prow_ref,
                nw_ref, o_ref, s_ref, wq_s, u_s, oi_s, kd_s, eg_s, *, n_chunks, first_chunk, unroll):
    c_len = GDN_CHUNK
    d = HEAD_DIM
    nq = GDN_QK_PER_STEP
    nv = 2 * nq
    wq, wk, wv = wq_ref[...], wk_ref[...], wv_ref[...]
    neg_a_col = -jnp.exp(pcol_ref[:, 0:nv])
    dtb_col = pcol_ref[:, nv:2 * nv]
    neg_a_pair = -jnp.exp(prow_ref[0:nq, :])
    dtb_pair = prow_ref[nq:2 * nq, :]
    pair_row = lax.broadcasted_iota(jnp.int32, (c_len, 2 * c_len), 0)
    pair_lane = lax.broadcasted_iota(jnp.int32, (c_len, 2 * c_len), 1)
    pair_col = pair_lane & (c_len - 1)
    first = pair_lane < c_len
    lower_pair = pair_row >= pair_col
    strict_pair = pair_row > pair_col
    lower_f = _tri(c_len).astype(F32)
    r2 = lax.broadcasted_iota(jnp.int32, (2 * c_len, 2 * c_len), 0)
    c2 = lax.broadcasted_iota(jnp.int32, (2 * c_len, 2 * c_len), 1)
    upper_pair = (((r2 < c_len) == (c2 < c_len)) & ((r2 & (c_len - 1)) <= (c2 & (c_len - 1)))).astype(F32)
    norm_w = nw_ref[...]

    def load_chunk(c):
        r0 = pl.multiple_of(c * c_len, c_len)
        valid_col = (r0 + lax.broadcasted_iota(jnp.int32, (c_len, 1), 0)) >= PAD
        valid_pair = (r0 + pair_col[0:1, :]) >= PAD
        gc = gcol_ref[c]
        g_col = jnp.where(valid_col, neg_a_col * _softplus(gc[:, nv:2 * nv] + dtb_col), 0.0)
        g_pair = jnp.where(valid_pair, neg_a_pair * _softplus(grow_ref[c] + dtb_pair), 0.0)
        return dict(
            c=c,
            q=jnp.where(valid_col, _conv_silu(q_ref, wq, r0, c_len), 0.0),
            k=jnp.where(valid_col, _conv_silu(k_ref, wk, r0, c_len), 0.0),
            v=jnp.where(valid_col, _conv_silu(v_ref, wv, r0, c_len), 0.0),
            beta=jnp.where(valid_col, _sigmoid(gc[:, 0:nv]), 0.0),
            cum_col=_hdot(lower_f, g_col),
            cum_pair=_hdot(g_pair, upper_pair),
        )

    def prepare_group(c_first, between):
        chunks = [load_chunk(c_first + j) for j in range(unroll)]
        probs = [dict(ch=ch, a=a) for ch in chunks for a in range(nq)]
        for pr in probs:
            ch, a = pr["ch"], pr["a"]
            q = ch["q"][:, a * d:(a + 1) * d]
            k = ch["k"][:, a * d:(a + 1) * d]
            pr["qn"] = q * lax.rsqrt(jnp.sum(q * q, axis=-1, keepdims=True) + EPS) * (d ** -0.5)
            pr["kn"] = k * lax.rsqrt(jnp.sum(k * k, axis=-1, keepdims=True) + EPS)
        for pr in probs:
            kn_b = pr["kn"].astype(BF16)
            both = lax.dot_general(jnp.concatenate([kn_b, pr["qn"].astype(BF16)], axis=0),
                                   jnp.concatenate([kn_b, kn_b], axis=0), NT, preferred_element_type=F32)
            pr["kk"], pr["qk"] = both[:c_len], both[c_len:]
        between()
        for pr in probs:
            ch, a = pr["ch"], pr["a"]
            h0, h1 = 2 * a, 2 * a + 1
            gcol = jnp.where(first, ch["cum_col"][:, h0:h0 + 1], ch["cum_col"][:, h1:h1 + 1])
            bcol = jnp.where(first, ch["beta"][:, h0:h0 + 1], ch["beta"][:, h1:h1 + 1])
            pr["decay"] = jnp.exp(jnp.where(lower_pair, gcol - ch["cum_pair"][a:a + 1, :], -jnp.inf))
            pr["amat"] = jnp.where(strict_pair, pr["kk"] * bcol * pr["decay"], 0.0)
        ts = _pair_unit_lower_inverses([pr["amat"] for pr in probs], c_len, first, between)
        for pr, t in zip(probs, ts):
            ch, a = pr["ch"], pr["a"]
            rhs = []
            for vh in (2 * a, 2 * a + 1):
                b1 = ch["beta"][:, vh:vh + 1]
                rhs.append(jnp.concatenate([ch["v"][:, vh * d:(vh + 1) * d] * b1,
                                            pr["kn"] * (b1 * jnp.exp(ch["cum_col"][:, vh:vh + 1]))], axis=1))
            zero = jnp.zeros_like(rhs[0])
            pr["uw"] = _bdot(t, jnp.concatenate([jnp.concatenate([rhs[0], zero], axis=1),
                                                 jnp.concatenate([zero, rhs[1]], axis=1)], axis=0))
        for pr in probs:
            uw = pr["uw"]
            zero = jnp.zeros_like(uw[:, :2 * d])
            pr["ow"] = _bdot(pr["qk"] * pr["decay"], jnp.concatenate(
                [jnp.concatenate([uw[:, :2 * d], zero], axis=1),
                 jnp.concatenate([zero, uw[:, 2 * d:]], axis=1)], axis=0))
        for pr in probs:
            ch, a, uw, ow = pr["ch"], pr["a"], pr["uw"], pr["ow"]
            c = ch["c"]
            for i, vh in enumerate((2 * a, 2 * a + 1)):
                gcol1 = ch["cum_col"][:, vh:vh + 1]
                g_last = ch["cum_col"][c_len - 1:c_len, vh:vh + 1]
                u, w = uw[:, 2 * i * d:(2 * i + 1) * d], uw[:, (2 * i + 1) * d:(2 * i + 2) * d]
                oi, qw = ow[:, 2 * i * d:(2 * i + 1) * d], ow[:, (2 * i + 1) * d:(2 * i + 2) * d]
                wq_s[c, vh, pl.ds(0, c_len), :] = w.astype(BF16)
                wq_s[c, vh, pl.ds(c_len, c_len), :] = (pr["qn"] * jnp.exp(gcol1) - qw).astype(BF16)
                u_s[c, vh] = u
                oi_s[c, vh] = oi
                kd_s[c, vh] = (pr["kn"] * jnp.exp(g_last - gcol1)).astype(BF16)
                eg_s[c, vh] = jnp.broadcast_to(jnp.exp(g_last), (8, d))

    def recur_group(c_first):
        heads = range(nv)
        state = dict(s=[s_ref[vh] for vh in heads])

        def products(c):
            state["sq"] = [jnp.dot(wq_s[c, vh], state["s"][vh].astype(BF16), preferred_element_type=F32)
                           for vh in heads]

        def update(c, last):
            r0 = pl.multiple_of(c * c_len, c_len)
            valid_col = (r0 + lax.broadcasted_iota(jnp.int32, (c_len, 1), 0)) >= PAD
            z_all = z_ref[pl.ds(r0, c_len), :].astype(F32)
            sq = state["sq"]
            upd = [_bdot(kd_s[c, vh], u_s[c, vh] - sq[vh][:c_len], TN) for vh in heads]
            state["s"] = [state["s"][vh] * eg_s[c, vh][0:1, :] + upd[vh] for vh in heads]
            outs = []
            for vh in heads:
                o = oi_s[c, vh] + sq[vh][c_len:]
                o = o * lax.rsqrt(jnp.mean(o * o, axis=-1, keepdims=True) + EPS) * norm_w
                outs.append(o * _silu(z_all[:, vh * d:(vh + 1) * d]))
            out = jnp.where(valid_col, jnp.concatenate(outs, axis=1), 0.0)
            o_ref[pl.ds(r0, c_len), :] = out.astype(o_ref.dtype)
            if last:
                for vh in heads:
                    s_ref[vh] = state["s"][vh]

        steps = []
        for j in range(unroll):
            steps.append(functools.partial(products, c_first + j))
            steps.append(functools.partial(update, c_first + j, j == unroll - 1))
        return steps

    def run_between(steps):
        pending = list(steps)

        def between():
            if pending:
                pending.pop(0)()
        return between, pending

    if first_chunk:
        o_ref[pl.ds(0, first_chunk * c_len), :] = jnp.zeros((first_chunk * c_len, nv * d), o_ref.dtype)
    s_ref[...] = jnp.zeros_like(s_ref)
    n_groups = (n_chunks - first_chunk) // unroll

    prepare_group(first_chunk, lambda: None)

    def group_step(g, _):
        between, pending = run_between(recur_group(first_chunk + (g - 1) * unroll))
        prepare_group(first_chunk + g * unroll, between)
        for step in pending:
            step()
        return 0

    lax.fori_loop(1, n_groups, group_step, 0)
    for step in recur_group(first_chunk + (n_groups - 1) * unroll):
        step()


def gdn_core(qkvz, ba, conv_w, a_log, dt_bias, norm_w, batch):
    m = qkvz.shape[0]
    lp = m // batch
    hk = 16
    d = HEAD_DIM
    nq = GDN_QK_PER_STEP
    nv = 2 * nq
    ng = hk // nq
    n_chunks = lp // GDN_CHUNK
    first_chunk, unroll = next((f, u) for u in range(GDN_MAX_UNROLL, 0, -1) for f in (1, 0)
                               if (n_chunks - f) % u == 0)
    qkvz = qkvz.reshape(batch, lp, qkvz.shape[1])
    b_log = ba[:, :2 * hk].reshape(batch, n_chunks, GDN_CHUNK, ng, nv)
    a_log_t = ba[:, 2 * hk:].reshape(batch, n_chunks, GDN_CHUNK, ng, nq, 2)
    gcol = jnp.transpose(jnp.concatenate([b_log, a_log_t.reshape(b_log.shape)], axis=-1), (0, 3, 1, 2, 4))
    grow = jnp.transpose(a_log_t, (0, 3, 1, 4, 5, 2)).reshape(batch, ng, n_chunks, nq, 2 * GDN_CHUNK)
    p = jnp.concatenate([a_log.reshape(ng, nv), dt_bias.reshape(ng, nv)], axis=-1).astype(F32)
    pcol = p.reshape(ng, 1, 2 * nv)
    prow = jnp.concatenate([jnp.repeat(a_log.reshape(ng, nq, 2), GDN_CHUNK, axis=-1),
                            jnp.repeat(dt_bias.reshape(ng, nq, 2), GDN_CHUNK, axis=-1)], axis=1).astype(F32)
    qb, vb = nq * d, nv * d
    kern = functools.partial(_gdn_kernel, n_chunks=n_chunks, first_chunk=first_chunk, unroll=unroll)
    out = pl.pallas_call(
        kern,
        out_shape=jax.ShapeDtypeStruct((batch, lp, 2 * hk * d), BF16),
        grid=(batch, ng),
        in_specs=[
            pl.BlockSpec((None, lp, qb), lambda b, h: (b, 0, h)),
            pl.BlockSpec((None, lp, qb), lambda b, h: (b, 0, ng + h)),
            pl.BlockSpec((None, lp, vb), lambda b, h: (b, 0, ng + h)),
            pl.BlockSpec((None, lp, vb), lambda b, h: (b, 0, 2 * ng + h)),
            pl.BlockSpec((CONV_W, qb), lambda b, h: (0, h)),
            pl.BlockSpec((CONV_W, qb), lambda b, h: (0, ng + h)),
            pl.BlockSpec((CONV_W, vb), lambda b, h: (0, ng + h)),
            pl.BlockSpec((None, None, n_chunks, GDN_CHUNK, 2 * nv), lambda b, h: (b, h, 0, 0, 0)),
            pl.BlockSpec((None, None, n_chunks, nq, 2 * GDN_CHUNK), lambda b, h: (b, h, 0, 0, 0)),
            pl.BlockSpec((None, 1, 2 * nv), lambda b, h: (h, 0, 0)),
            pl.BlockSpec((None, 2 * nq, 2 * GDN_CHUNK), lambda b, h: (h, 0, 0)),
            pl.BlockSpec((1, d), lambda b, h: (0, 0)),
        ],
        out_specs=pl.BlockSpec((None, lp, vb), lambda b, h: (b, 0, h)),
        scratch_shapes=[
            pltpu.VMEM((nv, d, d), F32),
            pltpu.VMEM((n_chunks, nv, 2 * GDN_CHUNK, d), BF16),
            pltpu.VMEM((n_chunks, nv, GDN_CHUNK, d), F32),
            pltpu.VMEM((n_chunks, nv, GDN_CHUNK, d), F32),
            pltpu.VMEM((n_chunks, nv, GDN_CHUNK, d), BF16),
            pltpu.VMEM((n_chunks, nv, 8, d), F32),
        ],
        compiler_params=_params("parallel", "parallel"),
        name="gdn_core",
    )(qkvz, qkvz, qkvz, qkvz, conv_w, conv_w, conv_w, gcol, grow, pcol, prow, norm_w.reshape(1, d).astype(F32))
    return out.reshape(m, 2 * hk * d)


def _ssd_kernel(z_ref, x_ref, b_ref, c_ref, wx_ref, wb_ref, wc_ref, bx_ref, bb_ref, bc_ref, dcol_ref, drow_ref,
                pcol_ref, prow_ref, nw_ref, o_ref, s_ref, *, n_chunks):
    c_len = SSM_CHUNK
    hp = SSM_HEAD_DIM
    nh = SSM_HPG
    width = nh * hp
    s_ref[...] = jnp.zeros_like(s_ref)

    wx, wb, wc = wx_ref[...], wb_ref[...], wc_ref[...]
    bx, bb, bc = bx_ref[...], bb_ref[...], bc_ref[...]
    neg_a_col = -jnp.exp(pcol_ref[0:1, :])
    dtb_col = pcol_ref[1:2, :]
    d_skip = pcol_ref[2:3, :]
    neg_a_row = -jnp.exp(prow_ref[:, 0:1])
    dtb_row = prow_ref[:, 1:2]
    lower = _tri(c_len)
    lower_f = lower.astype(F32)
    upper_f = (lax.broadcasted_iota(jnp.int32, (c_len, c_len), 0)
               <= lax.broadcasted_iota(jnp.int32, (c_len, c_len), 1)).astype(F32)
    lane = lax.broadcasted_iota(jnp.int32, (1, 2 * hp), 1)

    def spread(cols):
        return jnp.concatenate([jnp.where(lane < hp, cols[:, 2 * t:2 * t + 1], cols[:, 2 * t + 1:2 * t + 2])
                                for t in range(nh // 2)], axis=1)

    d_skip_x = spread(d_skip)
    norm_w = nw_ref[...]

    def body(c, _):
        r0 = pl.multiple_of(c * c_len, c_len)
        valid_col = (r0 + lax.broadcasted_iota(jnp.int32, (c_len, 1), 0)) >= PAD
        valid_row = (r0 + lax.broadcasted_iota(jnp.int32, (1, c_len), 1)) >= PAD
        xs = jnp.where(valid_col, _conv_silu(x_ref, wx, r0, c_len, bx), 0.0)
        bm = jnp.where(valid_col, _conv_silu(b_ref, wb, r0, c_len, bb), 0.0)
        cm = jnp.where(valid_col, _conv_silu(c_ref, wc, r0, c_len, bc), 0.0)
        dt_col = jnp.where(valid_col, _softplus(dcol_ref[c] + dtb_col), 0.0)
        dt_row = jnp.where(valid_row, _softplus(drow_ref[c] + dtb_row), 0.0)
        cum_col = _hdot(lower_f, dt_col * neg_a_col)
        cum_row = _hdot(dt_row * neg_a_row, upper_f)
        cum_last = cum_col[c_len - 1:c_len, :]

        xdt = xs * spread(dt_col)
        cb = _bdot(cm, bm, NT)
        s = s_ref[...]
        y = _bdot(cm, s) * spread(jnp.exp(cum_col))
        s_ref[...] = s * spread(jnp.exp(cum_last)) + _bdot(bm, xdt * spread(jnp.exp(cum_last - cum_col)), TN)
        diag = []
        for pair in range(nh // 2):
            sc = []
            for j in (2 * pair, 2 * pair + 1):
                lmat = jnp.exp(jnp.where(lower, cum_col[:, j:j + 1] - cum_row[j:j + 1, :], -jnp.inf))
                sc.append((cb * lmat).astype(BF16))
            xp = xdt[:, pair * 2 * hp:(pair + 1) * 2 * hp]
            rhs = jnp.concatenate([jnp.where(lane < hp, xp, 0.0), jnp.where(lane >= hp, xp, 0.0)], axis=0)
            diag.append(_bdot(jnp.concatenate(sc, axis=1), rhs))
        y = y + jnp.concatenate(diag, axis=1) + xs * d_skip_x
        y = y * _silu(z_ref[pl.ds(r0, c_len), :].astype(F32))
        y = y * lax.rsqrt(jnp.mean(y * y, axis=-1, keepdims=True) + EPS) * norm_w
        o_ref[pl.ds(r0, c_len), :] = jnp.where(valid_col, y, 0.0).astype(o_ref.dtype)
        return 0

    lax.fori_loop(0, n_chunks, body, 0)


def ssd_core(zx, dt_raw, conv_w, conv_b, a_log, dt_bias, d_skip, norm_w, batch):
    m = zx.shape[0]
    lp = m // batch
    ng = 8
    width = SSM_HPG * SSM_HEAD_DIM
    d_inner = ng * width
    n_chunks = lp // SSM_CHUNK
    zx = zx.reshape(batch, lp, zx.shape[1])
    d5 = dt_raw.reshape(batch, n_chunks, SSM_CHUNK, ng, SSM_HPG)
    dcol = jnp.transpose(d5, (0, 3, 1, 2, 4))
    drow = jnp.transpose(d5, (0, 3, 1, 4, 2))
    p3 = jnp.stack([a_log.reshape(ng, SSM_HPG), dt_bias.reshape(ng, SSM_HPG), d_skip.reshape(ng, SSM_HPG)],
                   axis=1).astype(F32)
    prow = jnp.transpose(p3, (0, 2, 1))
    conv_b = conv_b.reshape(1, -1).astype(F32)
    xo, bo, co = d_inner // width, d_inner // SSM_STATE, (d_inner + ng * SSM_STATE) // SSM_STATE
    kern = functools.partial(_ssd_kernel, n_chunks=n_chunks)
    out = pl.pallas_call(
        kern,
        out_shape=jax.ShapeDtypeStruct((batch, lp, d_inner), BF16),
        grid=(batch, ng),
        in_specs=[
            pl.BlockSpec((None, lp, width), lambda b, g: (b, 0, g)),
            pl.BlockSpec((None, lp, width), lambda b, g: (b, 0, xo + g)),
            pl.BlockSpec((None, lp, SSM_STATE), lambda b, g: (b, 0, 2 * bo + g)),
            pl.BlockSpec((None, lp, SSM_STATE), lambda b, g: (b, 0, bo + co + g)),
            pl.BlockSpec((CONV_W, width), lambda b, g: (0, g)),
            pl.BlockSpec((CONV_W, SSM_STATE), lambda b, g: (0, bo + g)),
            pl.BlockSpec((CONV_W, SSM_STATE), lambda b, g: (0, co + g)),
            pl.BlockSpec((1, width), lambda b, g: (0, g)),
            pl.BlockSpec((1, SSM_STATE), lambda b, g: (0, bo + g)),
            pl.BlockSpec((1, SSM_STATE), lambda b, g: (0, co + g)),
            pl.BlockSpec((None, None, n_chunks, SSM_CHUNK, SSM_HPG), lambda b, g: (b, g, 0, 0, 0)),
            pl.BlockSpec((None, None, n_chunks, SSM_HPG, SSM_CHUNK), lambda b, g: (b, g, 0, 0, 0)),
            pl.BlockSpec((None, 3, SSM_HPG), lambda b, g: (g, 0, 0)),
            pl.BlockSpec((None, SSM_HPG, 3), lambda b, g: (g, 0, 0)),
            pl.BlockSpec((1, width), lambda b, g: (0, g)),
        ],
        out_specs=pl.BlockSpec((None, lp, width), lambda b, g: (b, 0, g)),
        scratch_shapes=[pltpu.VMEM((SSM_STATE, width), F32)],
        compiler_params=_params("parallel", "parallel"),
        name="ssd_core",
    )(zx, zx, zx, zx, conv_w, conv_w, conv_w, conv_b, conv_b, conv_b, dcol, drow, p3, prow,
      norm_w.reshape(1, d_inner).astype(F32))
    return out.reshape(m, d_inner)


def _rope_tables(lp):
    inv = ROPE_THETA ** (-jnp.arange(0, MLA_ROPE, 2, dtype=F32) / MLA_ROPE)
    pos = jnp.maximum(jnp.arange(lp, dtype=F32) - PAD, 0.0)
    ang = pos[:, None] * inv[None, :]
    return jnp.tile(jnp.cos(ang), (1, 4)), jnp.tile(jnp.sin(ang), (1, 4))


def _pair_head(lane):
    return lax.shift_right_logical(lane, 5) & 1


def _rope_pair(t, cos, sin):
    lane = lax.broadcasted_iota(jnp.int32, (1, t.shape[1]), 1)
    partner = pltpu.roll(t, t.shape[1] // 2, axis=1)
    return t * cos + jnp.where(lane < t.shape[1] // 2, -partner, partner) * sin


def _qprep_kernel(q_ref, gn_ref, gp_ref, cos_ref, sin_ref, o_ref, *, n_heads, scale):
    d = HEAD_DIM
    cos, sin = cos_ref[...], sin_ref[...]
    for h in range(n_heads):
        x = q_ref[:, h * d:(h + 1) * d]
        y = x * lax.rsqrt(jnp.mean(x * x, axis=-1, keepdims=True) + EPS) * gn_ref[:, h * d:(h + 1) * d]
        o_ref[:, h * d:(h + 1) * d] = (y * scale).astype(o_ref.dtype)
    r = lax.broadcasted_iota(jnp.int32, (d, d), 0)
    c = lax.broadcasted_iota(jnp.int32, (d, d), 1)
    same_head = (_pair_head(r) == _pair_head(c)).astype(F32)
    base = n_heads * d
    for p in range(n_heads // 2):
        x = q_ref[:, base + p * d:base + (p + 1) * d]
        ms = _hdot(x * x, same_head) * (1.0 / MLA_ROPE)
        y = x * lax.rsqrt(ms + EPS) * gp_ref[:, p * d:(p + 1) * d]
        o_ref[:, base + p * d:base + (p + 1) * d] = (_rope_pair(y, cos, sin) * scale).astype(o_ref.dtype)


def _kprep_kernel(kv_ref, kpe_ref, gn_ref, gp_ref, cos_ref, sin_ref, k_ref, v_ref, pe_ref, *, n_heads):
    d = HEAD_DIM
    for h in range(n_heads):
        x = kv_ref[:, h * d:(h + 1) * d]
        y = x * lax.rsqrt(jnp.mean(x * x, axis=-1, keepdims=True) + EPS) * gn_ref[:, h * d:(h + 1) * d]
        k_ref[:, h * d:(h + 1) * d] = y.astype(k_ref.dtype)
    v_ref[...] = kv_ref[:, n_heads * d:].astype(v_ref.dtype)
    x = kpe_ref[...]
    y = x * lax.rsqrt(jnp.mean(x * x, axis=-1, keepdims=True) + EPS) * gp_ref[...]
    pe_ref[...] = _rope_pair(y, cos_ref[...], sin_ref[...]).astype(pe_ref.dtype)


def _flash_kernel(qn_ref, qpe_ref, kn_ref, kpe_ref, v_ref, o_ref, *, kv_blk):
    d = HEAD_DIM
    blk = qn_ref.shape[0]
    lp = kn_ref.shape[0]
    i = pl.program_id(2)
    lane = lax.broadcasted_iota(jnp.int32, (1, d), 1)
    qpe = qpe_ref[...]
    qf = []
    for hh in range(2):
        mine = _pair_head(lane) == hh
        qf.append(jnp.concatenate([qn_ref[:, hh * d:(hh + 1) * d], jnp.where(mine, qpe, jnp.zeros_like(qpe))],
                                  axis=1))
    qpos = i * blk + lax.broadcasted_iota(jnp.int32, (blk, 1), 0)
    q_limit = jnp.maximum(qpos, PAD)

    def body(j, carry, masked):
        c0 = pl.multiple_of(jnp.minimum(PAD + j * kv_blk, lp - kv_blk), HALO)
        kpe = kpe_ref[pl.ds(c0, kv_blk), :]
        scores = []
        for hh in range(2):
            kf = jnp.concatenate([kn_ref[pl.ds(c0, kv_blk), hh * d:(hh + 1) * d], kpe], axis=1)
            scores.append(lax.dot_general(qf[hh], kf, NT, preferred_element_type=F32))
        if masked:
            kpos = c0 + lax.broadcasted_iota(jnp.int32, (1, kv_blk), 1)
            ok = (kpos >= PAD + j * kv_blk) & (kpos <= q_limit)
            scores = [jnp.where(ok, s, -1e30) for s in scores]
        stats = []
        for hh in range(2):
            m, l, _ = carry[hh]
            s = scores[hh]
            m_new = jnp.maximum(m, jnp.max(s, axis=-1, keepdims=True))
            p = jnp.exp(s - m_new)
            alpha = jnp.exp(m - m_new)
            stats.append((m_new, alpha * l + jnp.sum(p, axis=-1, keepdims=True), alpha, p.astype(BF16)))
        new = []
        for hh in range(2):
            m_new, l, alpha, p = stats[hh]
            pv = jnp.dot(p, v_ref[pl.ds(c0, kv_blk), hh * d:(hh + 1) * d], preferred_element_type=F32)
            new.append((m_new, l, alpha * carry[hh][2] + pv))
        return tuple(new)

    init = tuple((jnp.full((blk, 1), -1e30, F32), jnp.zeros((blk, 1), F32), jnp.zeros((blk, d), F32))
                 for _ in range(2))
    n_free = lax.div(jnp.maximum(i * blk + 1 - PAD, 0), kv_blk)
    n_kv = lax.div(i * blk + blk - PAD + kv_blk - 1, kv_blk)
    res = lax.fori_loop(0, n_free, functools.partial(body, masked=False), init)
    res = lax.fori_loop(n_free, n_kv, functools.partial(body, masked=True), res)
    out = jnp.concatenate([acc / l for (_, l, acc) in res], axis=1)
    o_ref[...] = jnp.where(qpos >= PAD, out, 0.0).astype(o_ref.dtype)


def mla_attention(q_raw, kv_raw, kpe_src, kpe_col_block, q_norm, k_norm, batch):
    m = q_raw.shape[0]
    lp = m // batch
    nh = 16
    d = HEAD_DIM
    half = MLA_ROPE // 2
    scale = (d + MLA_ROPE) ** -0.5
    cos, sin = _rope_tables(lp)
    qn_gain = jnp.tile(q_norm[:d], nh).reshape(1, nh * d).astype(F32)
    kn_gain = jnp.tile(k_norm[:d], nh).reshape(1, nh * d).astype(F32)
    qf, qs = q_norm[d:d + half], q_norm[d + half:]
    kf, ks = k_norm[d:d + half], k_norm[d + half:]
    qp_gain = jnp.tile(jnp.concatenate([qf, qf, qs, qs]), nh // 2).reshape(1, nh // 2 * d).astype(F32)
    kp_gain = jnp.concatenate([kf, kf, ks, ks]).reshape(1, d).astype(F32)
    tp = ATT_PREP_ROWS
    nt = lp // tp
    qw = q_raw.shape[1]

    q_prep = pl.pallas_call(
        functools.partial(_qprep_kernel, n_heads=nh, scale=scale),
        out_shape=jax.ShapeDtypeStruct((batch, lp, qw), BF16),
        grid=(batch, nt),
        in_specs=[
            pl.BlockSpec((None, tp, qw), lambda b, t: (b, t, 0)),
            pl.BlockSpec((1, nh * d), lambda b, t: (0, 0)),
            pl.BlockSpec((1, nh // 2 * d), lambda b, t: (0, 0)),
            pl.BlockSpec((tp, d), lambda b, t: (t, 0)),
            pl.BlockSpec((tp, d), lambda b, t: (t, 0)),
        ],
        out_specs=pl.BlockSpec((None, tp, qw), lambda b, t: (b, t, 0)),
        compiler_params=_params("parallel", "parallel"),
        name="mla_q_prep",
    )(q_raw.reshape(batch, lp, qw), qn_gain, qp_gain, cos, sin)

    kvw = kv_raw.shape[1]
    kpe3 = kpe_src.reshape(batch, lp, kpe_src.shape[1])
    k_prep, v_prep, pe_prep = pl.pallas_call(
        functools.partial(_kprep_kernel, n_heads=nh),
        out_shape=(jax.ShapeDtypeStruct((batch, lp, nh * d), BF16),
                   jax.ShapeDtypeStruct((batch, lp, nh * d), BF16),
                   jax.ShapeDtypeStruct((batch, lp, d), BF16)),
        grid=(batch, nt),
        in_specs=[
            pl.BlockSpec((None, tp, kvw), lambda b, t: (b, t, 0)),
            pl.BlockSpec((None, tp, d), lambda b, t: (b, t, kpe_col_block)),
            pl.BlockSpec((1, nh * d), lambda b, t: (0, 0)),
            pl.BlockSpec((1, d), lambda b, t: (0, 0)),
            pl.BlockSpec((tp, d), lambda b, t: (t, 0)),
            pl.BlockSpec((tp, d), lambda b, t: (t, 0)),
        ],
        out_specs=(pl.BlockSpec((None, tp, nh * d), lambda b, t: (b, t, 0)),
                   pl.BlockSpec((None, tp, nh * d), lambda b, t: (b, t, 0)),
                   pl.BlockSpec((None, tp, d), lambda b, t: (b, t, 0))),
        compiler_params=_params("parallel", "parallel"),
        name="mla_kv_prep",
    )(kv_raw.reshape(batch, lp, kvw), kpe3, kn_gain, kp_gain, cos, sin)

    np_ = nh // 2
    tq = next(t for t in range(min(lp, ATT_MAX_Q_ROWS) // 16 * 16, 0, -16) if lp % t == 0)
    out = pl.pallas_call(
        functools.partial(_flash_kernel, kv_blk=min(ATT_KV_BLK, lp)),
        out_shape=jax.ShapeDtypeStruct((batch, lp, nh * d), BF16),
        grid=(batch, np_, lp // tq),
        in_specs=[
            pl.BlockSpec((None, tq, 2 * d), lambda b, p, i: (b, i, p)),
            pl.BlockSpec((None, tq, d), lambda b, p, i: (b, i, nh + p)),
            pl.BlockSpec((None, lp, 2 * d), lambda b, p, i: (b, 0, p)),
            pl.BlockSpec((None, lp, d), lambda b, p, i: (b, 0, 0)),
            pl.BlockSpec((None, lp, 2 * d), lambda b, p, i: (b, 0, p)),
        ],
        out_specs=pl.BlockSpec((None, tq, 2 * d), lambda b, p, i: (b, i, p)),
        compiler_params=_params("parallel", "parallel", "arbitrary"),
        name="mla_flash",
    )(q_prep, q_prep, k_prep, pe_prep, v_prep)
    return out.reshape(m, nh * d)


def _gdn_layer(stream, w_in_all, layer, conv_w, a_log, dt_bias, norm_w, w_out_all, batch):
    h, hb, ss = stream
    n_gate = 2 * a_log.shape[0]
    n_big = w_in_all.shape[2] - n_gate
    qkvz = scaled_matmul(hb, ss, None, w_in_all, layer=layer, n=n_big, name="gdn_in_proj")
    ba = scaled_matmul(hb, ss, None, w_in_all[layer, :, n_big:], out_dtype=F32, name="gdn_gate_proj")
    o = gdn_core(qkvz, ba, conv_w.astype(F32), a_log, dt_bias, norm_w, batch)
    return matmul_residual(o, w_out_all, h, layer=layer, name="gdn_out_proj")


def _mla_layer(stream, gain, w_in, norm_q_lat, norm_kv_lat, w_uq, w_ukv, q_norm, k_norm, w_out_all, layer, batch):
    h, hb, ss = stream
    nh, d, r = 16, HEAD_DIM, MLA_ROPE
    q_lora, kv_lora = norm_q_lat.shape[0], norm_kv_lat.shape[0]
    half = r // 2
    w_cq, w_ckv, w_pe = w_in[:, :q_lora], w_in[:, q_lora:q_lora + kv_lora], w_in[:, q_lora + kv_lora:]
    w_pe_t = jnp.concatenate([w_pe[:, :half], w_pe[:, :half], w_pe[:, half:], w_pe[:, half:]], axis=1)
    gap = (-(kv_lora + d)) % q_lora
    w_lat = jnp.concatenate([w_ckv, w_pe_t, jnp.zeros((w_in.shape[0], gap), w_in.dtype), w_cq], axis=1)
    lat = scaled_matmul(hb, ss, gain, w_lat, out_dtype=F32, name="mla_in_proj")
    wq = w_uq.reshape(q_lora, nh, d + r)
    wq_pe = wq[:, :, d:].reshape(q_lora, nh // 2, 2, 2, half)
    wq_pe = jnp.transpose(wq_pe, (0, 1, 3, 2, 4)).reshape(q_lora, nh // 2 * d)
    wq_all = jnp.concatenate([wq[:, :, :d].reshape(q_lora, nh * d), wq_pe], axis=1)
    q_raw = norm_matmul(lat, norm_q_lat, wq_all.astype(BF16), x_col_block=(kv_lora + d + gap) // q_lora,
                        out_dtype=F32, name="mla_q_up")
    wkv = w_ukv.reshape(kv_lora, nh, 2 * d)
    wkv_all = jnp.concatenate([wkv[:, :, :d].reshape(kv_lora, nh * d), wkv[:, :, d:].reshape(kv_lora, nh * d)],
                              axis=1)
    kv_raw = norm_matmul(lat, norm_kv_lat, wkv_all.astype(BF16), x_col_block=0, out_dtype=F32, name="mla_kv_up")
    o = mla_attention(q_raw, kv_raw, lat, kv_lora // d, q_norm, k_norm, batch)
    return matmul_residual(o, w_out_all, h, layer=layer, name="mla_out_proj")


def _ssm_layer(stream, w_in_all, layer, conv_w, conv_b, a_log, dt_bias, d_skip, norm_w, w_out_all, batch):
    h, hb, ss = stream
    n_dt = a_log.shape[0]
    n_big = w_in_all.shape[2] - n_dt
    zx = scaled_matmul(hb, ss, None, w_in_all, layer=layer, n=n_big, name="ssm_in_proj")
    dt_raw = scaled_matmul(hb, ss, None, w_in_all[layer, :, n_big:], out_dtype=F32, name="ssm_dt_proj")
    y = ssd_core(zx, dt_raw, conv_w.astype(F32), conv_b, a_log, dt_bias, d_skip, norm_w, batch)
    return matmul_residual(y, w_out_all, h, layer=layer, name="ssm_out_proj")


def _mlp_layer(stream, gain, w_up_all, w_down_all, layer):
    h, hb, ss = stream
    act = scaled_matmul(hb, ss, gain, w_up_all, layer=layer, act="relu2", name="mlp_up")
    return matmul_residual(act, w_down_all, h, layer=layer, name="mlp_down")


def kernel(x, meta_tokens, norm_mix, norm_mlp, mlp_w_up, mlp_w_down, gdn_w_in, gdn_conv_w, gdn_a_log, gdn_dt_bias, gdn_norm, gdn_w_out, mla_w_in, mla_norm_q_lat, mla_norm_kv_lat, mla_w_uq, mla_w_ukv, mla_q_norm, mla_k_norm, mla_w_out, ssm_w_in, ssm_conv_w, ssm_conv_b, ssm_a_log, ssm_dt_bias, ssm_d, ssm_norm, ssm_w_out):
    batch, seq, dm = x.shape
    depth = norm_mix.shape[0]
    lp = ROW_TILE + seq
    meta = jnp.broadcast_to(meta_tokens.astype(x.dtype)[None], (batch, N_META, dm))
    h = jnp.concatenate([jnp.zeros((batch, PAD, dm), x.dtype), meta, x], axis=1).reshape(batch * lp, dm)
    stream = (h,) + tuple(stream_entry(h))
    down_b, gdn_out_b, ssm_out_b, mla_out_b = (w.astype(BF16) for w in (mlp_w_down, gdn_w_out, ssm_w_out, mla_w_out))
    gdn_in_b = (gdn_w_in * norm_mix[0::3][:, :, None]).astype(BF16)
    ssm_in_b = (ssm_w_in * norm_mix[2::3][:, :, None]).astype(BF16)
    ia = ib = ic = 0
    for i in range(depth):
        kind = i % 3
        if kind == 0:
            stream = _gdn_layer(stream, gdn_in_b, ia, gdn_conv_w[ia], gdn_a_log[ia], gdn_dt_bias[ia], gdn_norm[ia],
                                gdn_out_b, batch)
            ia += 1
        elif kind == 1:
            stream = _mla_layer(stream, norm_mix[i], mla_w_in[ib], mla_norm_q_lat[ib], mla_norm_kv_lat[ib],
                                mla_w_uq[ib], mla_w_ukv[ib], mla_q_norm[ib], mla_k_norm[ib], mla_out_b, ib, batch)
            ib += 1
        else:
            stream = _ssm_layer(stream, ssm_in_b, ic, ssm_conv_w[ic], ssm_conv_b[ic], ssm_a_log[ic], ssm_dt_bias[ic],
                                ssm_d[ic], ssm_norm[ic], ssm_out_b, batch)
            ic += 1
        stream = _mlp_layer(stream, norm_mlp[i], mlp_w_up, down_b, i)
    return stream[0].reshape(batch, lp, dm)[:, ROW_TILE:]
```

```python
import functools
import math

import jax
import jax.numpy as jnp
from jax import lax
from jax.experimental import pallas as pl
from jax.experimental.pallas import tpu as pltpu

F32 = jnp.float32
BF16 = jnp.bfloat16
HI = lax.Precision.HIGHEST

EPS = 1e-6
N_META = 16
ROW_TILE = 128
PAD = ROW_TILE - N_META
LANES = 128
HALO = 16
CONV_W = 4

HEAD_DIM = 128
GDN_CHUNK = 64
SSM_CHUNK = 128
SSM_HEAD_DIM = 64
SSM_HPG = 8
SSM_STATE = 128
MLA_ROPE = 64
ROPE_THETA = 10000.0
ATT_PREP_ROWS = 128
ATT_MAX_Q_ROWS = 320
ATT_KV_BLK = 512
GDN_QK_PER_STEP = 2
SSM_UNROLL = 2
GDN_MAX_UNROLL = 3

VMEM_LIMIT = 56 * 1024 * 1024
MAX_ROW_TILE = 1088
MAX_K_TILE = 2048
NT = (((1,), (1,)), ((), ()))
TN = (((0,), (0,)), ((), ()))


def _row_tile(m, cap=MAX_ROW_TILE):
    for t in range(min(m, cap) // 64 * 64, 0, -64):
        if m % t == 0:
            return t
    raise ValueError(f"row count {m} has no tile that is a multiple of 64")


def _col_tile(n, cap=1024):
    if n <= cap:
        return n
    for t in range(cap, 0, -128):
        if n % t == 0:
            return t
    raise ValueError(f"column count {n} has no tile that is a multiple of 128")


def _params(*sem):
    return pltpu.CompilerParams(dimension_semantics=sem, vmem_limit_bytes=VMEM_LIMIT)


def _bdot(a, b, dims=None):
    a = a.astype(BF16)
    b = b.astype(BF16)
    if dims is None:
        return jnp.dot(a, b, preferred_element_type=F32)
    return lax.dot_general(a, b, dims, preferred_element_type=F32)


def _hdot(a, b):
    return jnp.dot(a, b, precision=HI, preferred_element_type=F32)


def _sigmoid(x):
    return 0.5 + 0.5 * jnp.tanh(0.5 * x)


def _silu(x):
    return x * _sigmoid(x)


def _softplus(x):
    return jnp.maximum(x, 0.0) + jnp.log(1.0 + jnp.exp(-jnp.abs(x)))


def _norm_matmul_kernel(x_ref, g_ref, w_ref, o_ref, xn_ref, *, sub):
    @pl.when(pl.program_id(1) == 0)
    def _():
        def body(r, _):
            r0 = pl.multiple_of(r * sub, sub)
            x = x_ref[pl.ds(r0, sub), :].astype(F32)
            ms = jnp.mean(x * x, axis=-1, keepdims=True)
            xn_ref[pl.ds(r0, sub), :] = (x * lax.rsqrt(ms + EPS) * g_ref[...]).astype(BF16)
            return 0

        lax.fori_loop(0, x_ref.shape[0] // sub, body, 0)

    o_ref[...] = jnp.dot(xn_ref[...], w_ref[...], preferred_element_type=F32).astype(o_ref.dtype)


def norm_matmul(x, gain, w, *, x_col_block=0, out_dtype=BF16, name):
    m = x.shape[0]
    k, n = w.shape
    tm, tn = _row_tile(m), _col_tile(n)
    return pl.pallas_call(
        functools.partial(_norm_matmul_kernel, sub=64),
        out_shape=jax.ShapeDtypeStruct((m, n), out_dtype),
        grid=(m // tm, n // tn),
        in_specs=[
            pl.BlockSpec((tm, k), lambda i, j: (i, x_col_block)),
            pl.BlockSpec((1, k), lambda i, j: (0, 0)),
            pl.BlockSpec((k, tn), lambda i, j: (0, j)),
        ],
        out_specs=pl.BlockSpec((tm, tn), lambda i, j: (i, j)),
        scratch_shapes=[pltpu.VMEM((tm, k), BF16)],
        compiler_params=_params("parallel", "arbitrary"),
        name=name,
    )(x, gain.reshape(1, k).astype(F32), w)


def _stats_rows(tm, cap=272):
    return next(t for t in range(min(tm, cap) // 16 * 16, 0, -16) if tm % t == 0)


def _stream_stats(h_ref, g_ref, hb_ref, ss_ref, sub):
    def body(r, _):
        r0 = pl.multiple_of(r * sub, sub)
        x = h_ref[pl.ds(r0, sub), :]
        hb_ref[pl.ds(r0, sub), :] = (x * g_ref[...]).astype(BF16)
        ss_ref[pl.ds(r0, sub), :] = jnp.broadcast_to(jnp.sum(x * x, axis=-1, keepdims=True), (sub, ss_ref.shape[1]))
        return 0

    lax.fori_loop(0, h_ref.shape[0] // sub, body, 0)


def _matmul_residual_kernel(a_ref, w_ref, h_ref, g_ref, o_ref, hb_ref, ss_ref, *, sub):
    @pl.when(pl.program_id(2) == 0)
    def _():
        o_ref[...] = h_ref[...]

    o_ref[...] += jnp.dot(a_ref[...], w_ref[...], preferred_element_type=F32)

    @pl.when(pl.program_id(2) == pl.num_programs(2) - 1)
    def _():
        _stream_stats(o_ref, g_ref, hb_ref, ss_ref, sub)


def _stream_out(m, n, tm, tn, index):
    shapes = (jax.ShapeDtypeStruct((m, n), F32), jax.ShapeDtypeStruct((m, n), BF16),
              jax.ShapeDtypeStruct((n // tn, m, LANES), F32))
    specs = (pl.BlockSpec((tm, tn), lambda *g: index(*g)),
             pl.BlockSpec((tm, tn), lambda *g: index(*g)),
             pl.BlockSpec((None, tm, LANES), lambda *g: (index(*g)[1], index(*g)[0], 0)))
    return shapes, specs


def matmul_residual(a, w, h, next_gain, *, layer=0, name):
    if w.ndim == 2:
        w = w[None]
    m, k = a.shape
    n = w.shape[2]
    tm, tn, tk = _row_tile(m), _col_tile(n), _col_tile(k, MAX_K_TILE)
    shapes, specs = _stream_out(m, n, tm, tn, lambda i, j, kk: (i, j))
    return pl.pallas_call(
        functools.partial(_matmul_residual_kernel, sub=_stats_rows(tm)),
        out_shape=shapes,
        grid=(m // tm, n // tn, k // tk),
        in_specs=[
            pl.BlockSpec((tm, tk), lambda i, j, kk: (i, kk)),
            pl.BlockSpec((None, tk, tn), lambda i, j, kk: (layer, kk, j)),
            pl.BlockSpec((tm, tn), lambda i, j, kk: (i, j)),
            pl.BlockSpec((1, tn), lambda i, j, kk: (0, j)),
        ],
        out_specs=specs,
        compiler_params=_params("parallel", "parallel", "arbitrary"),
        name=name,
    )(a, w, h, next_gain.reshape(1, n).astype(F32))


def _stream_entry_kernel(h_ref, g_ref, hb_ref, ss_ref, *, sub):
    _stream_stats(h_ref, g_ref, hb_ref, ss_ref, sub)


def stream_entry(h, next_gain):
    m, n = h.shape
    tm, tn = _row_tile(m), _col_tile(n)
    shapes, specs = _stream_out(m, n, tm, tn, lambda i, j: (i, j))
    return pl.pallas_call(
        functools.partial(_stream_entry_kernel, sub=_stats_rows(tm)),
        out_shape=shapes[1:],
        grid=(m // tm, n // tn),
        in_specs=[pl.BlockSpec((tm, tn), lambda i, j: (i, j)), pl.BlockSpec((1, tn), lambda i, j: (0, j))],
        out_specs=specs[1:],
        compiler_params=_params("parallel", "parallel"),
        name="stream_entry",
    )(h, next_gain.reshape(1, n).astype(F32))


def _scaled_matmul_kernel(x_ref, ss_ref, w_ref, o_ref, *scratch, act, inv_k, sub):
    if scratch:
        wb_ref, = scratch

        @pl.when(pl.program_id(1) == 0)
        def _():
            def body(r, _):
                r0 = pl.multiple_of(r * sub, sub)
                wb_ref[pl.ds(r0, sub), :] = w_ref[pl.ds(r0, sub), :].astype(BF16)
                return 0

            lax.fori_loop(0, w_ref.shape[0] // sub, body, 0)
    else:
        wb_ref = w_ref

    ss = ss_ref[0]
    for t in range(1, ss_ref.shape[0]):
        ss = ss + ss_ref[t]
    y = jnp.dot(x_ref[...], wb_ref[...], preferred_element_type=F32) * lax.rsqrt(ss[:, 0:1] * inv_k + EPS)
    if act == "relu2":
        y = jnp.square(jnp.maximum(y, 0.0))
    o_ref[...] = y.astype(o_ref.dtype)


def scaled_matmul(hb, ss, w, *, layer=0, n=None, act=None, out_dtype=BF16, name):
    if w.ndim == 2:
        w = w[None]
    m, k = hb.shape
    n = w.shape[2] if n is None else n
    tm, tn = _row_tile(m), _col_tile(n)
    return pl.pallas_call(
        functools.partial(_scaled_matmul_kernel, act=act, inv_k=1.0 / k, sub=256),
        out_shape=jax.ShapeDtypeStruct((m, n), out_dtype),
        grid=(n // tn, m // tm),
        in_specs=[
            pl.BlockSpec((tm, k), lambda j, i: (i, 0)),
            pl.BlockSpec((ss.shape[0], tm, LANES), lambda j, i: (0, i, 0)),
            pl.BlockSpec((None, k, tn), lambda j, i: (layer, 0, j)),
        ],
        out_specs=pl.BlockSpec((tm, tn), lambda j, i: (i, j)),
        scratch_shapes=[] if w.dtype == BF16 else [pltpu.VMEM((k, tn), BF16)],
        compiler_params=_params("arbitrary", "arbitrary"),
        name=name,
    )(hb, ss, w)


def _conv_silu(ref, w, r0, rows, bias=None):
    cur = ref[pl.ds(r0, rows), :].astype(F32)
    halo = ref[pl.ds(pl.multiple_of(jnp.maximum(r0 - HALO, 0), HALO), HALO), :].astype(F32)
    x = jnp.concatenate([halo, cur], axis=0)
    y = cur * w[CONV_W - 1:CONV_W, :]
    for j in range(CONV_W - 1):
        y = y + pltpu.roll(x, CONV_W - 1 - j, axis=0)[HALO:, :] * w[j:j + 1, :]
    if bias is not None:
        y = y + bias
    return _silu(y)


def _tri(n, strict=False):
    r = lax.broadcasted_iota(jnp.int32, (n, n), 0)
    c = lax.broadcasted_iota(jnp.int32, (n, n), 1)
    return (r > c) if strict else (r >= c)


def _split_bf16(x):
    hi = x.astype(BF16)
    return hi, (x - hi.astype(F32)).astype(BF16)


def _pair_blockdiag(x, first):
    z = jnp.zeros_like(x)
    return jnp.concatenate([jnp.where(first, x, z), jnp.where(first, z, x)], axis=0)


def _pair_matmul3(a_parts, b_parts, first):
    a_hi, a_lo = a_parts
    bh = _pair_blockdiag(b_parts[0], first)
    bl = _pair_blockdiag(b_parts[1], first)
    lhs = jnp.concatenate([a_hi, a_lo], axis=1)
    rhs = jnp.concatenate([jnp.concatenate([bh, bl], axis=1),
                           jnp.concatenate([bh, jnp.zeros_like(bl)], axis=1)], axis=0)
    r = jnp.dot(lhs, rhs, preferred_element_type=F32)
    w = a_hi.shape[1]
    return r[:, :w] + r[:, w:]


def _pair_unit_lower_inverses(mats, n, first, between):
    row = lax.broadcasted_iota(jnp.int32, (n, 2 * n), 0)
    col = lax.broadcasted_iota(jnp.int32, (n, 2 * n), 1) & (n - 1)
    eye = (row == col).astype(F32)
    ps = [-a for a in mats]
    ts = [eye + p for p in ps]
    splits = [_split_bf16(p) for p in ps]
    ps = [_pair_matmul3(s, s, first) for s in splits]
    between()
    k = 2
    while k < n // 2:
        splits = [_split_bf16(p) for p in ps]
        stacked = []
        for t, s in zip(ts, splits):
            t_hi, t_lo = _split_bf16(t)
            lhs = (jnp.concatenate([t_hi, s[0]], axis=0), jnp.concatenate([t_lo, s[1]], axis=0))
            stacked.append(_pair_matmul3(lhs, s, first))
        ts = [t + r[:n] for t, r in zip(ts, stacked)]
        ps = [r[n:] for r in stacked]
        between()
        k *= 2
    return [t + _pair_matmul3(_split_bf16(t), _split_bf16(p), first) for t, p in zip(ts, ps)]


def _gdn_kernel(q_ref, k_ref, v_ref, z_ref, wq_ref, wk_ref, wv_ref, gcol_ref, grow_ref, pcol_ref, prow_ref,
                nw_ref, o_ref, s_ref, wq_s, u_s, oi_s, kd_s, eg_s, *, n_chunks, first_chunk, unroll):
    c_len = GDN_CHUNK
    d = HEAD_DIM
    nq = GDN_QK_PER_STEP
    nv = 2 * nq
    wq, wk, wv = wq_ref[...], wk_ref[...], wv_ref[...]
    neg_a_col = -jnp.exp(pcol_ref[:, 0:nv])
    dtb_col = pcol_ref[:, nv:2 * nv]
    neg_a_pair = -jnp.exp(prow_ref[0:nq, :])
    dtb_pair = prow_ref[nq:2 * nq, :]
    pair_row = lax.broadcasted_iota(jnp.int32, (c_len, 2 * c_len), 0)
    pair_lane = lax.broadcasted_iota(jnp.int32, (c_len, 2 * c_len), 1)
    pair_col = pair_lane & (c_len - 1)
    first = pair_lane < c_len
    lower_pair = pair_row >= pair_col
    strict_pair = pair_row > pair_col
    lower_f = _tri(c_len).astype(F32)
    r2 = lax.broadcasted_iota(jnp.int32, (2 * c_len, 2 * c_len), 0)
    c2 = lax.broadcasted_iota(jnp.int32, (2 * c_len, 2 * c_len), 1)
    upper_pair = (((r2 < c_len) == (c2 < c_len)) & ((r2 & (c_len - 1)) <= (c2 & (c_len - 1)))).astype(F32)
    norm_w = nw_ref[...]

    def load_chunk(c):
        r0 = pl.multiple_of(c * c_len, c_len)
        valid_col = (r0 + lax.broadcasted_iota(jnp.int32, (c_len, 1), 0)) >= PAD
        valid_pair = (r0 + pair_col[0:1, :]) >= PAD
        gc = gcol_ref[c]
        g_col = jnp.where(valid_col, neg_a_col * _softplus(gc[:, nv:2 * nv] + dtb_col), 0.0)
        g_pair = jnp.where(valid_pair, neg_a_pair * _softplus(grow_ref[c] + dtb_pair), 0.0)
        return dict(
            c=c,
            q=jnp.where(valid_col, _conv_silu(q_ref, wq, r0, c_len), 0.0),
            k=jnp.where(valid_col, _conv_silu(k_ref, wk, r0, c_len), 0.0),
            v=jnp.where(valid_col, _conv_silu(v_ref, wv, r0, c_len), 0.0),
            beta=jnp.where(valid_col, _sigmoid(gc[:, 0:nv]), 0.0),
            cum_col=_hdot(lower_f, g_col),
            cum_pair=_hdot(g_pair, upper_pair),
        )

    def prepare_group(c_first, between):
        chunks = [load_chunk(c_first + j) for j in range(unroll)]
        probs = [dict(ch=ch, a=a) for ch in chunks for a in range(nq)]
        for pr in probs:
            ch, a = pr["ch"], pr["a"]
            q = ch["q"][:, a * d:(a + 1) * d]
            k = ch["k"][:, a * d:(a + 1) * d]
            pr["qn"] = q * lax.rsqrt(jnp.sum(q * q, axis=-1, keepdims=True) + EPS) * (d ** -0.5)
            pr["kn"] = k * lax.rsqrt(jnp.sum(k * k, axis=-1, keepdims=True) + EPS)
        for pr in probs:
            kn_b = pr["kn"].astype(BF16)
            both = lax.dot_general(jnp.concatenate([kn_b, pr["qn"].astype(BF16)], axis=0),
                                   jnp.concatenate([kn_b, kn_b], axis=0), NT, preferred_element_type=F32)
            pr["kk"], pr["qk"] = both[:c_len], both[c_len:]
        between()
        for pr in probs:
            ch, a = pr["ch"], pr["a"]
            h0, h1 = 2 * a, 2 * a + 1
            gcol = jnp.where(first, ch["cum_col"][:, h0:h0 + 1], ch["cum_col"][:, h1:h1 + 1])
            bcol = jnp.where(first, ch["beta"][:, h0:h0 + 1], ch["beta"][:, h1:h1 + 1])
            pr["decay"] = jnp.exp(jnp.where(lower_pair, gcol - ch["cum_pair"][a:a + 1, :], -jnp.inf))
            pr["amat"] = jnp.where(strict_pair, pr["kk"] * bcol * pr["decay"], 0.0)
        ts = _pair_unit_lower_inverses([pr["amat"] for pr in probs], c_len, first, between)
        for pr, t in zip(probs, ts):
            ch, a = pr["ch"], pr["a"]
            rhs = []
            for vh in (2 * a, 2 * a + 1):
                b1 = ch["beta"][:, vh:vh + 1]
                rhs.append(jnp.concatenate([ch["v"][:, vh * d:(vh + 1) * d] * b1,
                                            pr["kn"] * (b1 * jnp.exp(ch["cum_col"][:, vh:vh + 1]))], axis=1))
            zero = jnp.zeros_like(rhs[0])
            pr["uw"] = _bdot(t, jnp.concatenate([jnp.concatenate([rhs[0], zero], axis=1),
                                                 jnp.concatenate([zero, rhs[1]], axis=1)], axis=0))
        for pr in probs:
            uw = pr["uw"]
            zero = jnp.zeros_like(uw[:, :2 * d])
            pr["ow"] = _bdot(pr["qk"] * pr["decay"], jnp.concatenate(
                [jnp.concatenate([uw[:, :2 * d], zero], axis=1),
                 jnp.concatenate([zero, uw[:, 2 * d:]], axis=1)], axis=0))
        for pr in probs:
            ch, a, uw, ow = pr["ch"], pr["a"], pr["uw"], pr["ow"]
            c = ch["c"]
            for i, vh in enumerate((2 * a, 2 * a + 1)):
                gcol1 = ch["cum_col"][:, vh:vh + 1]
                g_last = ch["cum_col"][c_len - 1:c_len, vh:vh + 1]
                u, w = uw[:, 2 * i * d:(2 * i + 1) * d], uw[:, (2 * i + 1) * d:(2 * i + 2) * d]
                oi, qw = ow[:, 2 * i * d:(2 * i + 1) * d], ow[:, (2 * i + 1) * d:(2 * i + 2) * d]
                wq_s[c, vh, pl.ds(0, c_len), :] = w.astype(BF16)
                wq_s[c, vh, pl.ds(c_len, c_len), :] = (pr["qn"] * jnp.exp(gcol1) - qw).astype(BF16)
                u_s[c, vh] = u
                oi_s[c, vh] = oi
                kd_s[c, vh] = (pr["kn"] * jnp.exp(g_last - gcol1)).astype(BF16)
                eg_s[c, vh] = jnp.broadcast_to(jnp.exp(g_last), (8, d))

    def recur_group(c_first):
        heads = range(nv)
        state = dict(s=[s_ref[vh] for vh in heads])

        def products(c):
            state["sq"] = [jnp.dot(wq_s[c, vh], state["s"][vh].astype(BF16), preferred_element_type=F32)
                           for vh in heads]

        def update(c, last):
            r0 = pl.multiple_of(c * c_len, c_len)
            valid_col = (r0 + lax.broadcasted_iota(jnp.int32, (c_len, 1), 0)) >= PAD
            z_all = z_ref[pl.ds(r0, c_len), :].astype(F32)
            sq = state["sq"]
            upd = [_bdot(kd_s[c, vh], u_s[c, vh] - sq[vh][:c_len], TN) for vh in heads]
            state["s"] = [state["s"][vh] * eg_s[c, vh][0:1, :] + upd[vh] for vh in heads]
            outs = []
            for vh in heads:
                o = oi_s[c, vh] + sq[vh][c_len:]
                o = o * lax.rsqrt(jnp.mean(o * o, axis=-1, keepdims=True) + EPS) * norm_w
                outs.append(o * _silu(z_all[:, vh * d:(vh + 1) * d]))
            out = jnp.where(valid_col, jnp.concatenate(outs, axis=1), 0.0)
            o_ref[pl.ds(r0, c_len), :] = out.astype(o_ref.dtype)
            if last:
                for vh in heads:
                    s_ref[vh] = state["s"][vh]

        steps = []
        for j in range(unroll):
            steps.append(functools.partial(products, c_first + j))
            steps.append(functools.partial(update, c_first + j, j == unroll - 1))
        return steps

    def run_between(steps):
        pending = list(steps)

        def between():
            if pending:
                pending.pop(0)()
        return between, pending

    if first_chunk:
        o_ref[pl.ds(0, first_chunk * c_len), :] = jnp.zeros((first_chunk * c_len, nv * d), o_ref.dtype)
    s_ref[...] = jnp.zeros_like(s_ref)
    n_groups = (n_chunks - first_chunk) // unroll

    prepare_group(first_chunk, lambda: None)

    def group_step(g, _):
        between, pending = run_between(recur_group(first_chunk + (g - 1) * unroll))
        prepare_group(first_chunk + g * unroll, between)
        for step in pending:
            step()
        return 0

    lax.fori_loop(1, n_groups, group_step, 0)
    for step in recur_group(first_chunk + (n_groups - 1) * unroll):
        step()


def gdn_core(qkvz, ba, conv_w, a_log, dt_bias, norm_w, batch):
    m = qkvz.shape[0]
    lp = m // batch
    hk = 16
    d = HEAD_DIM
    nq = GDN_QK_PER_STEP
    nv = 2 * nq
    ng = hk // nq
    n_chunks = lp // GDN_CHUNK
    first_chunk, unroll = next((f, u) for u in range(GDN_MAX_UNROLL, 0, -1) for f in (1, 0)
                               if (n_chunks - f) % u == 0)
    qkvz = qkvz.reshape(batch, lp, qkvz.shape[1])
    b_log = ba[:, :2 * hk].reshape(batch, n_chunks, GDN_CHUNK, ng, nv)
    a_log_t = ba[:, 2 * hk:].reshape(batch, n_chunks, GDN_CHUNK, ng, nq, 2)
    gcol = jnp.transpose(jnp.concatenate([b_log, a_log_t.reshape(b_log.shape)], axis=-1), (0, 3, 1, 2, 4))
    grow = jnp.transpose(a_log_t, (0, 3, 1, 4, 5, 2)).reshape(batch, ng, n_chunks, nq, 2 * GDN_CHUNK)
    p = jnp.concatenate([a_log.reshape(ng, nv), dt_bias.reshape(ng, nv)], axis=-1).astype(F32)
    pcol = p.reshape(ng, 1, 2 * nv)
    prow = jnp.concatenate([jnp.repeat(a_log.reshape(ng, nq, 2), GDN_CHUNK, axis=-1),
                            jnp.repeat(dt_bias.reshape(ng, nq, 2), GDN_CHUNK, axis=-1)], axis=1).astype(F32)
    qb, vb = nq * d, nv * d
    kern = functools.partial(_gdn_kernel, n_chunks=n_chunks, first_chunk=first_chunk, unroll=unroll)
    out = pl.pallas_call(
        kern,
        out_shape=jax.ShapeDtypeStruct((batch, lp, 2 * hk * d), BF16),
        grid=(batch, ng),
        in_specs=[
            pl.BlockSpec((None, lp, qb), lambda b, h: (b, 0, h)),
            pl.BlockSpec((None, lp, qb), lambda b, h: (b, 0, ng + h)),
            pl.BlockSpec((None, lp, vb), lambda b, h: (b, 0, ng + h)),
            pl.BlockSpec((None, lp, vb), lambda b, h: (b, 0, 2 * ng + h)),
            pl.BlockSpec((CONV_W, qb), lambda b, h: (0, h)),
            pl.BlockSpec((CONV_W, qb), lambda b, h: (0, ng + h)),
            pl.BlockSpec((CONV_W, vb), lambda b, h: (0, ng + h)),
            pl.BlockSpec((None, None, n_chunks, GDN_CHUNK, 2 * nv), lambda b, h: (b, h, 0, 0, 0)),
            pl.BlockSpec((None, None, n_chunks, nq, 2 * GDN_CHUNK), lambda b, h: (b, h, 0, 0, 0)),
            pl.BlockSpec((None, 1, 2 * nv), lambda b, h: (h, 0, 0)),
            pl.BlockSpec((None, 2 * nq, 2 * GDN_CHUNK), lambda b, h: (h, 0, 0)),
            pl.BlockSpec((1, d), lambda b, h: (0, 0)),
        ],
        out_specs=pl.BlockSpec((None, lp, vb), lambda b, h: (b, 0, h)),
        scratch_shapes=[
            pltpu.VMEM((nv, d, d), F32),
            pltpu.VMEM((n_chunks, nv, 2 * GDN_CHUNK, d), BF16),
            pltpu.VMEM((n_chunks, nv, GDN_CHUNK, d), F32),
            pltpu.VMEM((n_chunks, nv, GDN_CHUNK, d), F32),
            pltpu.VMEM((n_chunks, nv, GDN_CHUNK, d), BF16),
            pltpu.VMEM((n_chunks, nv, 8, d), F32),
        ],
        compiler_params=_params("parallel", "parallel"),
        name="gdn_core",
    )(qkvz, qkvz, qkvz, qkvz, conv_w, conv_w, conv_w, gcol, grow, pcol, prow, norm_w.reshape(1, d).astype(F32))
    return out.reshape(m, 2 * hk * d)


def _ssd_kernel(z_ref, x_ref, b_ref, c_ref, wx_ref, wb_ref, wc_ref, bx_ref, bb_ref, bc_ref, dcol_ref, drow_ref,
                pcol_ref, prow_ref, nw_ref, o_ref, s_ref, *, n_chunks):
    c_len = SSM_CHUNK
    hp = SSM_HEAD_DIM
    nh = SSM_HPG
    width = nh * hp
    s_ref[...] = jnp.zeros_like(s_ref)

    wx, wb, wc = wx_ref[...], wb_ref[...], wc_ref[...]
    bx, bb, bc = bx_ref[...], bb_ref[...], bc_ref[...]
    neg_a_col = -jnp.exp(pcol_ref[0:1, :])
    dtb_col = pcol_ref[1:2, :]
    d_skip = pcol_ref[2:3, :]
    neg_a_row = -jnp.exp(prow_ref[:, 0:1])
    dtb_row = prow_ref[:, 1:2]
    lower = _tri(c_len)
    lower_f = lower.astype(F32)
    upper_f = (lax.broadcasted_iota(jnp.int32, (c_len, c_len), 0)
               <= lax.broadcasted_iota(jnp.int32, (c_len, c_len), 1)).astype(F32)
    lane = lax.broadcasted_iota(jnp.int32, (1, 2 * hp), 1)

    def spread(cols):
        return jnp.concatenate([jnp.where(lane < hp, cols[:, 2 * t:2 * t + 1], cols[:, 2 * t + 1:2 * t + 2])
                                for t in range(nh // 2)], axis=1)

    d_skip_x = spread(d_skip)
    norm_w = nw_ref[...]

    def load_chunk(c):
        r0 = pl.multiple_of(c * c_len, c_len)
        valid_col = (r0 + lax.broadcasted_iota(jnp.int32, (c_len, 1), 0)) >= PAD
        valid_row = (r0 + lax.broadcasted_iota(jnp.int32, (1, c_len), 1)) >= PAD
        dt_col = jnp.where(valid_col, _softplus(dcol_ref[c] + dtb_col), 0.0)
        dt_row = jnp.where(valid_row, _softplus(drow_ref[c] + dtb_row), 0.0)
        xs = jnp.where(valid_col, _conv_silu(x_ref, wx, r0, c_len, bx), 0.0)
        return dict(
            r0=r0, valid_col=valid_col, xs=xs, xdt=xs * spread(dt_col),
            bm=jnp.where(valid_col, _conv_silu(b_ref, wb, r0, c_len, bb), 0.0),
            cm=jnp.where(valid_col, _conv_silu(c_ref, wc, r0, c_len, bc), 0.0),
            cum_col=_hdot(lower_f, dt_col * neg_a_col),
            cum_row=_hdot(dt_row * neg_a_row, upper_f),
        )

    def run_chunks(cs):
        data = [load_chunk(c) for c in cs]
        for ch in data:
            ch["cum_last"] = ch["cum_col"][c_len - 1:c_len, :]
            ch["cb"] = _bdot(ch["cm"], ch["bm"], NT)
        for ch in data:
            ch["upd"] = _bdot(ch["bm"], ch["xdt"] * spread(jnp.exp(ch["cum_last"] - ch["cum_col"])), TN)
        s = s_ref[...]
        for ch in data:
            ch["y"] = _bdot(ch["cm"], s) * spread(jnp.exp(ch["cum_col"]))
            s = s * spread(jnp.exp(ch["cum_last"])) + ch["upd"]
        s_ref[...] = s
        for ch in data:
            diag = []
            for pair in range(nh // 2):
                sc = []
                for j in (2 * pair, 2 * pair + 1):
                    lmat = jnp.exp(jnp.where(lower, ch["cum_col"][:, j:j + 1] - ch["cum_row"][j:j + 1, :], -jnp.inf))
                    sc.append((ch["cb"] * lmat).astype(BF16))
                xp = ch["xdt"][:, pair * 2 * hp:(pair + 1) * 2 * hp]
                rhs = jnp.concatenate([jnp.where(lane < hp, xp, 0.0), jnp.where(lane >= hp, xp, 0.0)], axis=0)
                diag.append(_bdot(jnp.concatenate(sc, axis=1), rhs))
            ch["diag"] = jnp.concatenate(diag, axis=1)
        for ch in data:
            y = ch["y"] + ch["diag"] + ch["xs"] * d_skip_x
            y = y * _silu(z_ref[pl.ds(ch["r0"], c_len), :].astype(F32))
            y = y * lax.rsqrt(jnp.mean(y * y, axis=-1, keepdims=True) + EPS) * norm_w
            o_ref[pl.ds(ch["r0"], c_len), :] = jnp.where(ch["valid_col"], y, 0.0).astype(o_ref.dtype)

    first = n_chunks % SSM_UNROLL
    for c in range(first):
        run_chunks([c])

    def step(it, _):
        run_chunks([first + it * SSM_UNROLL + j for j in range(SSM_UNROLL)])
        return 0

    lax.fori_loop(0, (n_chunks - first) // SSM_UNROLL, step, 0)


def ssd_core(zx, dt_raw, conv_w, conv_b, a_log, dt_bias, d_skip, norm_w, batch):
    m = zx.shape[0]
    lp = m // batch
    ng = 8
    width = SSM_HPG * SSM_HEAD_DIM
    d_inner = ng * width
    n_chunks = lp // SSM_CHUNK
    zx = zx.reshape(batch, lp, zx.shape[1])
    d5 = dt_raw.reshape(batch, n_chunks, SSM_CHUNK, ng, SSM_HPG)
    dcol = jnp.transpose(d5, (0, 3, 1, 2, 4))
    drow = jnp.transpose(d5, (0, 3, 1, 4, 2))
    p3 = jnp.stack([a_log.reshape(ng, SSM_HPG), dt_bias.reshape(ng, SSM_HPG), d_skip.reshape(ng, SSM_HPG)],
                   axis=1).astype(F32)
    prow = jnp.transpose(p3, (0, 2, 1))
    conv_b = conv_b.reshape(1, -1).astype(F32)
    xo, bo, co = d_inner // width, d_inner // SSM_STATE, (d_inner + ng * SSM_STATE) // SSM_STATE
    kern = functools.partial(_ssd_kernel, n_chunks=n_chunks)
    out = pl.pallas_call(
        kern,
        out_shape=jax.ShapeDtypeStruct((batch, lp, d_inner), BF16),
        grid=(batch, ng),
        in_specs=[
            pl.BlockSpec((None, lp, width), lambda b, g: (b, 0, g)),
            pl.BlockSpec((None, lp, width), lambda b, g: (b, 0, xo + g)),
            pl.BlockSpec((None, lp, SSM_STATE), lambda b, g: (b, 0, 2 * bo + g)),
            pl.BlockSpec((None, lp, SSM_STATE), lambda b, g: (b, 0, bo + co + g)),
            pl.BlockSpec((CONV_W, width), lambda b, g: (0, g)),
            pl.BlockSpec((CONV_W, SSM_STATE), lambda b, g: (0, bo + g)),
            pl.BlockSpec((CONV_W, SSM_STATE), lambda b, g: (0, co + g)),
            pl.BlockSpec((1, width), lambda b, g: (0, g)),
            pl.BlockSpec((1, SSM_STATE), lambda b, g: (0, bo + g)),
            pl.BlockSpec((1, SSM_STATE), lambda b, g: (0, co + g)),
            pl.BlockSpec((None, None, n_chunks, SSM_CHUNK, SSM_HPG), lambda b, g: (b, g, 0, 0, 0)),
            pl.BlockSpec((None, None, n_chunks, SSM_HPG, SSM_CHUNK), lambda b, g: (b, g, 0, 0, 0)),
            pl.BlockSpec((None, 3, SSM_HPG), lambda b, g: (g, 0, 0)),
            pl.BlockSpec((None, SSM_HPG, 3), lambda b, g: (g, 0, 0)),
            pl.BlockSpec((1, width), lambda b, g: (0, g)),
        ],
        out_specs=pl.BlockSpec((None, lp, width), lambda b, g: (b, 0, g)),
        scratch_shapes=[pltpu.VMEM((SSM_STATE, width), F32)],
        compiler_params=_params("parallel", "parallel"),
        name="ssd_core",
    )(zx, zx, zx, zx, conv_w, conv_w, conv_w, conv_b, conv_b, conv_b, dcol, drow, p3, prow,
      norm_w.reshape(1, d_inner).astype(F32))
    return out.reshape(m, d_inner)


def _rope_tables(lp):
    inv = ROPE_THETA ** (-jnp.arange(0, MLA_ROPE, 2, dtype=F32) / MLA_ROPE)
    pos = jnp.maximum(jnp.arange(lp, dtype=F32) - PAD, 0.0)
    ang = pos[:, None] * inv[None, :]
    return jnp.tile(jnp.cos(ang), (1, 4)), jnp.tile(jnp.sin(ang), (1, 4))


def _pair_head(lane):
    return lax.shift_right_logical(lane, 5) & 1


def _rope_pair(t, cos, sin):
    lane = lax.broadcasted_iota(jnp.int32, (1, t.shape[1]), 1)
    partner = pltpu.roll(t, t.shape[1] // 2, axis=1)
    return t * cos + jnp.where(lane < t.shape[1] // 2, -partner, partner) * sin


def _qprep_kernel(q_ref, gn_ref, gp_ref, cos_ref, sin_ref, o_ref, *, n_heads, scale):
    d = HEAD_DIM
    cos, sin = cos_ref[...], sin_ref[...]
    for h in range(n_heads):
        x = q_ref[:, h * d:(h + 1) * d]
        y = x * lax.rsqrt(jnp.mean(x * x, axis=-1, keepdims=True) + EPS) * gn_ref[:, h * d:(h + 1) * d]
        o_ref[:, h * d:(h + 1) * d] = (y * scale).astype(o_ref.dtype)
    r = lax.broadcasted_iota(jnp.int32, (d, d), 0)
    c = lax.broadcasted_iota(jnp.int32, (d, d), 1)
    same_head = (_pair_head(r) == _pair_head(c)).astype(F32)
    base = n_heads * d
    for p in range(n_heads // 2):
        x = q_ref[:, base + p * d:base + (p + 1) * d]
        ms = _hdot(x * x, same_head) * (1.0 / MLA_ROPE)
        y = x * lax.rsqrt(ms + EPS) * gp_ref[:, p * d:(p + 1) * d]
        o_ref[:, base + p * d:base + (p + 1) * d] = (_rope_pair(y, cos, sin) * scale).astype(o_ref.dtype)


def _kprep_kernel(kv_ref, kpe_ref, gn_ref, gp_ref, cos_ref, sin_ref, k_ref, v_ref, pe_ref, *, n_heads):
    d = HEAD_DIM
    for h in range(n_heads):
        x = kv_ref[:, h * d:(h + 1) * d]
        y = x * lax.rsqrt(jnp.mean(x * x, axis=-1, keepdims=True) + EPS) * gn_ref[:, h * d:(h + 1) * d]
        k_ref[:, h * d:(h + 1) * d] = y.astype(k_ref.dtype)
    v_ref[...] = kv_ref[:, n_heads * d:].astype(v_ref.dtype)
    x = kpe_ref[...]
    y = x * lax.rsqrt(jnp.mean(x * x, axis=-1, keepdims=True) + EPS) * gp_ref[...]
    pe_ref[...] = _rope_pair(y, cos_ref[...], sin_ref[...]).astype(pe_ref.dtype)


def _flash_kernel(qn_ref, qpe_ref, kn_ref, kpe_ref, v_ref, o_ref, *, kv_blk):
    d = HEAD_DIM
    blk = qn_ref.shape[0]
    lp = kn_ref.shape[0]
    i = pl.program_id(2)
    lane = lax.broadcasted_iota(jnp.int32, (1, d), 1)
    qpe = qpe_ref[...]
    qf = []
    for hh in range(2):
        mine = _pair_head(lane) == hh
        qf.append(jnp.concatenate([qn_ref[:, hh * d:(hh + 1) * d], jnp.where(mine, qpe, jnp.zeros_like(qpe))],
                                  axis=1))
    qpos = i * blk + lax.broadcasted_iota(jnp.int32, (blk, 1), 0)
    q_limit = jnp.maximum(qpos, PAD)

    def body(j, carry, masked):
        c0 = pl.multiple_of(jnp.minimum(PAD + j * kv_blk, lp - kv_blk), HALO)
        kpe = kpe_ref[pl.ds(c0, kv_blk), :]
        scores = []
        for hh in range(2):
            kf = jnp.concatenate([kn_ref[pl.ds(c0, kv_blk), hh * d:(hh + 1) * d], kpe], axis=1)
            scores.append(lax.dot_general(qf[hh], kf, NT, preferred_element_type=F32))
        if masked:
            kpos = c0 + lax.broadcasted_iota(jnp.int32, (1, kv_blk), 1)
            ok = (kpos >= PAD + j * kv_blk) & (kpos <= q_limit)
            scores = [jnp.where(ok, s, -1e30) for s in scores]
        stats = []
        for hh in range(2):
            m, l, _ = carry[hh]
            s = scores[hh]
            m_new = jnp.maximum(m, jnp.max(s, axis=-1, keepdims=True))
            p = jnp.exp(s - m_new)
            alpha = jnp.exp(m - m_new)
            stats.append((m_new, alpha * l + jnp.sum(p, axis=-1, keepdims=True), alpha, p.astype(BF16)))
        new = []
        for hh in range(2):
            m_new, l, alpha, p = stats[hh]
            pv = jnp.dot(p, v_ref[pl.ds(c0, kv_blk), hh * d:(hh + 1) * d], preferred_element_type=F32)
            new.append((m_new, l, alpha * carry[hh][2] + pv))
        return tuple(new)

    init = tuple((jnp.full((blk, 1), -1e30, F32), jnp.zeros((blk, 1), F32), jnp.zeros((blk, d), F32))
                 for _ in range(2))
    n_free = lax.div(jnp.maximum(i * blk + 1 - PAD, 0), kv_blk)
    n_kv = lax.div(i * blk + blk - PAD + kv_blk - 1, kv_blk)
    res = lax.fori_loop(0, n_free, functools.partial(body, masked=False), init)
    res = lax.fori_loop(n_free, n_kv, functools.partial(body, masked=True), res)
    out = jnp.concatenate([acc / l for (_, l, acc) in res], axis=1)
    o_ref[...] = jnp.where(qpos >= PAD, out, 0.0).astype(o_ref.dtype)


def mla_attention(q_raw, kv_raw, kpe_src, kpe_col_block, q_norm, k_norm, batch):
    m = q_raw.shape[0]
    lp = m // batch
    nh = 16
    d = HEAD_DIM
    half = MLA_ROPE // 2
    scale = (d + MLA_ROPE) ** -0.5
    cos, sin = _rope_tables(lp)
    qn_gain = jnp.tile(q_norm[:d], nh).reshape(1, nh * d).astype(F32)
    kn_gain = jnp.tile(k_norm[:d], nh).reshape(1, nh * d).astype(F32)
    qf, qs = q_norm[d:d + half], q_norm[d + half:]
    kf, ks = k_norm[d:d + half], k_norm[d + half:]
    qp_gain = jnp.tile(jnp.concatenate([qf, qf, qs, qs]), nh // 2).reshape(1, nh // 2 * d).astype(F32)
    kp_gain = jnp.concatenate([kf, kf, ks, ks]).reshape(1, d).astype(F32)
    tp = ATT_PREP_ROWS
    nt = lp // tp
    qw = q_raw.shape[1]

    q_prep = pl.pallas_call(
        functools.partial(_qprep_kernel, n_heads=nh, scale=scale),
        out_shape=jax.ShapeDtypeStruct((batch, lp, qw), BF16),
        grid=(batch, nt),
        in_specs=[
            pl.BlockSpec((None, tp, qw), lambda b, t: (b, t, 0)),
            pl.BlockSpec((1, nh * d), lambda b, t: (0, 0)),
            pl.BlockSpec((1, nh // 2 * d), lambda b, t: (0, 0)),
            pl.BlockSpec((tp, d), lambda b, t: (t, 0)),
            pl.BlockSpec((tp, d), lambda b, t: (t, 0)),
        ],
        out_specs=pl.BlockSpec((None, tp, qw), lambda b, t: (b, t, 0)),
        compiler_params=_params("parallel", "parallel"),
        name="mla_q_prep",
    )(q_raw.reshape(batch, lp, qw), qn_gain, qp_gain, cos, sin)

    kvw = kv_raw.shape[1]
    kpe3 = kpe_src.reshape(batch, lp, kpe_src.shape[1])
    k_prep, v_prep, pe_prep = pl.pallas_call(
        functools.partial(_kprep_kernel, n_heads=nh),
        out_shape=(jax.ShapeDtypeStruct((batch, lp, nh * d), BF16),
                   jax.ShapeDtypeStruct((batch, lp, nh * d), BF16),
                   jax.ShapeDtypeStruct((batch, lp, d), BF16)),
        grid=(batch, nt),
        in_specs=[
            pl.BlockSpec((None, tp, kvw), lambda b, t: (b, t, 0)),
            pl.BlockSpec((None, tp, d), lambda b, t: (b, t, kpe_col_block)),
            pl.BlockSpec((1, nh * d), lambda b, t: (0, 0)),
            pl.BlockSpec((1, d), lambda b, t: (0, 0)),
            pl.BlockSpec((tp, d), lambda b, t: (t, 0)),
            pl.BlockSpec((tp, d), lambda b, t: (t, 0)),
        ],
        out_specs=(pl.BlockSpec((None, tp, nh * d), lambda b, t: (b, t, 0)),
                   pl.BlockSpec((None, tp, nh * d), lambda b, t: (b, t, 0)),
                   pl.BlockSpec((None, tp, d), lambda b, t: (b, t, 0))),
        compiler_params=_params("parallel", "parallel"),
        name="mla_kv_prep",
    )(kv_raw.reshape(batch, lp, kvw), kpe3, kn_gain, kp_gain, cos, sin)

    np_ = nh // 2
    tq = next(t for t in range(min(lp, ATT_MAX_Q_ROWS) // 16 * 16, 0, -16) if lp % t == 0)
    out = pl.pallas_call(
        functools.partial(_flash_kernel, kv_blk=min(ATT_KV_BLK, lp)),
        out_shape=jax.ShapeDtypeStruct((batch, lp, nh * d), BF16),
        grid=(batch, np_, lp // tq),
        in_specs=[
            pl.BlockSpec((None, tq, 2 * d), lambda b, p, i: (b, i, p)),
            pl.BlockSpec((None, tq, d), lambda b, p, i: (b, i, nh + p)),
            pl.BlockSpec((None, lp, 2 * d), lambda b, p, i: (b, 0, p)),
            pl.BlockSpec((None, lp, d), lambda b, p, i: (b, 0, 0)),
            pl.BlockSpec((None, lp, 2 * d), lambda b, p, i: (b, 0, p)),
        ],
        out_specs=pl.BlockSpec((None, tq, 2 * d), lambda b, p, i: (b, i, p)),
        compiler_params=_params("parallel", "parallel", "arbitrary"),
        name="mla_flash",
    )(q_prep, q_prep, k_prep, pe_prep, v_prep)
    return out.reshape(m, nh * d)


def _gdn_layer(stream, next_gain, w_in_all, layer, conv_w, a_log, dt_bias, norm_w, w_out_all, batch):
    h, hb, ss = stream
    n_gate = 2 * a_log.shape[0]
    n_big = w_in_all.shape[2] - n_gate
    qkvz = scaled_matmul(hb, ss, w_in_all, layer=layer, n=n_big, name="gdn_in_proj")
    ba = scaled_matmul(hb, ss, w_in_all[layer, :, n_big:], out_dtype=F32, name="gdn_gate_proj")
    o = gdn_core(qkvz, ba, conv_w.astype(F32), a_log, dt_bias, norm_w, batch)
    return matmul_residual(o, w_out_all, h, next_gain, layer=layer, name="gdn_out_proj")


def _mla_layer(stream, next_gain, w_in, norm_q_lat, norm_kv_lat, w_uq, w_ukv, q_norm, k_norm, w_out_all, layer, batch):
    h, hb, ss = stream
    nh, d, r = 16, HEAD_DIM, MLA_ROPE
    q_lora, kv_lora = norm_q_lat.shape[0], norm_kv_lat.shape[0]
    half = r // 2
    w_cq, w_ckv, w_pe = w_in[:, :q_lora], w_in[:, q_lora:q_lora + kv_lora], w_in[:, q_lora + kv_lora:]
    w_pe_t = jnp.concatenate([w_pe[:, :half], w_pe[:, :half], w_pe[:, half:], w_pe[:, half:]], axis=1)
    gap = (-(kv_lora + d)) % q_lora
    w_lat = jnp.concatenate([w_ckv, w_pe_t, jnp.zeros((w_in.shape[0], gap), w_in.dtype), w_cq], axis=1)
    lat = scaled_matmul(hb, ss, w_lat, out_dtype=F32, name="mla_in_proj")
    wq = w_uq.reshape(q_lora, nh, d + r)
    wq_pe = wq[:, :, d:].reshape(q_lora, nh // 2, 2, 2, half)
    wq_pe = jnp.transpose(wq_pe, (0, 1, 3, 2, 4)).reshape(q_lora, nh // 2 * d)
    wq_all = jnp.concatenate([wq[:, :, :d].reshape(q_lora, nh * d), wq_pe], axis=1)
    q_raw = norm_matmul(lat, norm_q_lat, wq_all.astype(BF16), x_col_block=(kv_lora + d + gap) // q_lora,
                        out_dtype=F32, name="mla_q_up")
    wkv = w_ukv.reshape(kv_lora, nh, 2 * d)
    wkv_all = jnp.concatenate([wkv[:, :, :d].reshape(kv_lora, nh * d), wkv[:, :, d:].reshape(kv_lora, nh * d)],
                              axis=1)
    kv_raw = norm_matmul(lat, norm_kv_lat, wkv_all.astype(BF16), x_col_block=0, out_dtype=F32, name="mla_kv_up")
    o = mla_attention(q_raw, kv_raw, lat, kv_lora // d, q_norm, k_norm, batch)
    return matmul_residual(o, w_out_all, h, next_gain, layer=layer, name="mla_out_proj")


def _ssm_layer(stream, next_gain, w_in_all, layer, conv_w, conv_b, a_log, dt_bias, d_skip, norm_w, w_out_all, batch):
    h, hb, ss = stream
    n_dt = a_log.shape[0]
    n_big = w_in_all.shape[2] - n_dt
    zx = scaled_matmul(hb, ss, w_in_all, layer=layer, n=n_big, name="ssm_in_proj")
    dt_raw = scaled_matmul(hb, ss, w_in_all[layer, :, n_big:], out_dtype=F32, name="ssm_dt_proj")
    y = ssd_core(zx, dt_raw, conv_w.astype(F32), conv_b, a_log, dt_bias, d_skip, norm_w, batch)
    return matmul_residual(y, w_out_all, h, next_gain, layer=layer, name="ssm_out_proj")


def _mlp_layer(stream, next_gain, w_up_all, w_down_all, layer):
    h, hb, ss = stream
    act = scaled_matmul(hb, ss, w_up_all, layer=layer, act="relu2", name="mlp_up")
    return matmul_residual(act, w_down_all, h, next_gain, layer=layer, name="mlp_down")


def kernel(x, meta_tokens, norm_mix, norm_mlp, mlp_w_up, mlp_w_down, gdn_w_in, gdn_conv_w, gdn_a_log, gdn_dt_bias, gdn_norm, gdn_w_out, mla_w_in, mla_norm_q_lat, mla_norm_kv_lat, mla_w_uq, mla_w_ukv, mla_q_norm, mla_k_norm, mla_w_out, ssm_w_in, ssm_conv_w, ssm_conv_b, ssm_a_log, ssm_dt_bias, ssm_d, ssm_norm, ssm_w_out):
    batch, seq, dm = x.shape
    depth = norm_mix.shape[0]
    lp = ROW_TILE + seq
    meta = jnp.broadcast_to(meta_tokens.astype(x.dtype)[None], (batch, N_META, dm))
    h = jnp.concatenate([jnp.zeros((batch, PAD, dm), x.dtype), meta, x], axis=1).reshape(batch * lp, dm)
    stream = (h,) + tuple(stream_entry(h, norm_mix[0]))
    down_b, gdn_out_b, ssm_out_b, mla_out_b = (w.astype(BF16) for w in (mlp_w_down, gdn_w_out, ssm_w_out, mla_w_out))
    gdn_in_b, ssm_in_b = gdn_w_in.astype(BF16), ssm_w_in.astype(BF16)
    ia = ib = ic = 0
    for i in range(depth):
        kind = i % 3
        if kind == 0:
            stream = _gdn_layer(stream, norm_mlp[i], gdn_in_b, ia, gdn_conv_w[ia], gdn_a_log[ia], gdn_dt_bias[ia],
                                gdn_norm[ia], gdn_out_b, batch)
            ia += 1
        elif kind == 1:
            stream = _mla_layer(stream, norm_mlp[i], mla_w_in[ib], mla_norm_q_lat[ib], mla_norm_kv_lat[ib],
                                mla_w_uq[ib], mla_w_ukv[ib], mla_q_norm[ib], mla_k_norm[ib], mla_out_b, ib, batch)
            ib += 1
        else:
            stream = _ssm_layer(stream, norm_mlp[i], ssm_in_b, ic, ssm_conv_w[ic], ssm_conv_b[ic], ssm_a_log[ic],
                                ssm_dt_bias[ic], ssm_d[ic], ssm_norm[ic], ssm_out_b, batch)
            ic += 1
        following = norm_mix[i + 1] if i + 1 < depth else jnp.ones_like(norm_mlp[i])
        stream = _mlp_layer(stream, following, mlp_w_up, down_b, i)
    return stream[0].reshape(batch, lp, dm)[:, ROW_TILE:]
```

```python
import functools
import math

import jax
import jax.numpy as jnp
from jax import lax
from jax.experimental import pallas as pl
from jax.experimental.pallas import tpu as pltpu

F32 = jnp.float32
BF16 = jnp.bfloat16
HI = lax.Precision.HIGHEST

EPS = 1e-6
N_META = 16
ROW_TILE = 128
PAD = ROW_TILE - N_META
LANES = 128
HALO = 16
CONV_W = 4

HEAD_DIM = 128
GDN_CHUNK = 64
SSM_CHUNK = 128
SSM_HEAD_DIM = 64
SSM_HPG = 8
SSM_STATE = 128
MLA_ROPE = 64
ROPE_THETA = 10000.0
ATT_PREP_ROWS = 128
ATT_MAX_Q_ROWS = 320
ATT_KV_BLK = 512
GDN_QK_PER_STEP = 2
SSM_UNROLL = 2
GDN_MAX_UNROLL = 3

VMEM_LIMIT = 56 * 1024 * 1024
MAX_ROW_TILE = 1088
MAX_K_TILE = 2048
NT = (((1,), (1,)), ((), ()))
TN = (((0,), (0,)), ((), ()))


def _row_tile(m, cap=MAX_ROW_TILE):
    for t in range(min(m, cap) // 64 * 64, 0, -64):
        if m % t == 0:
            return t
    raise ValueError(f"row count {m} has no tile that is a multiple of 64")


def _col_tile(n, cap=1024):
    if n <= cap:
        return n
    for t in range(cap, 0, -128):
        if n % t == 0:
            return t
    raise ValueError(f"column count {n} has no tile that is a multiple of 128")


def _params(*sem):
    return pltpu.CompilerParams(dimension_semantics=sem, vmem_limit_bytes=VMEM_LIMIT)


def _bdot(a, b, dims=None):
    a = a.astype(BF16)
    b = b.astype(BF16)
    if dims is None:
        return jnp.dot(a, b, preferred_element_type=F32)
    return lax.dot_general(a, b, dims, preferred_element_type=F32)


def _hdot(a, b):
    return jnp.dot(a, b, precision=HI, preferred_element_type=F32)


def _sigmoid(x):
    return 0.5 + 0.5 * jnp.tanh(0.5 * x)


def _silu(x):
    return x * _sigmoid(x)


def _softplus(x):
    return jnp.maximum(x, 0.0) + jnp.log(1.0 + jnp.exp(-jnp.abs(x)))


def _norm_matmul_kernel(x_ref, g_ref, w_ref, o_ref, xn_ref, *, sub):
    @pl.when(pl.program_id(1) == 0)
    def _():
        def body(r, _):
            r0 = pl.multiple_of(r * sub, sub)
            x = x_ref[pl.ds(r0, sub), :].astype(F32)
            ms = jnp.mean(x * x, axis=-1, keepdims=True)
            xn_ref[pl.ds(r0, sub), :] = (x * lax.rsqrt(ms + EPS) * g_ref[...]).astype(BF16)
            return 0

        lax.fori_loop(0, x_ref.shape[0] // sub, body, 0)

    o_ref[...] = jnp.dot(xn_ref[...], w_ref[...], preferred_element_type=F32).astype(o_ref.dtype)


def norm_matmul(x, gain, w, *, x_col_block=0, out_dtype=BF16, name):
    m = x.shape[0]
    k, n = w.shape
    tm, tn = _row_tile(m), _col_tile(n)
    return pl.pallas_call(
        functools.partial(_norm_matmul_kernel, sub=64),
        out_shape=jax.ShapeDtypeStruct((m, n), out_dtype),
        grid=(m // tm, n // tn),
        in_specs=[
            pl.BlockSpec((tm, k), lambda i, j: (i, x_col_block)),
            pl.BlockSpec((1, k), lambda i, j: (0, 0)),
            pl.BlockSpec((k, tn), lambda i, j: (0, j)),
        ],
        out_specs=pl.BlockSpec((tm, tn), lambda i, j: (i, j)),
        scratch_shapes=[pltpu.VMEM((tm, k), BF16)],
        compiler_params=_params("parallel", "arbitrary"),
        name=name,
    )(x, gain.reshape(1, k).astype(F32), w)


def _stats_rows(tm, cap=272):
    return next(t for t in range(min(tm, cap) // 16 * 16, 0, -16) if tm % t == 0)


def _stream_stats(h_ref, g_ref, hb_ref, ss_ref, sub):
    def body(r, _):
        r0 = pl.multiple_of(r * sub, sub)
        x = h_ref[pl.ds(r0, sub), :]
        hb_ref[pl.ds(r0, sub), :] = (x * g_ref[...]).astype(BF16)
        ss_ref[pl.ds(r0, sub), :] = jnp.broadcast_to(jnp.sum(x * x, axis=-1, keepdims=True), (sub, ss_ref.shape[1]))
        return 0

    lax.fori_loop(0, h_ref.shape[0] // sub, body, 0)


def _matmul_residual_kernel(a_ref, w_ref, h_ref, g_ref, o_ref, hb_ref, ss_ref, *, sub):
    @pl.when(pl.program_id(2) == 0)
    def _():
        o_ref[...] = h_ref[...]

    last = pl.program_id(2) == pl.num_programs(2) - 1

    @pl.when(jnp.logical_not(last))
    def _():
        o_ref[...] += jnp.dot(a_ref[...], w_ref[...], preferred_element_type=F32)

    @pl.when(last)
    def _():
        new = o_ref[...] + jnp.dot(a_ref[...], w_ref[...], preferred_element_type=F32)
        o_ref[...] = new
        hb_ref[...] = (new * g_ref[...]).astype(BF16)
        ss_ref[...] = jnp.broadcast_to(jnp.sum(new * new, axis=-1, keepdims=True), ss_ref.shape)


def _stream_out(m, n, tm, tn, index):
    shapes = (jax.ShapeDtypeStruct((m, n), F32), jax.ShapeDtypeStruct((m, n), BF16),
              jax.ShapeDtypeStruct((n // tn, m, LANES), F32))
    specs = (pl.BlockSpec((tm, tn), lambda *g: index(*g)),
             pl.BlockSpec((tm, tn), lambda *g: index(*g)),
             pl.BlockSpec((None, tm, LANES), lambda *g: (index(*g)[1], index(*g)[0], 0)))
    return shapes, specs


def matmul_residual(a, w, h, next_gain, *, layer=0, name):
    if w.ndim == 2:
        w = w[None]
    m, k = a.shape
    n = w.shape[2]
    tm, tn, tk = _row_tile(m), _col_tile(n), _col_tile(k, MAX_K_TILE)
    shapes, specs = _stream_out(m, n, tm, tn, lambda i, j, kk: (i, j))
    return pl.pallas_call(
        functools.partial(_matmul_residual_kernel, sub=_stats_rows(tm)),
        out_shape=shapes,
        grid=(m // tm, n // tn, k // tk),
        in_specs=[
            pl.BlockSpec((tm, tk), lambda i, j, kk: (i, kk)),
            pl.BlockSpec((None, tk, tn), lambda i, j, kk: (layer, kk, j)),
            pl.BlockSpec((tm, tn), lambda i, j, kk: (i, j)),
            pl.BlockSpec((1, tn), lambda i, j, kk: (0, j)),
        ],
        out_specs=specs,
        compiler_params=_params("parallel", "parallel", "arbitrary"),
        name=name,
    )(a, w, h, next_gain.reshape(1, n).astype(F32))


def _stream_entry_kernel(h_ref, g_ref, hb_ref, ss_ref, *, sub):
    _stream_stats(h_ref, g_ref, hb_ref, ss_ref, sub)


def stream_entry(h, next_gain):
    m, n = h.shape
    tm, tn = _row_tile(m), _col_tile(n)
    shapes, specs = _stream_out(m, n, tm, tn, lambda i, j: (i, j))
    return pl.pallas_call(
        functools.partial(_stream_entry_kernel, sub=_stats_rows(tm)),
        out_shape=shapes[1:],
        grid=(m // tm, n // tn),
        in_specs=[pl.BlockSpec((tm, tn), lambda i, j: (i, j)), pl.BlockSpec((1, tn), lambda i, j: (0, j))],
        out_specs=specs[1:],
        compiler_params=_params("parallel", "parallel"),
        name="stream_entry",
    )(h, next_gain.reshape(1, n).astype(F32))


def _scaled_matmul_kernel(x_ref, ss_ref, w_ref, o_ref, *scratch, act, inv_k, sub):
    if scratch:
        wb_ref, = scratch

        @pl.when(pl.program_id(1) == 0)
        def _():
            def body(r, _):
                r0 = pl.multiple_of(r * sub, sub)
                wb_ref[pl.ds(r0, sub), :] = w_ref[pl.ds(r0, sub), :].astype(BF16)
                return 0

            lax.fori_loop(0, w_ref.shape[0] // sub, body, 0)
    else:
        wb_ref = w_ref

    ss = ss_ref[0]
    for t in range(1, ss_ref.shape[0]):
        ss = ss + ss_ref[t]
    y = jnp.dot(x_ref[...], wb_ref[...], preferred_element_type=F32) * lax.rsqrt(ss[:, 0:1] * inv_k + EPS)
    if act == "relu2":
        y = jnp.square(jnp.maximum(y, 0.0))
    o_ref[...] = y.astype(o_ref.dtype)


def scaled_matmul(hb, ss, w, *, layer=0, n=None, act=None, out_dtype=BF16, name):
    if w.ndim == 2:
        w = w[None]
    m, k = hb.shape
    n = w.shape[2] if n is None else n
    tm, tn = _row_tile(m), _col_tile(n)
    return pl.pallas_call(
        functools.partial(_scaled_matmul_kernel, act=act, inv_k=1.0 / k, sub=256),
        out_shape=jax.ShapeDtypeStruct((m, n), out_dtype),
        grid=(n // tn, m // tm),
        in_specs=[
            pl.BlockSpec((tm, k), lambda j, i: (i, 0)),
            pl.BlockSpec((ss.shape[0], tm, LANES), lambda j, i: (0, i, 0)),
            pl.BlockSpec((None, k, tn), lambda j, i: (layer, 0, j)),
        ],
        out_specs=pl.BlockSpec((tm, tn), lambda j, i: (i, j)),
        scratch_shapes=[] if w.dtype == BF16 else [pltpu.VMEM((k, tn), BF16)],
        compiler_params=_params("arbitrary", "arbitrary"),
        name=name,
    )(hb, ss, w)


def _conv_silu(ref, w, r0, rows, bias=None):
    cur = ref[pl.ds(r0, rows), :].astype(F32)
    halo = ref[pl.ds(pl.multiple_of(jnp.maximum(r0 - HALO, 0), HALO), HALO), :].astype(F32)
    x = jnp.concatenate([halo, cur], axis=0)
    y = cur * w[CONV_W - 1:CONV_W, :]
    for j in range(CONV_W - 1):
        y = y + pltpu.roll(x, CONV_W - 1 - j, axis=0)[HALO:, :] * w[j:j + 1, :]
    if bias is not None:
        y = y + bias
    return _silu(y)


def _tri(n, strict=False):
    r = lax.broadcasted_iota(jnp.int32, (n, n), 0)
    c = lax.broadcasted_iota(jnp.int32, (n, n), 1)
    return (r > c) if strict else (r >= c)


def _split_bf16(x):
    hi = x.astype(BF16)
    return hi, (x - hi.astype(F32)).astype(BF16)


def _pair_blockdiag(x, first):
    z = jnp.zeros_like(x)
    return jnp.concatenate([jnp.where(first, x, z), jnp.where(first, z, x)], axis=0)


def _pair_matmul3(a_parts, b_parts, first):
    a_hi, a_lo = a_parts
    bh = _pair_blockdiag(b_parts[0], first)
    bl = _pair_blockdiag(b_parts[1], first)
    lhs = jnp.concatenate([a_hi, a_lo], axis=1)
    rhs = jnp.concatenate([jnp.concatenate([bh, bl], axis=1),
                           jnp.concatenate([bh, jnp.zeros_like(bl)], axis=1)], axis=0)
    r = jnp.dot(lhs, rhs, preferred_element_type=F32)
    w = a_hi.shape[1]
    return r[:, :w] + r[:, w:]


def _pair_unit_lower_inverses(mats, n, first, between):
    row = lax.broadcasted_iota(jnp.int32, (n, 2 * n), 0)
    col = lax.broadcasted_iota(jnp.int32, (n, 2 * n), 1) & (n - 1)
    eye = (row == col).astype(F32)
    ps = [-a for a in mats]
    ts = [eye + p for p in ps]
    splits = [_split_bf16(p) for p in ps]
    ps = [_pair_matmul3(s, s, first) for s in splits]
    between()
    k = 2
    while k < n // 2:
        splits = [_split_bf16(p) for p in ps]
        stacked = []
        for t, s in zip(ts, splits):
            t_hi, t_lo = _split_bf16(t)
            lhs = (jnp.concatenate([t_hi, s[0]], axis=0), jnp.concatenate([t_lo, s[1]], axis=0))
            stacked.append(_pair_matmul3(lhs, s, first))
        ts = [t + r[:n] for t, r in zip(ts, stacked)]
        ps = [r[n:] for r in stacked]
        between()
        k *= 2
    return [t + _pair_matmul3(_split_bf16(t), _split_bf16(p), first) for t, p in zip(ts, ps)]


def _gdn_kernel(q_ref, k_ref, v_ref, z_ref, wq_ref, wk_ref, wv_ref, gcol_ref, grow_ref, pcol_ref, prow_ref,
                nw_ref, o_ref, s_ref, wq_s, u_s, oi_s, kd_s, eg_s, *, n_chunks, first_chunk, unroll):
    c_len = GDN_CHUNK
    d = HEAD_DIM
    nq = GDN_QK_PER_STEP
    nv = 2 * nq
    wq, wk, wv = wq_ref[...], wk_ref[...], wv_ref[...]
    neg_a_col = -jnp.exp(pcol_ref[:, 0:nv])
    dtb_col = pcol_ref[:, nv:2 * nv]
    neg_a_pair = -jnp.exp(prow_ref[0:nq, :])
    dtb_pair = prow_ref[nq:2 * nq, :]
    pair_row = lax.broadcasted_iota(jnp.int32, (c_len, 2 * c_len), 0)
    pair_lane = lax.broadcasted_iota(jnp.int32, (c_len, 2 * c_len), 1)
    pair_col = pair_lane & (c_len - 1)
    first = pair_lane < c_len
    lower_pair = pair_row >= pair_col
    strict_pair = pair_row > pair_col
    lower_f = _tri(c_len).astype(F32)
    r2 = lax.broadcasted_iota(jnp.int32, (2 * c_len, 2 * c_len), 0)
    c2 = lax.broadcasted_iota(jnp.int32, (2 * c_len, 2 * c_len), 1)
    upper_pair = (((r2 < c_len) == (c2 < c_len)) & ((r2 & (c_len - 1)) <= (c2 & (c_len - 1)))).astype(F32)
    norm_w = nw_ref[...]

    def load_chunk(c):
        r0 = pl.multiple_of(c * c_len, c_len)
        valid_col = (r0 + lax.broadcasted_iota(jnp.int32, (c_len, 1), 0)) >= PAD
        valid_pair = (r0 + pair_col[0:1, :]) >= PAD
        gc = gcol_ref[c]
        g_col = jnp.where(valid_col, neg_a_col * _softplus(gc[:, nv:2 * nv] + dtb_col), 0.0)
        g_pair = jnp.where(valid_pair, neg_a_pair * _softplus(grow_ref[c] + dtb_pair), 0.0)
        return dict(
            c=c,
            q=jnp.where(valid_col, _conv_silu(q_ref, wq, r0, c_len), 0.0),
            k=jnp.where(valid_col, _conv_silu(k_ref, wk, r0, c_len), 0.0),
            v=jnp.where(valid_col, _conv_silu(v_ref, wv, r0, c_len), 0.0),
            beta=jnp.where(valid_col, _sigmoid(gc[:, 0:nv]), 0.0),
            cum_col=_hdot(lower_f, g_col),
            cum_pair=_hdot(g_pair, upper_pair),
        )

    def prepare_group(c_first, between):
        chunks = [load_chunk(c_first + j) for j in range(unroll)]
        probs = [dict(ch=ch, a=a) for ch in chunks for a in range(nq)]
        for pr in probs:
            ch, a = pr["ch"], pr["a"]
            q = ch["q"][:, a * d:(a + 1) * d]
            k = ch["k"][:, a * d:(a + 1) * d]
            pr["qn"] = q * lax.rsqrt(jnp.sum(q * q, axis=-1, keepdims=True) + EPS) * (d ** -0.5)
            pr["kn"] = k * lax.rsqrt(jnp.sum(k * k, axis=-1, keepdims=True) + EPS)
        for pr in probs:
            kn_b = pr["kn"].astype(BF16)
            both = lax.dot_general(jnp.concatenate([kn_b, pr["qn"].astype(BF16)], axis=0),
                                   jnp.concatenate([kn_b, kn_b], axis=0), NT, preferred_element_type=F32)
            pr["kk"], pr["qk"] = both[:c_len], both[c_len:]
        between()
        for pr in probs:
            ch, a = pr["ch"], pr["a"]
            h0, h1 = 2 * a, 2 * a + 1
            gcol = jnp.where(first, ch["cum_col"][:, h0:h0 + 1], ch["cum_col"][:, h1:h1 + 1])
            bcol = jnp.where(first, ch["beta"][:, h0:h0 + 1], ch["beta"][:, h1:h1 + 1])
            pr["decay"] = jnp.exp(jnp.where(lower_pair, gcol - ch["cum_pair"][a:a + 1, :], -jnp.inf))
            pr["amat"] = jnp.where(strict_pair, pr["kk"] * bcol * pr["decay"], 0.0)
        ts = _pair_unit_lower_inverses([pr["amat"] for pr in probs], c_len, first, between)
        for pr, t in zip(probs, ts):
            ch, a = pr["ch"], pr["a"]
            rhs = []
            for vh in (2 * a, 2 * a + 1):
                b1 = ch["beta"][:, vh:vh + 1]
                rhs.append(jnp.concatenate([ch["v"][:, vh * d:(vh + 1) * d] * b1,
                                            pr["kn"] * (b1 * jnp.exp(ch["cum_col"][:, vh:vh + 1]))], axis=1))
            zero = jnp.zeros_like(rhs[0])
            pr["uw"] = _bdot(t, jnp.concatenate([jnp.concatenate([rhs[0], zero], axis=1),
                                                 jnp.concatenate([zero, rhs[1]], axis=1)], axis=0))
        for pr in probs:
            uw = pr["uw"]
            zero = jnp.zeros_like(uw[:, :2 * d])
            pr["ow"] = _bdot(pr["qk"] * pr["decay"], jnp.concatenate(
                [jnp.concatenate([uw[:, :2 * d], zero], axis=1),
                 jnp.concatenate([zero, uw[:, 2 * d:]], axis=1)], axis=0))
        for pr in probs:
            ch, a, uw, ow = pr["ch"], pr["a"], pr["uw"], pr["ow"]
            c = ch["c"]
            for i, vh in enumerate((2 * a, 2 * a + 1)):
                gcol1 = ch["cum_col"][:, vh:vh + 1]
                g_last = ch["cum_col"][c_len - 1:c_len, vh:vh + 1]
                u, w = uw[:, 2 * i * d:(2 * i + 1) * d], uw[:, (2 * i + 1) * d:(2 * i + 2) * d]
                oi, qw = ow[:, 2 * i * d:(2 * i + 1) * d], ow[:, (2 * i + 1) * d:(2 * i + 2) * d]
                wq_s[c, vh, pl.ds(0, c_len), :] = w.astype(BF16)
                wq_s[c, vh, pl.ds(c_len, c_len), :] = (pr["qn"] * jnp.exp(gcol1) - qw).astype(BF16)
                u_s[c, vh] = u
                oi_s[c, vh] = oi
                kd_s[c, vh] = (pr["kn"] * jnp.exp(g_last - gcol1)).astype(BF16)
                eg_s[c, vh] = jnp.broadcast_to(jnp.exp(g_last), (8, d))

    def recur_group(c_first):
        heads = range(nv)
        state = dict(s=[s_ref[vh] for vh in heads])

        def products(c):
            state["sq"] = [jnp.dot(wq_s[c, vh], state["s"][vh].astype(BF16), preferred_element_type=F32)
                           for vh in heads]

        def update(c, last):
            r0 = pl.multiple_of(c * c_len, c_len)
            valid_col = (r0 + lax.broadcasted_iota(jnp.int32, (c_len, 1), 0)) >= PAD
            z_all = z_ref[pl.ds(r0, c_len), :].astype(F32)
            sq = state["sq"]
            upd = [_bdot(kd_s[c, vh], u_s[c, vh] - sq[vh][:c_len], TN) for vh in heads]
            state["s"] = [state["s"][vh] * eg_s[c, vh][0:1, :] + upd[vh] for vh in heads]
            outs = []
            for vh in heads:
                o = oi_s[c, vh] + sq[vh][c_len:]
                o = o * lax.rsqrt(jnp.mean(o * o, axis=-1, keepdims=True) + EPS) * norm_w
                outs.append(o * _silu(z_all[:, vh * d:(vh + 1) * d]))
            out = jnp.where(valid_col, jnp.concatenate(outs, axis=1), 0.0)
            o_ref[pl.ds(r0, c_len), :] = out.astype(o_ref.dtype)
            if last:
                for vh in heads:
                    s_ref[vh] = state["s"][vh]

        steps = []
        for j in range(unroll):
            steps.append(functools.partial(products, c_first + j))
            steps.append(functools.partial(update, c_first + j, j == unroll - 1))
        return steps

    def run_between(steps):
        pending = list(steps)

        def between():
            if pending:
                pending.pop(0)()
        return between, pending

    if first_chunk:
        o_ref[pl.ds(0, first_chunk * c_len), :] = jnp.zeros((first_chunk * c_len, nv * d), o_ref.dtype)
    s_ref[...] = jnp.zeros_like(s_ref)
    n_groups = (n_chunks - first_chunk) // unroll

    prepare_group(first_chunk, lambda: None)

    def group_step(g, _):
        between, pending = run_between(recur_group(first_chunk + (g - 1) * unroll))
        prepare_group(first_chunk + g * unroll, between)
        for step in pending:
            step()
        return 0

    lax.fori_loop(1, n_groups, group_step, 0)
    for step in recur_group(first_chunk + (n_groups - 1) * unroll):
        step()


def gdn_core(qkvz, ba, conv_w, a_log, dt_bias, norm_w, batch):
    m = qkvz.shape[0]
    lp = m // batch
    hk = 16
    d = HEAD_DIM
    nq = GDN_QK_PER_STEP
    nv = 2 * nq
    ng = hk // nq
    n_chunks = lp // GDN_CHUNK
    first_chunk, unroll = next((f, u) for u in range(GDN_MAX_UNROLL, 0, -1) for f in (1, 0)
                               if (n_chunks - f) % u == 0)
    qkvz = qkvz.reshape(batch, lp, qkvz.shape[1])
    b_log = ba[:, :2 * hk].reshape(batch, n_chunks, GDN_CHUNK, ng, nv)
    a_log_t = ba[:, 2 * hk:].reshape(batch, n_chunks, GDN_CHUNK, ng, nq, 2)
    gcol = jnp.transpose(jnp.concatenate([b_log, a_log_t.reshape(b_log.shape)], axis=-1), (0, 3, 1, 2, 4))
    grow = jnp.transpose(a_log_t, (0, 3, 1, 4, 5, 2)).reshape(batch, ng, n_chunks, nq, 2 * GDN_CHUNK)
    p = jnp.concatenate([a_log.reshape(ng, nv), dt_bias.reshape(ng, nv)], axis=-1).astype(F32)
    pcol = p.reshape(ng, 1, 2 * nv)
    prow = jnp.concatenate([jnp.repeat(a_log.reshape(ng, nq, 2), GDN_CHUNK, axis=-1),
                            jnp.repeat(dt_bias.reshape(ng, nq, 2), GDN_CHUNK, axis=-1)], axis=1).astype(F32)
    qb, vb = nq * d, nv * d
    kern = functools.partial(_gdn_kernel, n_chunks=n_chunks, first_chunk=first_chunk, unroll=unroll)
    out = pl.pallas_call(
        kern,
        out_shape=jax.ShapeDtypeStruct((batch, lp, 2 * hk * d), BF16),
        grid=(batch, ng),
        in_specs=[
            pl.BlockSpec((None, lp, qb), lambda b, h: (b, 0, h)),
            pl.BlockSpec((None, lp, qb), lambda b, h: (b, 0, ng + h)),
            pl.BlockSpec((None, lp, vb), lambda b, h: (b, 0, ng + h)),
            pl.BlockSpec((None, lp, vb), lambda b, h: (b, 0, 2 * ng + h)),
            pl.BlockSpec((CONV_W, qb), lambda b, h: (0, h)),
            pl.BlockSpec((CONV_W, qb), lambda b, h: (0, ng + h)),
            pl.BlockSpec((CONV_W, vb), lambda b, h: (0, ng + h)),
            pl.BlockSpec((None, None, n_chunks, GDN_CHUNK, 2 * nv), lambda b, h: (b, h, 0, 0, 0)),
            pl.BlockSpec((None, None, n_chunks, nq, 2 * GDN_CHUNK), lambda b, h: (b, h, 0, 0, 0)),
            pl.BlockSpec((None, 1, 2 * nv), lambda b, h: (h, 0, 0)),
            pl.BlockSpec((None, 2 * nq, 2 * GDN_CHUNK), lambda b, h: (h, 0, 0)),
            pl.BlockSpec((1, d), lambda b, h: (0, 0)),
        ],
        out_specs=pl.BlockSpec((None, lp, vb), lambda b, h: (b, 0, h)),
        scratch_shapes=[
            pltpu.VMEM((nv, d, d), F32),
            pltpu.VMEM((n_chunks, nv, 2 * GDN_CHUNK, d), BF16),
            pltpu.VMEM((n_chunks, nv, GDN_CHUNK, d), F32),
            pltpu.VMEM((n_chunks, nv, GDN_CHUNK, d), F32),
            pltpu.VMEM((n_chunks, nv, GDN_CHUNK, d), BF16),
            pltpu.VMEM((n_chunks, nv, 8, d), F32),
        ],
        compiler_params=_params("parallel", "parallel"),
        name="gdn_core",
    )(qkvz, qkvz, qkvz, qkvz, conv_w, conv_w, conv_w, gcol, grow, pcol, prow, norm_w.reshape(1, d).astype(F32))
    return out.reshape(m, 2 * hk * d)


def _ssd_kernel(z_ref, x_ref, b_ref, c_ref, wx_ref, wb_ref, wc_ref, bx_ref, bb_ref, bc_ref, dcol_ref, drow_ref,
                pcol_ref, prow_ref, nw_ref, o_ref, s_ref, *, n_chunks):
    c_len = SSM_CHUNK
    hp = SSM_HEAD_DIM
    nh = SSM_HPG
    width = nh * hp
    s_ref[...] = jnp.zeros_like(s_ref)

    wx, wb, wc = wx_ref[...], wb_ref[...], wc_ref[...]
    bx, bb, bc = bx_ref[...], bb_ref[...], bc_ref[...]
    neg_a_col = -jnp.exp(pcol_ref[0:1, :])
    dtb_col = pcol_ref[1:2, :]
    d_skip = pcol_ref[2:3, :]
    neg_a_row = -jnp.exp(prow_ref[:, 0:1])
    dtb_row = prow_ref[:, 1:2]
    lower = _tri(c_len)
    lower_f = lower.astype(F32)
    upper_f = (lax.broadcasted_iota(jnp.int32, (c_len, c_len), 0)
               <= lax.broadcasted_iota(jnp.int32, (c_len, c_len), 1)).astype(F32)
    lane = lax.broadcasted_iota(jnp.int32, (1, 2 * hp), 1)

    def spread(cols):
        return jnp.concatenate([jnp.where(lane < hp, cols[:, 2 * t:2 * t + 1], cols[:, 2 * t + 1:2 * t + 2])
                                for t in range(nh // 2)], axis=1)

    d_skip_x = spread(d_skip)
    norm_w = nw_ref[...]

    def load_chunk(c):
        r0 = pl.multiple_of(c * c_len, c_len)
        valid_col = (r0 + lax.broadcasted_iota(jnp.int32, (c_len, 1), 0)) >= PAD
        valid_row = (r0 + lax.broadcasted_iota(jnp.int32, (1, c_len), 1)) >= PAD
        dt_col = jnp.where(valid_col, _softplus(dcol_ref[c] + dtb_col), 0.0)
        dt_row = jnp.where(valid_row, _softplus(drow_ref[c] + dtb_row), 0.0)
        xs = jnp.where(valid_col, _conv_silu(x_ref, wx, r0, c_len, bx), 0.0)
        return dict(
            r0=r0, valid_col=valid_col, xs=xs, xdt=xs * spread(dt_col),
            bm=jnp.where(valid_col, _conv_silu(b_ref, wb, r0, c_len, bb), 0.0),
            cm=jnp.where(valid_col, _conv_silu(c_ref, wc, r0, c_len, bc), 0.0),
            cum_col=_hdot(lower_f, dt_col * neg_a_col),
            cum_row=_hdot(dt_row * neg_a_row, upper_f),
        )

    def run_chunks(cs):
        data = [load_chunk(c) for c in cs]
        for ch in data:
            ch["cum_last"] = ch["cum_col"][c_len - 1:c_len, :]
            ch["cb"] = _bdot(ch["cm"], ch["bm"], NT)
        for ch in data:
            ch["upd"] = _bdot(ch["bm"], ch["xdt"] * spread(jnp.exp(ch["cum_last"] - ch["cum_col"])), TN)
        s = s_ref[...]
        for ch in data:
            ch["y"] = _bdot(ch["cm"], s) * spread(jnp.exp(ch["cum_col"]))
            s = s * spread(jnp.exp(ch["cum_last"])) + ch["upd"]
        s_ref[...] = s
        for ch in data:
            diag = []
            for pair in range(nh // 2):
                sc = []
                for j in (2 * pair, 2 * pair + 1):
                    lmat = jnp.exp(jnp.where(lower, ch["cum_col"][:, j:j + 1] - ch["cum_row"][j:j + 1, :], -jnp.inf))
                    sc.append((ch["cb"] * lmat).astype(BF16))
                xp = ch["xdt"][:, pair * 2 * hp:(pair + 1) * 2 * hp]
                rhs = jnp.concatenate([jnp.where(lane < hp, xp, 0.0), jnp.where(lane >= hp, xp, 0.0)], axis=0)
                diag.append(_bdot(jnp.concatenate(sc, axis=1), rhs))
            ch["diag"] = jnp.concatenate(diag, axis=1)
        for ch in data:
            y = ch["y"] + ch["diag"] + ch["xs"] * d_skip_x
            y = y * _silu(z_ref[pl.ds(ch["r0"], c_len), :].astype(F32))
            y = y * lax.rsqrt(jnp.mean(y * y, axis=-1, keepdims=True) + EPS) * norm_w
            o_ref[pl.ds(ch["r0"], c_len), :] = jnp.where(ch["valid_col"], y, 0.0).astype(o_ref.dtype)

    first = n_chunks % SSM_UNROLL
    for c in range(first):
        run_chunks([c])

    def step(it, _):
        run_chunks([first + it * SSM_UNROLL + j for j in range(SSM_UNROLL)])
        return 0

    lax.fori_loop(0, (n_chunks - first) // SSM_UNROLL, step, 0)


def ssd_core(zx, dt_raw, conv_w, conv_b, a_log, dt_bias, d_skip, norm_w, batch):
    m = zx.shape[0]
    lp = m // batch
    ng = 8
    width = SSM_HPG * SSM_HEAD_DIM
    d_inner = ng * width
    n_chunks = lp // SSM_CHUNK
    zx = zx.reshape(batch, lp, zx.shape[1])
    d5 = dt_raw.reshape(batch, n_chunks, SSM_CHUNK, ng, SSM_HPG)
    dcol = jnp.transpose(d5, (0, 3, 1, 2, 4))
    drow = jnp.transpose(d5, (0, 3, 1, 4, 2))
    p3 = jnp.stack([a_log.reshape(ng, SSM_HPG), dt_bias.reshape(ng, SSM_HPG), d_skip.reshape(ng, SSM_HPG)],
                   axis=1).astype(F32)
    prow = jnp.transpose(p3, (0, 2, 1))
    conv_b = conv_b.reshape(1, -1).astype(F32)
    xo, bo, co = d_inner // width, d_inner // SSM_STATE, (d_inner + ng * SSM_STATE) // SSM_STATE
    kern = functools.partial(_ssd_kernel, n_chunks=n_chunks)
    out = pl.pallas_call(
        kern,
        out_shape=jax.ShapeDtypeStruct((batch, lp, d_inner), BF16),
        grid=(batch, ng),
        in_specs=[
            pl.BlockSpec((None, lp, width), lambda b, g: (b, 0, g)),
            pl.BlockSpec((None, lp, width), lambda b, g: (b, 0, xo + g)),
            pl.BlockSpec((None, lp, SSM_STATE), lambda b, g: (b, 0, 2 * bo + g)),
            pl.BlockSpec((None, lp, SSM_STATE), lambda b, g: (b, 0, bo + co + g)),
            pl.BlockSpec((CONV_W, width), lambda b, g: (0, g)),
            pl.BlockSpec((CONV_W, SSM_STATE), lambda b, g: (0, bo + g)),
            pl.BlockSpec((CONV_W, SSM_STATE), lambda b, g: (0, co + g)),
            pl.BlockSpec((1, width), lambda b, g: (0, g)),
            pl.BlockSpec((1, SSM_STATE), lambda b, g: (0, bo + g)),
            pl.BlockSpec((1, SSM_STATE), lambda b, g: (0, co + g)),
            pl.BlockSpec((None, None, n_chunks, SSM_CHUNK, SSM_HPG), lambda b, g: (b, g, 0, 0, 0)),
            pl.BlockSpec((None, None, n_chunks, SSM_HPG, SSM_CHUNK), lambda b, g: (b, g, 0, 0, 0)),
            pl.BlockSpec((None, 3, SSM_HPG), lambda b, g: (g, 0, 0)),
            pl.BlockSpec((None, SSM_HPG, 3), lambda b, g: (g, 0, 0)),
            pl.BlockSpec((1, width), lambda b, g: (0, g)),
        ],
        out_specs=pl.BlockSpec((None, lp, width), lambda b, g: (b, 0, g)),
        scratch_shapes=[pltpu.VMEM((SSM_STATE, width), F32)],
        compiler_params=_params("parallel", "parallel"),
        name="ssd_core",
    )(zx, zx, zx, zx, conv_w, conv_w, conv_w, conv_b, conv_b, conv_b, dcol, drow, p3, prow,
      norm_w.reshape(1, d_inner).astype(F32))
    return out.reshape(m, d_inner)


def _rope_tables(lp):
    inv = ROPE_THETA ** (-jnp.arange(0, MLA_ROPE, 2, dtype=F32) / MLA_ROPE)
    pos = jnp.maximum(jnp.arange(lp, dtype=F32) - PAD, 0.0)
    ang = pos[:, None] * inv[None, :]
    return jnp.tile(jnp.cos(ang), (1, 4)), jnp.tile(jnp.sin(ang), (1, 4))


def _pair_head(lane):
    return lax.shift_right_logical(lane, 5) & 1


def _rope_pair(t, cos, sin):
    lane = lax.broadcasted_iota(jnp.int32, (1, t.shape[1]), 1)
    partner = pltpu.roll(t, t.shape[1] // 2, axis=1)
    return t * cos + jnp.where(lane < t.shape[1] // 2, -partner, partner) * sin


def _qprep_kernel(q_ref, gn_ref, gp_ref, cos_ref, sin_ref, o_ref, *, n_heads, scale):
    d = HEAD_DIM
    cos, sin = cos_ref[...], sin_ref[...]
    for h in range(n_heads):
        x = q_ref[:, h * d:(h + 1) * d]
        y = x * lax.rsqrt(jnp.mean(x * x, axis=-1, keepdims=True) + EPS) * gn_ref[:, h * d:(h + 1) * d]
        o_ref[:, h * d:(h + 1) * d] = (y * scale).astype(o_ref.dtype)
    r = lax.broadcasted_iota(jnp.int32, (d, d), 0)
    c = lax.broadcasted_iota(jnp.int32, (d, d), 1)
    same_head = (_pair_head(r) == _pair_head(c)).astype(F32)
    base = n_heads * d
    for p in range(n_heads // 2):
        x = q_ref[:, base + p * d:base + (p + 1) * d]
        ms = _hdot(x * x, same_head) * (1.0 / MLA_ROPE)
        y = x * lax.rsqrt(ms + EPS) * gp_ref[:, p * d:(p + 1) * d]
        o_ref[:, base + p * d:base + (p + 1) * d] = (_rope_pair(y, cos, sin) * scale).astype(o_ref.dtype)


def _kprep_kernel(kv_ref, kpe_ref, gn_ref, gp_ref, cos_ref, sin_ref, k_ref, v_ref, pe_ref, *, n_heads):
    d = HEAD_DIM
    for h in range(n_heads):
        x = kv_ref[:, h * d:(h + 1) * d]
        y = x * lax.rsqrt(jnp.mean(x * x, axis=-1, keepdims=True) + EPS) * gn_ref[:, h * d:(h + 1) * d]
        k_ref[:, h * d:(h + 1) * d] = y.astype(k_ref.dtype)
    v_ref[...] = kv_ref[:, n_heads * d:].astype(v_ref.dtype)
    x = kpe_ref[...]
    y = x * lax.rsqrt(jnp.mean(x * x, axis=-1, keepdims=True) + EPS) * gp_ref[...]
    pe_ref[...] = _rope_pair(y, cos_ref[...], sin_ref[...]).astype(pe_ref.dtype)


def _flash_kernel(qn_ref, qpe_ref, kn_ref, kpe_ref, v_ref, o_ref, *, kv_blk):
    d = HEAD_DIM
    blk = qn_ref.shape[0]
    lp = kn_ref.shape[0]
    i = pl.program_id(2)
    lane = lax.broadcasted_iota(jnp.int32, (1, d), 1)
    qpe = qpe_ref[...]
    qf = []
    for hh in range(2):
        mine = _pair_head(lane) == hh
        qf.append(jnp.concatenate([qn_ref[:, hh * d:(hh + 1) * d], jnp.where(mine, qpe, jnp.zeros_like(qpe))],
                                  axis=1))
    qpos = i * blk + lax.broadcasted_iota(jnp.int32, (blk, 1), 0)
    q_limit = jnp.maximum(qpos, PAD)

    def body(j, carry, masked):
        c0 = pl.multiple_of(jnp.minimum(PAD + j * kv_blk, lp - kv_blk), HALO)
        kpe = kpe_ref[pl.ds(c0, kv_blk), :]
        scores = []
        for hh in range(2):
            kf = jnp.concatenate([kn_ref[pl.ds(c0, kv_blk), hh * d:(hh + 1) * d], kpe], axis=1)
            scores.append(lax.dot_general(qf[hh], kf, NT, preferred_element_type=F32))
        if masked:
            kpos = c0 + lax.broadcasted_iota(jnp.int32, (1, kv_blk), 1)
            ok = (kpos >= PAD + j * kv_blk) & (kpos <= q_limit)
            scores = [jnp.where(ok, s, -1e30) for s in scores]
        stats = []
        for hh in range(2):
            m, l, _ = carry[hh]
            s = scores[hh]
            m_new = jnp.maximum(m, jnp.max(s, axis=-1, keepdims=True))
            p = jnp.exp(s - m_new)
            alpha = jnp.exp(m - m_new)
            stats.append((m_new, alpha * l + jnp.sum(p, axis=-1, keepdims=True), alpha, p.astype(BF16)))
        new = []
        for hh in range(2):
            m_new, l, alpha, p = stats[hh]
            pv = jnp.dot(p, v_ref[pl.ds(c0, kv_blk), hh * d:(hh + 1) * d], preferred_element_type=F32)
            new.append((m_new, l, alpha * carry[hh][2] + pv))
        return tuple(new)

    init = tuple((jnp.full((blk, 1), -1e30, F32), jnp.zeros((blk, 1), F32), jnp.zeros((blk, d), F32))
                 for _ in range(2))
    n_free = lax.div(jnp.maximum(i * blk + 1 - PAD, 0), kv_blk)
    n_kv = lax.div(i * blk + blk - PAD + kv_blk - 1, kv_blk)
    res = lax.fori_loop(0, n_free, functools.partial(body, masked=False), init)
    res = lax.fori_loop(n_free, n_kv, functools.partial(body, masked=True), res)
    out = jnp.concatenate([acc / l for (_, l, acc) in res], axis=1)
    o_ref[...] = jnp.where(qpos >= PAD, out, 0.0).astype(o_ref.dtype)


def mla_attention(q_raw, kv_raw, kpe_src, kpe_col_block, q_norm, k_norm, batch):
    m = q_raw.shape[0]
    lp = m // batch
    nh = 16
    d = HEAD_DIM
    half = MLA_ROPE // 2
    scale = (d + MLA_ROPE) ** -0.5
    cos, sin = _rope_tables(lp)
    qn_gain = jnp.tile(q_norm[:d], nh).reshape(1, nh * d).astype(F32)
    kn_gain = jnp.tile(k_norm[:d], nh).reshape(1, nh * d).astype(F32)
    qf, qs = q_norm[d:d + half], q_norm[d + half:]
    kf, ks = k_norm[d:d + half], k_norm[d + half:]
    qp_gain = jnp.tile(jnp.concatenate([qf, qf, qs, qs]), nh // 2).reshape(1, nh // 2 * d).astype(F32)
    kp_gain = jnp.concatenate([kf, kf, ks, ks]).reshape(1, d).astype(F32)
    tp = ATT_PREP_ROWS
    nt = lp // tp
    qw = q_raw.shape[1]

    q_prep = pl.pallas_call(
        functools.partial(_qprep_kernel, n_heads=nh, scale=scale),
        out_shape=jax.ShapeDtypeStruct((batch, lp, qw), BF16),
        grid=(batch, nt),
        in_specs=[
            pl.BlockSpec((None, tp, qw), lambda b, t: (b, t, 0)),
            pl.BlockSpec((1, nh * d), lambda b, t: (0, 0)),
            pl.BlockSpec((1, nh // 2 * d), lambda b, t: (0, 0)),
            pl.BlockSpec((tp, d), lambda b, t: (t, 0)),
            pl.BlockSpec((tp, d), lambda b, t: (t, 0)),
        ],
        out_specs=pl.BlockSpec((None, tp, qw), lambda b, t: (b, t, 0)),
        compiler_params=_params("parallel", "parallel"),
        name="mla_q_prep",
    )(q_raw.reshape(batch, lp, qw), qn_gain, qp_gain, cos, sin)

    kvw = kv_raw.shape[1]
    kpe3 = kpe_src.reshape(batch, lp, kpe_src.shape[1])
    k_prep, v_prep, pe_prep = pl.pallas_call(
        functools.partial(_kprep_kernel, n_heads=nh),
        out_shape=(jax.ShapeDtypeStruct((batch, lp, nh * d), BF16),
                   jax.ShapeDtypeStruct((batch, lp, nh * d), BF16),
                   jax.ShapeDtypeStruct((batch, lp, d), BF16)),
        grid=(batch, nt),
        in_specs=[
            pl.BlockSpec((None, tp, kvw), lambda b, t: (b, t, 0)),
            pl.BlockSpec((None, tp, d), lambda b, t: (b, t, kpe_col_block)),
            pl.BlockSpec((1, nh * d), lambda b, t: (0, 0)),
            pl.BlockSpec((1, d), lambda b, t: (0, 0)),
            pl.BlockSpec((tp, d), lambda b, t: (t, 0)),
            pl.BlockSpec((tp, d), lambda b, t: (t, 0)),
        ],
        out_specs=(pl.BlockSpec((None, tp, nh * d), lambda b, t: (b, t, 0)),
                   pl.BlockSpec((None, tp, nh * d), lambda b, t: (b, t, 0)),
                   pl.BlockSpec((None, tp, d), lambda b, t: (b, t, 0))),
        compiler_params=_params("parallel", "parallel"),
        name="mla_kv_prep",
    )(kv_raw.reshape(batch, lp, kvw), kpe3, kn_gain, kp_gain, cos, sin)

    np_ = nh // 2
    tq = next(t for t in range(min(lp, ATT_MAX_Q_ROWS) // 16 * 16, 0, -16) if lp % t == 0)
    out = pl.pallas_call(
        functools.partial(_flash_kernel, kv_blk=min(ATT_KV_BLK, lp)),
        out_shape=jax.ShapeDtypeStruct((batch, lp, nh * d), BF16),
        grid=(batch, np_, lp // tq),
        in_specs=[
            pl.BlockSpec((None, tq, 2 * d), lambda b, p, i: (b, i, p)),
            pl.BlockSpec((None, tq, d), lambda b, p, i: (b, i, nh + p)),
            pl.BlockSpec((None, lp, 2 * d), lambda b, p, i: (b, 0, p)),
            pl.BlockSpec((None, lp, d), lambda b, p, i: (b, 0, 0)),
            pl.BlockSpec((None, lp, 2 * d), lambda b, p, i: (b, 0, p)),
        ],
        out_specs=pl.BlockSpec((None, tq, 2 * d), lambda b, p, i: (b, i, p)),
        compiler_params=_params("parallel", "parallel", "arbitrary"),
        name="mla_flash",
    )(q_prep, q_prep, k_prep, pe_prep, v_prep)
    return out.reshape(m, nh * d)


def _gdn_layer(stream, next_gain, w_in_all, layer, conv_w, a_log, dt_bias, norm_w, w_out_all, batch):
    h, hb, ss = stream
    n_gate = 2 * a_log.shape[0]
    n_big = w_in_all.shape[2] - n_gate
    qkvz = scaled_matmul(hb, ss, w_in_all, layer=layer, n=n_big, name="gdn_in_proj")
    ba = scaled_matmul(hb, ss, w_in_all[layer, :, n_big:], out_dtype=F32, name="gdn_gate_proj")
    o = gdn_core(qkvz, ba, conv_w.astype(F32), a_log, dt_bias, norm_w, batch)
    return matmul_residual(o, w_out_all, h, next_gain, layer=layer, name="gdn_out_proj")


def _mla_layer(stream, next_gain, w_in, norm_q_lat, norm_kv_lat, w_uq, w_ukv, q_norm, k_norm, w_out_all, layer, batch):
    h, hb, ss = stream
    nh, d, r = 16, HEAD_DIM, MLA_ROPE
    q_lora, kv_lora = norm_q_lat.shape[0], norm_kv_lat.shape[0]
    half = r // 2
    w_cq, w_ckv, w_pe = w_in[:, :q_lora], w_in[:, q_lora:q_lora + kv_lora], w_in[:, q_lora + kv_lora:]
    w_pe_t = jnp.concatenate([w_pe[:, :half], w_pe[:, :half], w_pe[:, half:], w_pe[:, half:]], axis=1)
    gap = (-(kv_lora + d)) % q_lora
    w_lat = jnp.concatenate([w_ckv, w_pe_t, jnp.zeros((w_in.shape[0], gap), w_in.dtype), w_cq], axis=1)
    lat = scaled_matmul(hb, ss, w_lat, out_dtype=F32, name="mla_in_proj")
    wq = w_uq.reshape(q_lora, nh, d + r)
    wq_pe = wq[:, :, d:].reshape(q_lora, nh // 2, 2, 2, half)
    wq_pe = jnp.transpose(wq_pe, (0, 1, 3, 2, 4)).reshape(q_lora, nh // 2 * d)
    wq_all = jnp.concatenate([wq[:, :, :d].reshape(q_lora, nh * d), wq_pe], axis=1)
    q_raw = norm_matmul(lat, norm_q_lat, wq_all.astype(BF16), x_col_block=(kv_lora + d + gap) // q_lora,
                        out_dtype=F32, name="mla_q_up")
    wkv = w_ukv.reshape(kv_lora, nh, 2 * d)
    wkv_all = jnp.concatenate([wkv[:, :, :d].reshape(kv_lora, nh * d), wkv[:, :, d:].reshape(kv_lora, nh * d)],
                              axis=1)
    kv_raw = norm_matmul(lat, norm_kv_lat, wkv_all.astype(BF16), x_col_block=0, out_dtype=F32, name="mla_kv_up")
    o = mla_attention(q_raw, kv_raw, lat, kv_lora // d, q_norm, k_norm, batch)
    return matmul_residual(o, w_out_all, h, next_gain, layer=layer, name="mla_out_proj")


def _ssm_layer(stream, next_gain, w_in_all, layer, conv_w, conv_b, a_log, dt_bias, d_skip, norm_w, w_out_all, batch):
    h, hb, ss = stream
    n_dt = a_log.shape[0]
    n_big = w_in_all.shape[2] - n_dt
    zx = scaled_matmul(hb, ss, w_in_all, layer=layer, n=n_big, name="ssm_in_proj")
    dt_raw = scaled_matmul(hb, ss, w_in_all[layer, :, n_big:], out_dtype=F32, name="ssm_dt_proj")
    y = ssd_core(zx, dt_raw, conv_w.astype(F32), conv_b, a_log, dt_bias, d_skip, norm_w, batch)
    return matmul_residual(y, w_out_all, h, next_gain, layer=layer, name="ssm_out_proj")


def _mlp_layer(stream, next_gain, w_up_all, w_down_all, layer):
    h, hb, ss = stream
    act = scaled_matmul(hb, ss, w_up_all, layer=layer, act="relu2", name="mlp_up")
    return matmul_residual(act, w_down_all, h, next_gain, layer=layer, name="mlp_down")


def kernel(x, meta_tokens, norm_mix, norm_mlp, mlp_w_up, mlp_w_down, gdn_w_in, gdn_conv_w, gdn_a_log, gdn_dt_bias, gdn_norm, gdn_w_out, mla_w_in, mla_norm_q_lat, mla_norm_kv_lat, mla_w_uq, mla_w_ukv, mla_q_norm, mla_k_norm, mla_w_out, ssm_w_in, ssm_conv_w, ssm_conv_b, ssm_a_log, ssm_dt_bias, ssm_d, ssm_norm, ssm_w_out):
    batch, seq, dm = x.shape
    depth = norm_mix.shape[0]
    lp = ROW_TILE + seq
    meta = jnp.broadcast_to(meta_tokens.astype(x.dtype)[None], (batch, N_META, dm))
    h = jnp.concatenate([jnp.zeros((batch, PAD, dm), x.dtype), meta, x], axis=1).reshape(batch * lp, dm)
    stream = (h,) + tuple(stream_entry(h, norm_mix[0]))
    down_b, gdn_out_b, ssm_out_b, mla_out_b = (w.astype(BF16) for w in (mlp_w_down, gdn_w_out, ssm_w_out, mla_w_out))
    gdn_in_b, ssm_in_b = gdn_w_in.astype(BF16), ssm_w_in.astype(BF16)
    ia = ib = ic = 0
    for i in range(depth):
        kind = i % 3
        if kind == 0:
            stream = _gdn_layer(stream, norm_mlp[i], gdn_in_b, ia, gdn_conv_w[ia], gdn_a_log[ia], gdn_dt_bias[ia],
                                gdn_norm[ia], gdn_out_b, batch)
            ia += 1
        elif kind == 1:
            stream = _mla_layer(stream, norm_mlp[i], mla_w_in[ib], mla_norm_q_lat[ib], mla_norm_kv_lat[ib],
                                mla_w_uq[ib], mla_w_ukv[ib], mla_q_norm[ib], mla_k_norm[ib], mla_out_b, ib, batch)
            ib += 1
        else:
            stream = _ssm_layer(stream, norm_mlp[i], ssm_in_b, ic, ssm_conv_w[ic], ssm_conv_b[ic], ssm_a_log[ic],
                                ssm_dt_bias[ic], ssm_d[ic], ssm_norm[ic], ssm_out_b, batch)
            ic += 1
        following = norm_mix[i + 1] if i + 1 < depth else jnp.ones_like(norm_mlp[i])
        stream = _mlp_layer(stream, following, mlp_w_up, down_b, i)
    return stream[0].reshape(batch, lp, dm)[:, ROW_TILE:]
```

```python
import functools
import math

import jax
import jax.numpy as jnp
from jax import lax
from jax.experimental import pallas as pl
from jax.experimental.pallas import tpu as pltpu

F32 = jnp.float32
BF16 = jnp.bfloat16
HI = lax.Precision.HIGHEST

EPS = 1e-6
N_META = 16
ROW_TILE = 128
PAD = ROW_TILE - N_META
LANES = 128
HALO = 16
CONV_W = 4

HEAD_DIM = 128
GDN_CHUNK = 64
SSM_CHUNK = 128
SSM_HEAD_DIM = 64
SSM_HPG = 8
SSM_STATE = 128
MLA_ROPE = 64
ROPE_THETA = 10000.0
ATT_MAX_Q_ROWS = 320
ATT_KV_BLK = 512
GDN_QK_PER_STEP = 2
SSM_UNROLL = 2
GDN_MAX_UNROLL = 3

VMEM_LIMIT = 56 * 1024 * 1024
MAX_ROW_TILE = 1088
MAX_K_TILE = 2048
NT = (((1,), (1,)), ((), ()))
TN = (((0,), (0,)), ((), ()))


def _row_tile(m, cap=MAX_ROW_TILE):
    for t in range(min(m, cap) // 64 * 64, 0, -64):
        if m % t == 0:
            return t
    raise ValueError(f"row count {m} has no tile that is a multiple of 64")


def _col_tile(n, cap=1024):
    if n <= cap:
        return n
    for t in range(cap, 0, -128):
        if n % t == 0:
            return t
    raise ValueError(f"column count {n} has no tile that is a multiple of 128")


def _params(*sem):
    return pltpu.CompilerParams(dimension_semantics=sem, vmem_limit_bytes=VMEM_LIMIT)


def _bdot(a, b, dims=None):
    a = a.astype(BF16)
    b = b.astype(BF16)
    if dims is None:
        return jnp.dot(a, b, preferred_element_type=F32)
    return lax.dot_general(a, b, dims, preferred_element_type=F32)


def _hdot(a, b):
    return jnp.dot(a, b, precision=HI, preferred_element_type=F32)


def _sigmoid(x):
    return 0.5 + 0.5 * jnp.tanh(0.5 * x)


def _silu(x):
    return x * _sigmoid(x)


def _softplus(x):
    return jnp.maximum(x, 0.0) + jnp.log(1.0 + jnp.exp(-jnp.abs(x)))


def _stats_rows(tm, cap=272):
    return next(t for t in range(min(tm, cap) // 16 * 16, 0, -16) if tm % t == 0)


def _stream_stats(h_ref, g_ref, hb_ref, ss_ref, sub):
    def body(r, _):
        r0 = pl.multiple_of(r * sub, sub)
        x = h_ref[pl.ds(r0, sub), :]
        hb_ref[pl.ds(r0, sub), :] = (x * g_ref[...]).astype(BF16)
        ss_ref[pl.ds(r0, sub), :] = jnp.broadcast_to(jnp.sum(x * x, axis=-1, keepdims=True), (sub, ss_ref.shape[1]))
        return 0

    lax.fori_loop(0, h_ref.shape[0] // sub, body, 0)


def _matmul_residual_kernel(a_ref, w_ref, h_ref, g_ref, o_ref, hb_ref, ss_ref, *, sub):
    @pl.when(pl.program_id(2) == 0)
    def _():
        o_ref[...] = h_ref[...]

    last = pl.program_id(2) == pl.num_programs(2) - 1

    @pl.when(jnp.logical_not(last))
    def _():
        o_ref[...] += jnp.dot(a_ref[...], w_ref[...], preferred_element_type=F32)

    @pl.when(last)
    def _():
        new = o_ref[...] + jnp.dot(a_ref[...], w_ref[...], preferred_element_type=F32)
        o_ref[...] = new
        hb_ref[...] = (new * g_ref[...]).astype(BF16)
        ss_ref[...] = jnp.broadcast_to(jnp.sum(new * new, axis=-1, keepdims=True), ss_ref.shape)


def _stream_out(m, n, tm, tn, index):
    shapes = (jax.ShapeDtypeStruct((m, n), F32), jax.ShapeDtypeStruct((m, n), BF16),
              jax.ShapeDtypeStruct((n // tn, m, LANES), F32))
    specs = (pl.BlockSpec((tm, tn), lambda *g: index(*g)),
             pl.BlockSpec((tm, tn), lambda *g: index(*g)),
             pl.BlockSpec((None, tm, LANES), lambda *g: (index(*g)[1], index(*g)[0], 0)))
    return shapes, specs


def matmul_residual(a, w, h, next_gain, *, layer=0, name):
    if w.ndim == 2:
        w = w[None]
    m, k = a.shape
    n = w.shape[2]
    tm, tn, tk = _row_tile(m), _col_tile(n), _col_tile(k, MAX_K_TILE)
    shapes, specs = _stream_out(m, n, tm, tn, lambda i, j, kk: (i, j))
    return pl.pallas_call(
        functools.partial(_matmul_residual_kernel, sub=_stats_rows(tm)),
        out_shape=shapes,
        grid=(m // tm, n // tn, k // tk),
        in_specs=[
            pl.BlockSpec((tm, tk), lambda i, j, kk: (i, kk)),
            pl.BlockSpec((None, tk, tn), lambda i, j, kk: (layer, kk, j)),
            pl.BlockSpec((tm, tn), lambda i, j, kk: (i, j)),
            pl.BlockSpec((1, tn), lambda i, j, kk: (0, j)),
        ],
        out_specs=specs,
        compiler_params=_params("parallel", "parallel", "arbitrary"),
        name=name,
    )(a, w, h, next_gain.reshape(1, n).astype(F32))


def _stream_entry_kernel(h_ref, g_ref, hb_ref, ss_ref, *, sub):
    _stream_stats(h_ref, g_ref, hb_ref, ss_ref, sub)


def stream_entry(h, next_gain):
    m, n = h.shape
    tm, tn = _row_tile(m), _col_tile(n)
    shapes, specs = _stream_out(m, n, tm, tn, lambda i, j: (i, j))
    return pl.pallas_call(
        functools.partial(_stream_entry_kernel, sub=_stats_rows(tm)),
        out_shape=shapes[1:],
        grid=(m // tm, n // tn),
        in_specs=[pl.BlockSpec((tm, tn), lambda i, j: (i, j)), pl.BlockSpec((1, tn), lambda i, j: (0, j))],
        out_specs=specs[1:],
        compiler_params=_params("parallel", "parallel"),
        name="stream_entry",
    )(h, next_gain.reshape(1, n).astype(F32))


def _scaled_matmul_kernel(x_ref, ss_ref, w_ref, o_ref, *scratch, act, inv_k, sub):
    if scratch:
        wb_ref, = scratch

        @pl.when(pl.program_id(1) == 0)
        def _():
            def body(r, _):
                r0 = pl.multiple_of(r * sub, sub)
                wb_ref[pl.ds(r0, sub), :] = w_ref[pl.ds(r0, sub), :].astype(BF16)
                return 0

            lax.fori_loop(0, w_ref.shape[0] // sub, body, 0)
    else:
        wb_ref = w_ref

    ss = ss_ref[0]
    for t in range(1, ss_ref.shape[0]):
        ss = ss + ss_ref[t]
    y = jnp.dot(x_ref[...], wb_ref[...], preferred_element_type=F32) * lax.rsqrt(ss[:, 0:1] * inv_k + EPS)
    if act == "relu2":
        y = jnp.square(jnp.maximum(y, 0.0))
    o_ref[...] = y.astype(o_ref.dtype)


def scaled_matmul(hb, ss, w, *, layer=0, n=None, act=None, out_dtype=BF16, name):
    if w.ndim == 2:
        w = w[None]
    m, k = hb.shape
    n = w.shape[2] if n is None else n
    tm, tn = _row_tile(m), _col_tile(n)
    return pl.pallas_call(
        functools.partial(_scaled_matmul_kernel, act=act, inv_k=1.0 / k, sub=256),
        out_shape=jax.ShapeDtypeStruct((m, n), out_dtype),
        grid=(n // tn, m // tm),
        in_specs=[
            pl.BlockSpec((tm, k), lambda j, i: (i, 0)),
            pl.BlockSpec((ss.shape[0], tm, LANES), lambda j, i: (0, i, 0)),
            pl.BlockSpec((None, k, tn), lambda j, i: (layer, 0, j)),
        ],
        out_specs=pl.BlockSpec((tm, tn), lambda j, i: (i, j)),
        scratch_shapes=[] if w.dtype == BF16 else [pltpu.VMEM((k, tn), BF16)],
        compiler_params=_params("arbitrary", "arbitrary"),
        name=name,
    )(hb, ss, w)


def _conv_silu(ref, w, r0, rows, bias=None):
    cur = ref[pl.ds(r0, rows), :].astype(F32)
    halo = ref[pl.ds(pl.multiple_of(jnp.maximum(r0 - HALO, 0), HALO), HALO), :].astype(F32)
    x = jnp.concatenate([halo, cur], axis=0)
    y = cur * w[CONV_W - 1:CONV_W, :]
    for j in range(CONV_W - 1):
        y = y + pltpu.roll(x, CONV_W - 1 - j, axis=0)[HALO:, :] * w[j:j + 1, :]
    if bias is not None:
        y = y + bias
    return _silu(y)


def _tri(n, strict=False):
    r = lax.broadcasted_iota(jnp.int32, (n, n), 0)
    c = lax.broadcasted_iota(jnp.int32, (n, n), 1)
    return (r > c) if strict else (r >= c)


def _split_bf16(x):
    hi = x.astype(BF16)
    return hi, (x - hi.astype(F32)).astype(BF16)


def _pair_blockdiag(x, first):
    z = jnp.zeros_like(x)
    return jnp.concatenate([jnp.where(first, x, z), jnp.where(first, z, x)], axis=0)


def _pair_matmul3(a_parts, b_parts, first):
    a_hi, a_lo = a_parts
    bh = _pair_blockdiag(b_parts[0], first)
    bl = _pair_blockdiag(b_parts[1], first)
    lhs = jnp.concatenate([a_hi, a_lo], axis=1)
    rhs = jnp.concatenate([jnp.concatenate([bh, bl], axis=1),
                           jnp.concatenate([bh, jnp.zeros_like(bl)], axis=1)], axis=0)
    r = jnp.dot(lhs, rhs, preferred_element_type=F32)
    w = a_hi.shape[1]
    return r[:, :w] + r[:, w:]


def _pair_unit_lower_inverses(mats, n, first, between):
    row = lax.broadcasted_iota(jnp.int32, (n, 2 * n), 0)
    col = lax.broadcasted_iota(jnp.int32, (n, 2 * n), 1) & (n - 1)
    eye = (row == col).astype(F32)
    ps = [-a for a in mats]
    ts = [eye + p for p in ps]
    splits = [_split_bf16(p) for p in ps]
    ps = [_pair_matmul3(s, s, first) for s in splits]
    between()
    k = 2
    while k < n // 2:
        splits = [_split_bf16(p) for p in ps]
        stacked = []
        for t, s in zip(ts, splits):
            t_hi, t_lo = _split_bf16(t)
            lhs = (jnp.concatenate([t_hi, s[0]], axis=0), jnp.concatenate([t_lo, s[1]], axis=0))
            stacked.append(_pair_matmul3(lhs, s, first))
        ts = [t + r[:n] for t, r in zip(ts, stacked)]
        ps = [r[n:] for r in stacked]
        between()
        k *= 2
    return [t + _pair_matmul3(_split_bf16(t), _split_bf16(p), first) for t, p in zip(ts, ps)]


def _gdn_kernel(q_ref, k_ref, v_ref, z_ref, wq_ref, wk_ref, wv_ref, gcol_ref, grow_ref, pcol_ref, prow_ref,
                nw_ref, o_ref, s_ref, wq_s, u_s, oi_s, kd_s, eg_s, *, n_chunks, first_chunk, unroll):
    c_len = GDN_CHUNK
    d = HEAD_DIM
    nq = GDN_QK_PER_STEP
    nv = 2 * nq
    wq, wk, wv = wq_ref[...], wk_ref[...], wv_ref[...]
    neg_a_col = -jnp.exp(pcol_ref[:, 0:nv])
    dtb_col = pcol_ref[:, nv:2 * nv]
    neg_a_pair = -jnp.exp(prow_ref[0:nq, :])
    dtb_pair = prow_ref[nq:2 * nq, :]
    pair_row = lax.broadcasted_iota(jnp.int32, (c_len, 2 * c_len), 0)
    pair_lane = lax.broadcasted_iota(jnp.int32, (c_len, 2 * c_len), 1)
    pair_col = pair_lane & (c_len - 1)
    first = pair_lane < c_len
    lower_pair = pair_row >= pair_col
    strict_pair = pair_row > pair_col
    lower_f = _tri(c_len).astype(F32)
    r2 = lax.broadcasted_iota(jnp.int32, (2 * c_len, 2 * c_len), 0)
    c2 = lax.broadcasted_iota(jnp.int32, (2 * c_len, 2 * c_len), 1)
    upper_pair = (((r2 < c_len) == (c2 < c_len)) & ((r2 & (c_len - 1)) <= (c2 & (c_len - 1)))).astype(F32)
    norm_w = nw_ref[...]

    def load_chunk(c):
        r0 = pl.multiple_of(c * c_len, c_len)
        valid_col = (r0 + lax.broadcasted_iota(jnp.int32, (c_len, 1), 0)) >= PAD
        valid_pair = (r0 + pair_col[0:1, :]) >= PAD
        gc = gcol_ref[c]
        g_col = jnp.where(valid_col, neg_a_col * _softplus(gc[:, nv:2 * nv] + dtb_col), 0.0)
        g_pair = jnp.where(valid_pair, neg_a_pair * _softplus(grow_ref[c] + dtb_pair), 0.0)
        return dict(
            c=c,
            q=jnp.where(valid_col, _conv_silu(q_ref, wq, r0, c_len), 0.0),
            k=jnp.where(valid_col, _conv_silu(k_ref, wk, r0, c_len), 0.0),
            v=jnp.where(valid_col, _conv_silu(v_ref, wv, r0, c_len), 0.0),
            beta=jnp.where(valid_col, _sigmoid(gc[:, 0:nv]), 0.0),
            cum_col=_hdot(lower_f, g_col),
            cum_pair=_hdot(g_pair, upper_pair),
        )

    def prepare_group(c_first, between):
        chunks = [load_chunk(c_first + j) for j in range(unroll)]
        probs = [dict(ch=ch, a=a) for ch in chunks for a in range(nq)]
        for pr in probs:
            ch, a = pr["ch"], pr["a"]
            q = ch["q"][:, a * d:(a + 1) * d]
            k = ch["k"][:, a * d:(a + 1) * d]
            pr["qn"] = q * lax.rsqrt(jnp.sum(q * q, axis=-1, keepdims=True) + EPS) * (d ** -0.5)
            pr["kn"] = k * lax.rsqrt(jnp.sum(k * k, axis=-1, keepdims=True) + EPS)
        for pr in probs:
            kn_b = pr["kn"].astype(BF16)
            both = lax.dot_general(jnp.concatenate([kn_b, pr["qn"].astype(BF16)], axis=0),
                                   jnp.concatenate([kn_b, kn_b], axis=0), NT, preferred_element_type=F32)
            pr["kk"], pr["qk"] = both[:c_len], both[c_len:]
        between()
        for pr in probs:
            ch, a = pr["ch"], pr["a"]
            h0, h1 = 2 * a, 2 * a + 1
            gcol = jnp.where(first, ch["cum_col"][:, h0:h0 + 1], ch["cum_col"][:, h1:h1 + 1])
            bcol = jnp.where(first, ch["beta"][:, h0:h0 + 1], ch["beta"][:, h1:h1 + 1])
            pr["decay"] = jnp.exp(jnp.where(lower_pair, gcol - ch["cum_pair"][a:a + 1, :], -jnp.inf))
            pr["amat"] = jnp.where(strict_pair, pr["kk"] * bcol * pr["decay"], 0.0)
        ts = _pair_unit_lower_inverses([pr["amat"] for pr in probs], c_len, first, between)
        for pr, t in zip(probs, ts):
            ch, a = pr["ch"], pr["a"]
            rhs = []
            for vh in (2 * a, 2 * a + 1):
                b1 = ch["beta"][:, vh:vh + 1]
                rhs.append(jnp.concatenate([ch["v"][:, vh * d:(vh + 1) * d] * b1,
                                            pr["kn"] * (b1 * jnp.exp(ch["cum_col"][:, vh:vh + 1]))], axis=1))
            zero = jnp.zeros_like(rhs[0])
            pr["uw"] = _bdot(t, jnp.concatenate([jnp.concatenate([rhs[0], zero], axis=1),
                                                 jnp.concatenate([zero, rhs[1]], axis=1)], axis=0))
        for pr in probs:
            uw = pr["uw"]
            zero = jnp.zeros_like(uw[:, :2 * d])
            pr["ow"] = _bdot(pr["qk"] * pr["decay"], jnp.concatenate(
                [jnp.concatenate([uw[:, :2 * d], zero], axis=1),
                 jnp.concatenate([zero, uw[:, 2 * d:]], axis=1)], axis=0))
        for pr in probs:
            ch, a, uw, ow = pr["ch"], pr["a"], pr["uw"], pr["ow"]
            c = ch["c"]
            for i, vh in enumerate((2 * a, 2 * a + 1)):
                gcol1 = ch["cum_col"][:, vh:vh + 1]
                g_last = ch["cum_col"][c_len - 1:c_len, vh:vh + 1]
                u, w = uw[:, 2 * i * d:(2 * i + 1) * d], uw[:, (2 * i + 1) * d:(2 * i + 2) * d]
                oi, qw = ow[:, 2 * i * d:(2 * i + 1) * d], ow[:, (2 * i + 1) * d:(2 * i + 2) * d]
                wq_s[c, vh, pl.ds(0, c_len), :] = w.astype(BF16)
                wq_s[c, vh, pl.ds(c_len, c_len), :] = (pr["qn"] * jnp.exp(gcol1) - qw).astype(BF16)
                u_s[c, vh] = u
                oi_s[c, vh] = oi
                kd_s[c, vh] = (pr["kn"] * jnp.exp(g_last - gcol1)).astype(BF16)
                eg_s[c, vh] = jnp.broadcast_to(jnp.exp(g_last), (8, d))

    def recur_group(c_first):
        heads = range(nv)
        state = dict(s=[s_ref[vh] for vh in heads])

        def products(c):
            state["sq"] = [jnp.dot(wq_s[c, vh], state["s"][vh].astype(BF16), preferred_element_type=F32)
                           for vh in heads]

        def update(c, last):
            r0 = pl.multiple_of(c * c_len, c_len)
            valid_col = (r0 + lax.broadcasted_iota(jnp.int32, (c_len, 1), 0)) >= PAD
            z_all = z_ref[pl.ds(r0, c_len), :].astype(F32)
            sq = state["sq"]
            upd = [_bdot(kd_s[c, vh], u_s[c, vh] - sq[vh][:c_len], TN) for vh in heads]
            state["s"] = [state["s"][vh] * eg_s[c, vh][0:1, :] + upd[vh] for vh in heads]
            outs = []
            for vh in heads:
                o = oi_s[c, vh] + sq[vh][c_len:]
                o = o * lax.rsqrt(jnp.mean(o * o, axis=-1, keepdims=True) + EPS) * norm_w
                outs.append(o * _silu(z_all[:, vh * d:(vh + 1) * d]))
            out = jnp.where(valid_col, jnp.concatenate(outs, axis=1), 0.0)
            o_ref[pl.ds(r0, c_len), :] = out.astype(o_ref.dtype)
            if last:
                for vh in heads:
                    s_ref[vh] = state["s"][vh]

        steps = []
        for j in range(unroll):
            steps.append(functools.partial(products, c_first + j))
            steps.append(functools.partial(update, c_first + j, j == unroll - 1))
        return steps

    def run_between(steps):
        pending = list(steps)

        def between():
            if pending:
                pending.pop(0)()
        return between, pending

    if first_chunk:
        o_ref[pl.ds(0, first_chunk * c_len), :] = jnp.zeros((first_chunk * c_len, nv * d), o_ref.dtype)
    s_ref[...] = jnp.zeros_like(s_ref)
    n_groups = (n_chunks - first_chunk) // unroll

    prepare_group(first_chunk, lambda: None)

    def group_step(g, _):
        between, pending = run_between(recur_group(first_chunk + (g - 1) * unroll))
        prepare_group(first_chunk + g * unroll, between)
        for step in pending:
            step()
        return 0

    lax.fori_loop(1, n_groups, group_step, 0)
    for step in recur_group(first_chunk + (n_groups - 1) * unroll):
        step()


def gdn_core(qkvz, ba, conv_w, a_log, dt_bias, norm_w, batch):
    m = qkvz.shape[0]
    lp = m // batch
    hk = 16
    d = HEAD_DIM
    nq = GDN_QK_PER_STEP
    nv = 2 * nq
    ng = hk // nq
    n_chunks = lp // GDN_CHUNK
    first_chunk, unroll = next((f, u) for u in range(GDN_MAX_UNROLL, 0, -1) for f in (1, 0)
                               if (n_chunks - f) % u == 0)
    qkvz = qkvz.reshape(batch, lp, qkvz.shape[1])
    b_log = ba[:, :2 * hk].reshape(batch, n_chunks, GDN_CHUNK, ng, nv)
    a_log_t = ba[:, 2 * hk:].reshape(batch, n_chunks, GDN_CHUNK, ng, nq, 2)
    gcol = jnp.transpose(jnp.concatenate([b_log, a_log_t.reshape(b_log.shape)], axis=-1), (0, 3, 1, 2, 4))
    grow = jnp.transpose(a_log_t, (0, 3, 1, 4, 5, 2)).reshape(batch, ng, n_chunks, nq, 2 * GDN_CHUNK)
    p = jnp.concatenate([a_log.reshape(ng, nv), dt_bias.reshape(ng, nv)], axis=-1).astype(F32)
    pcol = p.reshape(ng, 1, 2 * nv)
    prow = jnp.concatenate([jnp.repeat(a_log.reshape(ng, nq, 2), GDN_CHUNK, axis=-1),
                            jnp.repeat(dt_bias.reshape(ng, nq, 2), GDN_CHUNK, axis=-1)], axis=1).astype(F32)
    qb, vb = nq * d, nv * d
    kern = functools.partial(_gdn_kernel, n_chunks=n_chunks, first_chunk=first_chunk, unroll=unroll)
    out = pl.pallas_call(
        kern,
        out_shape=jax.ShapeDtypeStruct((batch, lp, 2 * hk * d), BF16),
        grid=(batch, ng),
        in_specs=[
            pl.BlockSpec((None, lp, qb), lambda b, h: (b, 0, h)),
            pl.BlockSpec((None, lp, qb), lambda b, h: (b, 0, ng + h)),
            pl.BlockSpec((None, lp, vb), lambda b, h: (b, 0, ng + h)),
            pl.BlockSpec((None, lp, vb), lambda b, h: (b, 0, 2 * ng + h)),
            pl.BlockSpec((CONV_W, qb), lambda b, h: (0, h)),
            pl.BlockSpec((CONV_W, qb), lambda b, h: (0, ng + h)),
            pl.BlockSpec((CONV_W, vb), lambda b, h: (0, ng + h)),
            pl.BlockSpec((None, None, n_chunks, GDN_CHUNK, 2 * nv), lambda b, h: (b, h, 0, 0, 0)),
            pl.BlockSpec((None, None, n_chunks, nq, 2 * GDN_CHUNK), lambda b, h: (b, h, 0, 0, 0)),
            pl.BlockSpec((None, 1, 2 * nv), lambda b, h: (h, 0, 0)),
            pl.BlockSpec((None, 2 * nq, 2 * GDN_CHUNK), lambda b, h: (h, 0, 0)),
            pl.BlockSpec((1, d), lambda b, h: (0, 0)),
        ],
        out_specs=pl.BlockSpec((None, lp, vb), lambda b, h: (b, 0, h)),
        scratch_shapes=[
            pltpu.VMEM((nv, d, d), F32),
            pltpu.VMEM((n_chunks, nv, 2 * GDN_CHUNK, d), BF16),
            pltpu.VMEM((n_chunks, nv, GDN_CHUNK, d), F32),
            pltpu.VMEM((n_chunks, nv, GDN_CHUNK, d), F32),
            pltpu.VMEM((n_chunks, nv, GDN_CHUNK, d), BF16),
            pltpu.VMEM((n_chunks, nv, 8, d), F32),
        ],
        compiler_params=_params("parallel", "parallel"),
        name="gdn_core",
    )(qkvz, qkvz, qkvz, qkvz, conv_w, conv_w, conv_w, gcol, grow, pcol, prow, norm_w.reshape(1, d).astype(F32))
    return out.reshape(m, 2 * hk * d)


def _ssd_kernel(z_ref, x_ref, b_ref, c_ref, wx_ref, wb_ref, wc_ref, bx_ref, bb_ref, bc_ref, dcol_ref, drow_ref,
                pcol_ref, prow_ref, nw_ref, o_ref, s_ref, *, n_chunks):
    c_len = SSM_CHUNK
    hp = SSM_HEAD_DIM
    nh = SSM_HPG
    width = nh * hp
    s_ref[...] = jnp.zeros_like(s_ref)

    wx, wb, wc = wx_ref[...], wb_ref[...], wc_ref[...]
    bx, bb, bc = bx_ref[...], bb_ref[...], bc_ref[...]
    neg_a_col = -jnp.exp(pcol_ref[0:1, :])
    dtb_col = pcol_ref[1:2, :]
    d_skip = pcol_ref[2:3, :]
    neg_a_row = -jnp.exp(prow_ref[:, 0:1])
    dtb_row = prow_ref[:, 1:2]
    lower = _tri(c_len)
    lower_f = lower.astype(F32)
    upper_f = (lax.broadcasted_iota(jnp.int32, (c_len, c_len), 0)
               <= lax.broadcasted_iota(jnp.int32, (c_len, c_len), 1)).astype(F32)
    lane = lax.broadcasted_iota(jnp.int32, (1, 2 * hp), 1)

    def spread(cols):
        return jnp.concatenate([jnp.where(lane < hp, cols[:, 2 * t:2 * t + 1], cols[:, 2 * t + 1:2 * t + 2])
                                for t in range(nh // 2)], axis=1)

    d_skip_x = spread(d_skip)
    norm_w = nw_ref[...]

    def load_chunk(c):
        r0 = pl.multiple_of(c * c_len, c_len)
        valid_col = (r0 + lax.broadcasted_iota(jnp.int32, (c_len, 1), 0)) >= PAD
        valid_row = (r0 + lax.broadcasted_iota(jnp.int32, (1, c_len), 1)) >= PAD
        dt_col = jnp.where(valid_col, _softplus(dcol_ref[c] + dtb_col), 0.0)
        dt_row = jnp.where(valid_row, _softplus(drow_ref[c] + dtb_row), 0.0)
        xs = jnp.where(valid_col, _conv_silu(x_ref, wx, r0, c_len, bx), 0.0)
        return dict(
            r0=r0, valid_col=valid_col, xs=xs, xdt=xs * spread(dt_col),
            bm=jnp.where(valid_col, _conv_silu(b_ref, wb, r0, c_len, bb), 0.0),
            cm=jnp.where(valid_col, _conv_silu(c_ref, wc, r0, c_len, bc), 0.0),
            cum_col=_hdot(lower_f, dt_col * neg_a_col),
            cum_row=_hdot(dt_row * neg_a_row, upper_f),
        )

    def run_chunks(cs):
        data = [load_chunk(c) for c in cs]
        for ch in data:
            ch["cum_last"] = ch["cum_col"][c_len - 1:c_len, :]
            ch["cb"] = _bdot(ch["cm"], ch["bm"], NT)
        for ch in data:
            ch["upd"] = _bdot(ch["bm"], ch["xdt"] * spread(jnp.exp(ch["cum_last"] - ch["cum_col"])), TN)
        s = s_ref[...]
        for ch in data:
            ch["y"] = _bdot(ch["cm"], s) * spread(jnp.exp(ch["cum_col"]))
            s = s * spread(jnp.exp(ch["cum_last"])) + ch["upd"]
        s_ref[...] = s
        for ch in data:
            diag = []
            for pair in range(nh // 2):
                sc = []
                for j in (2 * pair, 2 * pair + 1):
                    lmat = jnp.exp(jnp.where(lower, ch["cum_col"][:, j:j + 1] - ch["cum_row"][j:j + 1, :], -jnp.inf))
                    sc.append((ch["cb"] * lmat).astype(BF16))
                xp = ch["xdt"][:, pair * 2 * hp:(pair + 1) * 2 * hp]
                rhs = jnp.concatenate([jnp.where(lane < hp, xp, 0.0), jnp.where(lane >= hp, xp, 0.0)], axis=0)
                diag.append(_bdot(jnp.concatenate(sc, axis=1), rhs))
            ch["diag"] = jnp.concatenate(diag, axis=1)
        for ch in data:
            y = ch["y"] + ch["diag"] + ch["xs"] * d_skip_x
            y = y * _silu(z_ref[pl.ds(ch["r0"], c_len), :].astype(F32))
            y = y * lax.rsqrt(jnp.mean(y * y, axis=-1, keepdims=True) + EPS) * norm_w
            o_ref[pl.ds(ch["r0"], c_len), :] = jnp.where(ch["valid_col"], y, 0.0).astype(o_ref.dtype)

    first = n_chunks % SSM_UNROLL
    for c in range(first):
        run_chunks([c])

    def step(it, _):
        run_chunks([first + it * SSM_UNROLL + j for j in range(SSM_UNROLL)])
        return 0

    lax.fori_loop(0, (n_chunks - first) // SSM_UNROLL, step, 0)


def ssd_core(zx, dt_raw, conv_w, conv_b, a_log, dt_bias, d_skip, norm_w, batch):
    m = zx.shape[0]
    lp = m // batch
    ng = 8
    width = SSM_HPG * SSM_HEAD_DIM
    d_inner = ng * width
    n_chunks = lp // SSM_CHUNK
    zx = zx.reshape(batch, lp, zx.shape[1])
    d5 = dt_raw.reshape(batch, n_chunks, SSM_CHUNK, ng, SSM_HPG)
    dcol = jnp.transpose(d5, (0, 3, 1, 2, 4))
    drow = jnp.transpose(d5, (0, 3, 1, 4, 2))
    p3 = jnp.stack([a_log.reshape(ng, SSM_HPG), dt_bias.reshape(ng, SSM_HPG), d_skip.reshape(ng, SSM_HPG)],
                   axis=1).astype(F32)
    prow = jnp.transpose(p3, (0, 2, 1))
    conv_b = conv_b.reshape(1, -1).astype(F32)
    xo, bo, co = d_inner // width, d_inner // SSM_STATE, (d_inner + ng * SSM_STATE) // SSM_STATE
    kern = functools.partial(_ssd_kernel, n_chunks=n_chunks)
    out = pl.pallas_call(
        kern,
        out_shape=jax.ShapeDtypeStruct((batch, lp, d_inner), BF16),
        grid=(batch, ng),
        in_specs=[
            pl.BlockSpec((None, lp, width), lambda b, g: (b, 0, g)),
            pl.BlockSpec((None, lp, width), lambda b, g: (b, 0, xo + g)),
            pl.BlockSpec((None, lp, SSM_STATE), lambda b, g: (b, 0, 2 * bo + g)),
            pl.BlockSpec((None, lp, SSM_STATE), lambda b, g: (b, 0, bo + co + g)),
            pl.BlockSpec((CONV_W, width), lambda b, g: (0, g)),
            pl.BlockSpec((CONV_W, SSM_STATE), lambda b, g: (0, bo + g)),
            pl.BlockSpec((CONV_W, SSM_STATE), lambda b, g: (0, co + g)),
            pl.BlockSpec((1, width), lambda b, g: (0, g)),
            pl.BlockSpec((1, SSM_STATE), lambda b, g: (0, bo + g)),
            pl.BlockSpec((1, SSM_STATE), lambda b, g: (0, co + g)),
            pl.BlockSpec((None, None, n_chunks, SSM_CHUNK, SSM_HPG), lambda b, g: (b, g, 0, 0, 0)),
            pl.BlockSpec((None, None, n_chunks, SSM_HPG, SSM_CHUNK), lambda b, g: (b, g, 0, 0, 0)),
            pl.BlockSpec((None, 3, SSM_HPG), lambda b, g: (g, 0, 0)),
            pl.BlockSpec((None, SSM_HPG, 3), lambda b, g: (g, 0, 0)),
            pl.BlockSpec((1, width), lambda b, g: (0, g)),
        ],
        out_specs=pl.BlockSpec((None, lp, width), lambda b, g: (b, 0, g)),
        scratch_shapes=[pltpu.VMEM((SSM_STATE, width), F32)],
        compiler_params=_params("parallel", "parallel"),
        name="ssd_core",
    )(zx, zx, zx, zx, conv_w, conv_w, conv_w, conv_b, conv_b, conv_b, dcol, drow, p3, prow,
      norm_w.reshape(1, d_inner).astype(F32))
    return out.reshape(m, d_inner)


def _rope_tables(lp):
    inv = ROPE_THETA ** (-jnp.arange(0, MLA_ROPE, 2, dtype=F32) / MLA_ROPE)
    pos = jnp.maximum(jnp.arange(lp, dtype=F32) - PAD, 0.0)
    ang = pos[:, None] * inv[None, :]
    return jnp.tile(jnp.cos(ang), (1, 4)), jnp.tile(jnp.sin(ang), (1, 4))


def _pair_head(lane):
    return lax.shift_right_logical(lane, 5) & 1


def _rope_pair(t, cos, sin):
    lane = lax.broadcasted_iota(jnp.int32, (1, t.shape[1]), 1)
    partner = pltpu.roll(t, t.shape[1] // 2, axis=1)
    return t * cos + jnp.where(lane < t.shape[1] // 2, -partner, partner) * sin


def _normed(x_ref, g_ref, xn_ref, sub):
    def body(r, _):
        r0 = pl.multiple_of(r * sub, sub)
        x = x_ref[pl.ds(r0, sub), :].astype(F32)
        ms = jnp.mean(x * x, axis=-1, keepdims=True)
        xn_ref[pl.ds(r0, sub), :] = (x * lax.rsqrt(ms + EPS) * g_ref[...]).astype(BF16)
        return 0

    lax.fori_loop(0, x_ref.shape[0] // sub, body, 0)


def _q_up_kernel(x_ref, g_ref, w_ref, gn_ref, gp_ref, cos_ref, sin_ref, o_ref, xn_ref, y_ref, *, n_heads, scale, sub):
    d = HEAD_DIM
    _normed(x_ref, g_ref, xn_ref, sub)
    y_ref[...] = jnp.dot(xn_ref[...], w_ref[...], preferred_element_type=F32)
    r = lax.broadcasted_iota(jnp.int32, (d, d), 0)
    c = lax.broadcasted_iota(jnp.int32, (d, d), 1)
    same_head = (_pair_head(r) == _pair_head(c)).astype(F32)
    base = n_heads * d

    def body(i, _):
        rows = pl.ds(pl.multiple_of(i * sub, sub), sub)
        cos, sin = cos_ref[rows, :], sin_ref[rows, :]
        for h in range(n_heads):
            cols = slice(h * d, (h + 1) * d)
            x = y_ref[rows, cols]
            y = x * lax.rsqrt(jnp.mean(x * x, axis=-1, keepdims=True) + EPS) * gn_ref[:, cols]
            o_ref[rows, cols] = (y * scale).astype(o_ref.dtype)
        for p in range(n_heads // 2):
            cols = slice(base + p * d, base + (p + 1) * d)
            x = y_ref[rows, cols]
            ms = _hdot(x * x, same_head) * (1.0 / MLA_ROPE)
            y = x * lax.rsqrt(ms + EPS) * gp_ref[:, p * d:(p + 1) * d]
            o_ref[rows, cols] = (_rope_pair(y, cos, sin) * scale).astype(o_ref.dtype)
        return 0

    lax.fori_loop(0, y_ref.shape[0] // sub, body, 0)


def _kv_up_kernel(x_ref, kpe_ref, g_ref, w_ref, gn_ref, gp_ref, cos_ref, sin_ref, k_ref, v_ref, pe_ref, xn_ref, y_ref,
                  *, n_heads, sub):
    d = HEAD_DIM
    _normed(x_ref, g_ref, xn_ref, sub)
    y_ref[...] = jnp.dot(xn_ref[...], w_ref[:, :n_heads * d], preferred_element_type=F32)
    v_ref[...] = jnp.dot(xn_ref[...], w_ref[:, n_heads * d:], preferred_element_type=F32).astype(v_ref.dtype)

    def body(i, _):
        rows = pl.ds(pl.multiple_of(i * sub, sub), sub)
        for h in range(n_heads):
            cols = slice(h * d, (h + 1) * d)
            x = y_ref[rows, cols]
            y = x * lax.rsqrt(jnp.mean(x * x, axis=-1, keepdims=True) + EPS) * gn_ref[:, cols]
            k_ref[rows, cols] = y.astype(k_ref.dtype)
        x = kpe_ref[rows, :]
        y = x * lax.rsqrt(jnp.mean(x * x, axis=-1, keepdims=True) + EPS) * gp_ref[...]
        pe_ref[rows, :] = _rope_pair(y, cos_ref[rows, :], sin_ref[rows, :]).astype(pe_ref.dtype)
        return 0

    lax.fori_loop(0, y_ref.shape[0] // sub, body, 0)


def _flash_kernel(qn_ref, qpe_ref, kn_ref, kpe_ref, v_ref, o_ref, *, kv_blk):
    d = HEAD_DIM
    blk = qn_ref.shape[0]
    lp = kn_ref.shape[0]
    i = pl.program_id(2)
    lane = lax.broadcasted_iota(jnp.int32, (1, d), 1)
    qpe = qpe_ref[...]
    qf = []
    for hh in range(2):
        mine = _pair_head(lane) == hh
        qf.append(jnp.concatenate([qn_ref[:, hh * d:(hh + 1) * d], jnp.where(mine, qpe, jnp.zeros_like(qpe))],
                                  axis=1))
    qpos = i * blk + lax.broadcasted_iota(jnp.int32, (blk, 1), 0)
    q_limit = jnp.maximum(qpos, PAD)

    def body(j, carry, masked):
        c0 = pl.multiple_of(jnp.minimum(PAD + j * kv_blk, lp - kv_blk), HALO)
        kpe = kpe_ref[pl.ds(c0, kv_blk), :]
        scores = []
        for hh in range(2):
            kf = jnp.concatenate([kn_ref[pl.ds(c0, kv_blk), hh * d:(hh + 1) * d], kpe], axis=1)
            scores.append(lax.dot_general(qf[hh], kf, NT, preferred_element_type=F32))
        if masked:
            kpos = c0 + lax.broadcasted_iota(jnp.int32, (1, kv_blk), 1)
            ok = (kpos >= PAD + j * kv_blk) & (kpos <= q_limit)
            scores = [jnp.where(ok, s, -1e30) for s in scores]
        stats = []
        for hh in range(2):
            m, l, _ = carry[hh]
            s = scores[hh]
            m_new = jnp.maximum(m, jnp.max(s, axis=-1, keepdims=True))
            p = jnp.exp(s - m_new)
            alpha = jnp.exp(m - m_new)
            stats.append((m_new, alpha * l + jnp.sum(p, axis=-1, keepdims=True), alpha, p.astype(BF16)))
        new = []
        for hh in range(2):
            m_new, l, alpha, p = stats[hh]
            pv = jnp.dot(p, v_ref[pl.ds(c0, kv_blk), hh * d:(hh + 1) * d], preferred_element_type=F32)
            new.append((m_new, l, alpha * carry[hh][2] + pv))
        return tuple(new)

    init = tuple((jnp.full((blk, 1), -1e30, F32), jnp.zeros((blk, 1), F32), jnp.zeros((blk, d), F32))
                 for _ in range(2))
    n_free = lax.div(jnp.maximum(i * blk + 1 - PAD, 0), kv_blk)
    n_kv = lax.div(i * blk + blk - PAD + kv_blk - 1, kv_blk)
    res = lax.fori_loop(0, n_free, functools.partial(body, masked=False), init)
    res = lax.fori_loop(n_free, n_kv, functools.partial(body, masked=True), res)
    out = jnp.concatenate([acc / l for (_, l, acc) in res], axis=1)
    o_ref[...] = jnp.where(qpos >= PAD, out, 0.0).astype(o_ref.dtype)


def mla_attention(lat, q_col_block, kv_col_block, kpe_col_block, norm_q_lat, norm_kv_lat, wq, wkv, q_norm, k_norm, batch):
    m = lat.shape[0]
    lp = m // batch
    nh = 16
    d = HEAD_DIM
    half = MLA_ROPE // 2
    scale = (d + MLA_ROPE) ** -0.5
    cos, sin = _rope_tables(lp)
    qn_gain = jnp.tile(q_norm[:d], nh).reshape(1, nh * d).astype(F32)
    kn_gain = jnp.tile(k_norm[:d], nh).reshape(1, nh * d).astype(F32)
    qf, qs = q_norm[d:d + half], q_norm[d + half:]
    kf, ks = k_norm[d:d + half], k_norm[d + half:]
    qp_gain = jnp.tile(jnp.concatenate([qf, qf, qs, qs]), nh // 2).reshape(1, nh // 2 * d).astype(F32)
    kp_gain = jnp.concatenate([kf, kf, ks, ks]).reshape(1, d).astype(F32)
    tm = next(t for t in range(min(lp, MAX_ROW_TILE) // 64 * 64, 0, -64) if lp % t == 0)
    per_seq = lp // tm
    q_lora, qw = wq.shape
    kv_lora, kvw = wkv.shape
    whole = lambda i: (0, 0)
    pos = lambda i: (i % per_seq, 0)

    q_prep = pl.pallas_call(
        functools.partial(_q_up_kernel, n_heads=nh, scale=scale, sub=64),
        out_shape=jax.ShapeDtypeStruct((m, qw), BF16),
        grid=(m // tm,),
        in_specs=[
            pl.BlockSpec((tm, q_lora), lambda i: (i, q_col_block)),
            pl.BlockSpec((1, q_lora), whole),
            pl.BlockSpec((q_lora, qw), whole),
            pl.BlockSpec((1, nh * d), whole),
            pl.BlockSpec((1, nh // 2 * d), whole),
            pl.BlockSpec((tm, d), pos),
            pl.BlockSpec((tm, d), pos),
        ],
        out_specs=pl.BlockSpec((tm, qw), lambda i: (i, 0)),
        scratch_shapes=[pltpu.VMEM((tm, q_lora), BF16), pltpu.VMEM((tm, qw), F32)],
        compiler_params=_params("parallel"),
        name="mla_q_up",
    )(lat, norm_q_lat.reshape(1, q_lora).astype(F32), wq, qn_gain, qp_gain, cos, sin).reshape(batch, lp, qw)

    k_prep, v_prep, pe_prep = pl.pallas_call(
        functools.partial(_kv_up_kernel, n_heads=nh, sub=64),
        out_shape=(jax.ShapeDtypeStruct((m, nh * d), BF16),
                   jax.ShapeDtypeStruct((m, nh * d), BF16),
                   jax.ShapeDtypeStruct((m, d), BF16)),
        grid=(m // tm,),
        in_specs=[
            pl.BlockSpec((tm, kv_lora), lambda i: (i, kv_col_block)),
            pl.BlockSpec((tm, d), lambda i: (i, kpe_col_block)),
            pl.BlockSpec((1, kv_lora), whole),
            pl.BlockSpec((kv_lora, kvw), whole),
            pl.BlockSpec((1, nh * d), whole),
            pl.BlockSpec((1, d), whole),
            pl.BlockSpec((tm, d), pos),
            pl.BlockSpec((tm, d), pos),
        ],
        out_specs=(pl.BlockSpec((tm, nh * d), lambda i: (i, 0)),
                   pl.BlockSpec((tm, nh * d), lambda i: (i, 0)),
                   pl.BlockSpec((tm, d), lambda i: (i, 0))),
        scratch_shapes=[pltpu.VMEM((tm, kv_lora), BF16), pltpu.VMEM((tm, nh * d), F32)],
        compiler_params=_params("parallel"),
        name="mla_kv_up",
    )(lat, lat, norm_kv_lat.reshape(1, kv_lora).astype(F32), wkv, kn_gain, kp_gain, cos, sin)
    k_prep, v_prep, pe_prep = (t.reshape(batch, lp, t.shape[1]) for t in (k_prep, v_prep, pe_prep))

    np_ = nh // 2
    tq = next(t for t in range(min(lp, ATT_MAX_Q_ROWS) // 16 * 16, 0, -16) if lp % t == 0)
    out = pl.pallas_call(
        functools.partial(_flash_kernel, kv_blk=min(ATT_KV_BLK, lp)),
        out_shape=jax.ShapeDtypeStruct((batch, lp, nh * d), BF16),
        grid=(batch, np_, lp // tq),
        in_specs=[
            pl.BlockSpec((None, tq, 2 * d), lambda b, p, i: (b, i, p)),
            pl.BlockSpec((None, tq, d), lambda b, p, i: (b, i, nh + p)),
            pl.BlockSpec((None, lp, 2 * d), lambda b, p, i: (b, 0, p)),
            pl.BlockSpec((None, lp, d), lambda b, p, i: (b, 0, 0)),
            pl.BlockSpec((None, lp, 2 * d), lambda b, p, i: (b, 0, p)),
        ],
        out_specs=pl.BlockSpec((None, tq, 2 * d), lambda b, p, i: (b, i, p)),
        compiler_params=_params("parallel", "parallel", "arbitrary"),
        name="mla_flash",
    )(q_prep, q_prep, k_prep, pe_prep, v_prep)
    return out.reshape(m, nh * d)


def _gdn_layer(stream, next_gain, w_in_all, layer, conv_w, a_log, dt_bias, norm_w, w_out_all, batch):
    h, hb, ss = stream
    n_gate = 2 * a_log.shape[0]
    n_big = w_in_all.shape[2] - n_gate
    qkvz = scaled_matmul(hb, ss, w_in_all, layer=layer, n=n_big, name="gdn_in_proj")
    ba = scaled_matmul(hb, ss, w_in_all[layer, :, n_big:], out_dtype=F32, name="gdn_gate_proj")
    o = gdn_core(qkvz, ba, conv_w.astype(F32), a_log, dt_bias, norm_w, batch)
    return matmul_residual(o, w_out_all, h, next_gain, layer=layer, name="gdn_out_proj")


def _mla_layer(stream, next_gain, w_in, norm_q_lat, norm_kv_lat, w_uq, w_ukv, q_norm, k_norm, w_out_all, layer, batch):
    h, hb, ss = stream
    nh, d, r = 16, HEAD_DIM, MLA_ROPE
    q_lora, kv_lora = norm_q_lat.shape[0], norm_kv_lat.shape[0]
    half = r // 2
    w_cq, w_ckv, w_pe = w_in[:, :q_lora], w_in[:, q_lora:q_lora + kv_lora], w_in[:, q_lora + kv_lora:]
    w_pe_t = jnp.concatenate([w_pe[:, :half], w_pe[:, :half], w_pe[:, half:], w_pe[:, half:]], axis=1)
    gap = (-(kv_lora + d)) % q_lora
    w_lat = jnp.concatenate([w_ckv, w_pe_t, jnp.zeros((w_in.shape[0], gap), w_in.dtype), w_cq], axis=1)
    lat = scaled_matmul(hb, ss, w_lat, out_dtype=F32, name="mla_in_proj")
    wq = w_uq.reshape(q_lora, nh, d + r)
    wq_pe = wq[:, :, d:].reshape(q_lora, nh // 2, 2, 2, half)
    wq_pe = jnp.transpose(wq_pe, (0, 1, 3, 2, 4)).reshape(q_lora, nh // 2 * d)
    wq_all = jnp.concatenate([wq[:, :, :d].reshape(q_lora, nh * d), wq_pe], axis=1)
    wkv = w_ukv.reshape(kv_lora, nh, 2 * d)
    wkv_all = jnp.concatenate([wkv[:, :, :d].reshape(kv_lora, nh * d), wkv[:, :, d:].reshape(kv_lora, nh * d)],
                              axis=1)
    o = mla_attention(lat, (kv_lora + d + gap) // q_lora, 0, kv_lora // d, norm_q_lat, norm_kv_lat,
                      wq_all.astype(BF16), wkv_all.astype(BF16), q_norm, k_norm, batch)
    return matmul_residual(o, w_out_all, h, next_gain, layer=layer, name="mla_out_proj")


def _ssm_layer(stream, next_gain, w_in_all, layer, conv_w, conv_b, a_log, dt_bias, d_skip, norm_w, w_out_all, batch):
    h, hb, ss = stream
    n_dt = a_log.shape[0]
    n_big = w_in_all.shape[2] - n_dt
    zx = scaled_matmul(hb, ss, w_in_all, layer=layer, n=n_big, name="ssm_in_proj")
    dt_raw = scaled_matmul(hb, ss, w_in_all[layer, :, n_big:], out_dtype=F32, name="ssm_dt_proj")
    y = ssd_core(zx, dt_raw, conv_w.astype(F32), conv_b, a_log, dt_bias, d_skip, norm_w, batch)
    return matmul_residual(y, w_out_all, h, next_gain, layer=layer, name="ssm_out_proj")


def _mlp_layer(stream, next_gain, w_up_all, w_down_all, layer):
    h, hb, ss = stream
    act = scaled_matmul(hb, ss, w_up_all, layer=layer, act="relu2", name="mlp_up")
    return matmul_residual(act, w_down_all, h, next_gain, layer=layer, name="mlp_down")


def kernel(x, meta_tokens, norm_mix, norm_mlp, mlp_w_up, mlp_w_down, gdn_w_in, gdn_conv_w, gdn_a_log, gdn_dt_bias, gdn_norm, gdn_w_out, mla_w_in, mla_norm_q_lat, mla_norm_kv_lat, mla_w_uq, mla_w_ukv, mla_q_norm, mla_k_norm, mla_w_out, ssm_w_in, ssm_conv_w, ssm_conv_b, ssm_a_log, ssm_dt_bias, ssm_d, ssm_norm, ssm_w_out):
    batch, seq, dm = x.shape
    depth = norm_mix.shape[0]
    lp = ROW_TILE + seq
    meta = jnp.broadcast_to(meta_tokens.astype(x.dtype)[None], (batch, N_META, dm))
    h = jnp.concatenate([jnp.zeros((batch, PAD, dm), x.dtype), meta, x], axis=1).reshape(batch * lp, dm)
    stream = (h,) + tuple(stream_entry(h, norm_mix[0]))
    down_b, gdn_out_b, ssm_out_b, mla_out_b = (w.astype(BF16) for w in (mlp_w_down, gdn_w_out, ssm_w_out, mla_w_out))
    gdn_in_b, ssm_in_b = gdn_w_in.astype(BF16), ssm_w_in.astype(BF16)
    ia = ib = ic = 0
    for i in range(depth):
        kind = i % 3
        if kind == 0:
            stream = _gdn_layer(stream, norm_mlp[i], gdn_in_b, ia, gdn_conv_w[ia], gdn_a_log[ia], gdn_dt_bias[ia],
                                gdn_norm[ia], gdn_out_b, batch)
            ia += 1
        elif kind == 1:
            stream = _mla_layer(stream, norm_mlp[i], mla_w_in[ib], mla_norm_q_lat[ib], mla_norm_kv_lat[ib],
                                mla_w_uq[ib], mla_w_ukv[ib], mla_q_norm[ib], mla_k_norm[ib], mla_out_b, ib, batch)
            ib += 1
        else:
            stream = _ssm_layer(stream, norm_mlp[i], ssm_in_b, ic, ssm_conv_w[ic], ssm_conv_b[ic], ssm_a_log[ic],
                                ssm_dt_bias[ic], ssm_d[ic], ssm_norm[ic], ssm_out_b, batch)
            ic += 1
        following = norm_mix[i + 1] if i + 1 < depth else jnp.ones_like(norm_mlp[i])
        stream = _mlp_layer(stream, following, mlp_w_up, down_b, i)
    return stream[0].reshape(batch, lp, dm)[:, ROW_TILE:]
```

```python
import functools
import math

import jax
import jax.numpy as jnp
from jax import lax
from jax.experimental import pallas as pl
from jax.experimental.pallas import tpu as pltpu

F32 = jnp.float32
BF16 = jnp.bfloat16
HI = lax.Precision.HIGHEST

EPS = 1e-6
N_META = 16
ROW_TILE = 128
PAD = ROW_TILE - N_META
LANES = 128
HALO = 16
CONV_W = 4

HEAD_DIM = 128
GDN_CHUNK = 64
SSM_CHUNK = 128
SSM_HEAD_DIM = 64
SSM_HPG = 8
SSM_STATE = 128
MLA_ROPE = 64
ROPE_THETA = 10000.0
ATT_MAX_Q_ROWS = 320
ATT_KV_BLK = 512
GDN_QK_PER_STEP = 2
SSM_UNROLL = 4
GDN_MAX_UNROLL = 3

VMEM_LIMIT = 56 * 1024 * 1024
MAX_ROW_TILE = 1088
MAX_K_TILE = 2048
NT = (((1,), (1,)), ((), ()))
TN = (((0,), (0,)), ((), ()))


def _row_tile(m, cap=MAX_ROW_TILE):
    for t in range(min(m, cap) // 64 * 64, 0, -64):
        if m % t == 0:
            return t
    raise ValueError(f"row count {m} has no tile that is a multiple of 64")


def _col_tile(n, cap=1024):
    if n <= cap:
        return n
    for t in range(cap, 0, -128):
        if n % t == 0:
            return t
    raise ValueError(f"column count {n} has no tile that is a multiple of 128")


def _params(*sem):
    return pltpu.CompilerParams(dimension_semantics=sem, vmem_limit_bytes=VMEM_LIMIT)


def _bdot(a, b, dims=None):
    a = a.astype(BF16)
    b = b.astype(BF16)
    if dims is None:
        return jnp.dot(a, b, preferred_element_type=F32)
    return lax.dot_general(a, b, dims, preferred_element_type=F32)


def _hdot(a, b):
    return jnp.dot(a, b, precision=HI, preferred_element_type=F32)


def _sigmoid(x):
    return 0.5 + 0.5 * jnp.tanh(0.5 * x)


def _silu(x):
    return x * _sigmoid(x)


def _softplus(x):
    return jnp.maximum(x, 0.0) + jnp.log(1.0 + jnp.exp(-jnp.abs(x)))


def _stats_rows(tm, cap=272):
    return next(t for t in range(min(tm, cap) // 16 * 16, 0, -16) if tm % t == 0)


def _stream_stats(h_ref, g_ref, hb_ref, ss_ref, sub):
    def body(r, _):
        r0 = pl.multiple_of(r * sub, sub)
        x = h_ref[pl.ds(r0, sub), :]
        hb_ref[pl.ds(r0, sub), :] = (x * g_ref[...]).astype(BF16)
        ss_ref[pl.ds(r0, sub), :] = jnp.broadcast_to(jnp.sum(x * x, axis=-1, keepdims=True), (sub, ss_ref.shape[1]))
        return 0

    lax.fori_loop(0, h_ref.shape[0] // sub, body, 0)


def _matmul_residual_kernel(a_ref, w_ref, h_ref, g_ref, o_ref, hb_ref, ss_ref, *, sub):
    @pl.when(pl.program_id(2) == 0)
    def _():
        o_ref[...] = h_ref[...]

    last = pl.program_id(2) == pl.num_programs(2) - 1

    @pl.when(jnp.logical_not(last))
    def _():
        o_ref[...] += jnp.dot(a_ref[...], w_ref[...], preferred_element_type=F32)

    @pl.when(last)
    def _():
        new = o_ref[...] + jnp.dot(a_ref[...], w_ref[...], preferred_element_type=F32)
        o_ref[...] = new
        hb_ref[...] = (new * g_ref[...]).astype(BF16)
        ss_ref[...] = jnp.broadcast_to(jnp.sum(new * new, axis=-1, keepdims=True), ss_ref.shape)


def _stream_out(m, n, tm, tn, index):
    shapes = (jax.ShapeDtypeStruct((m, n), F32), jax.ShapeDtypeStruct((m, n), BF16),
              jax.ShapeDtypeStruct((n // tn, m, LANES), F32))
    specs = (pl.BlockSpec((tm, tn), lambda *g: index(*g)),
             pl.BlockSpec((tm, tn), lambda *g: index(*g)),
             pl.BlockSpec((None, tm, LANES), lambda *g: (index(*g)[1], index(*g)[0], 0)))
    return shapes, specs


def matmul_residual(a, w, h, next_gain, *, layer=0, name):
    if w.ndim == 2:
        w = w[None]
    m, k = a.shape
    n = w.shape[2]
    tm, tn, tk = _row_tile(m), _col_tile(n), _col_tile(k, MAX_K_TILE)
    shapes, specs = _stream_out(m, n, tm, tn, lambda i, j, kk: (i, j))
    return pl.pallas_call(
        functools.partial(_matmul_residual_kernel, sub=_stats_rows(tm)),
        out_shape=shapes,
        grid=(m // tm, n // tn, k // tk),
        in_specs=[
            pl.BlockSpec((tm, tk), lambda i, j, kk: (i, kk)),
            pl.BlockSpec((None, tk, tn), lambda i, j, kk: (layer, kk, j)),
            pl.BlockSpec((tm, tn), lambda i, j, kk: (i, j)),
            pl.BlockSpec((1, tn), lambda i, j, kk: (0, j)),
        ],
        out_specs=specs,
        compiler_params=_params("parallel", "parallel", "arbitrary"),
        name=name,
    )(a, w, h, next_gain.reshape(1, n).astype(F32))


def _stream_entry_kernel(h_ref, g_ref, hb_ref, ss_ref, *, sub):
    _stream_stats(h_ref, g_ref, hb_ref, ss_ref, sub)


def stream_entry(h, next_gain):
    m, n = h.shape
    tm, tn = _row_tile(m), _col_tile(n)
    shapes, specs = _stream_out(m, n, tm, tn, lambda i, j: (i, j))
    return pl.pallas_call(
        functools.partial(_stream_entry_kernel, sub=_stats_rows(tm)),
        out_shape=shapes[1:],
        grid=(m // tm, n // tn),
        in_specs=[pl.BlockSpec((tm, tn), lambda i, j: (i, j)), pl.BlockSpec((1, tn), lambda i, j: (0, j))],
        out_specs=specs[1:],
        compiler_params=_params("parallel", "parallel"),
        name="stream_entry",
    )(h, next_gain.reshape(1, n).astype(F32))


def _scaled_matmul_kernel(x_ref, ss_ref, w_ref, o_ref, *scratch, act, inv_k, sub):
    if scratch:
        wb_ref, = scratch

        @pl.when(pl.program_id(1) == 0)
        def _():
            def body(r, _):
                r0 = pl.multiple_of(r * sub, sub)
                wb_ref[pl.ds(r0, sub), :] = w_ref[pl.ds(r0, sub), :].astype(BF16)
                return 0

            lax.fori_loop(0, w_ref.shape[0] // sub, body, 0)
    else:
        wb_ref = w_ref

    ss = ss_ref[0]
    for t in range(1, ss_ref.shape[0]):
        ss = ss + ss_ref[t]
    y = jnp.dot(x_ref[...], wb_ref[...], preferred_element_type=F32) * lax.rsqrt(ss[:, 0:1] * inv_k + EPS)
    if act == "relu2":
        y = jnp.square(jnp.maximum(y, 0.0))
    o_ref[...] = y.astype(o_ref.dtype)


def scaled_matmul(hb, ss, w, *, layer=0, n=None, act=None, out_dtype=BF16, name):
    if w.ndim == 2:
        w = w[None]
    m, k = hb.shape
    n = w.shape[2] if n is None else n
    tm, tn = _row_tile(m), _col_tile(n)
    return pl.pallas_call(
        functools.partial(_scaled_matmul_kernel, act=act, inv_k=1.0 / k, sub=256),
        out_shape=jax.ShapeDtypeStruct((m, n), out_dtype),
        grid=(n // tn, m // tm),
        in_specs=[
            pl.BlockSpec((tm, k), lambda j, i: (i, 0)),
            pl.BlockSpec((ss.shape[0], tm, LANES), lambda j, i: (0, i, 0)),
            pl.BlockSpec((None, k, tn), lambda j, i: (layer, 0, j)),
        ],
        out_specs=pl.BlockSpec((tm, tn), lambda j, i: (i, j)),
        scratch_shapes=[] if w.dtype == BF16 else [pltpu.VMEM((k, tn), BF16)],
        compiler_params=_params("arbitrary", "arbitrary"),
        name=name,
    )(hb, ss, w)


def _conv_silu(ref, w, r0, rows, bias=None):
    cur = ref[pl.ds(r0, rows), :].astype(F32)
    halo = ref[pl.ds(pl.multiple_of(jnp.maximum(r0 - HALO, 0), HALO), HALO), :].astype(F32)
    x = jnp.concatenate([halo, cur], axis=0)
    y = cur * w[CONV_W - 1:CONV_W, :]
    for j in range(CONV_W - 1):
        y = y + pltpu.roll(x, CONV_W - 1 - j, axis=0)[HALO:, :] * w[j:j + 1, :]
    if bias is not None:
        y = y + bias
    return _silu(y)


def _tri(n, strict=False):
    r = lax.broadcasted_iota(jnp.int32, (n, n), 0)
    c = lax.broadcasted_iota(jnp.int32, (n, n), 1)
    return (r > c) if strict else (r >= c)


def _split_bf16(x):
    hi = x.astype(BF16)
    return hi, (x - hi.astype(F32)).astype(BF16)


def _pair_blockdiag(x, first):
    z = jnp.zeros_like(x)
    return jnp.concatenate([jnp.where(first, x, z), jnp.where(first, z, x)], axis=0)


def _pair_matmul3(a_parts, b_parts, first):
    a_hi, a_lo = a_parts
    bh = _pair_blockdiag(b_parts[0], first)
    bl = _pair_blockdiag(b_parts[1], first)
    lhs = jnp.concatenate([a_hi, a_lo], axis=1)
    rhs = jnp.concatenate([jnp.concatenate([bh, bl], axis=1),
                           jnp.concatenate([bh, jnp.zeros_like(bl)], axis=1)], axis=0)
    r = jnp.dot(lhs, rhs, preferred_element_type=F32)
    w = a_hi.shape[1]
    return r[:, :w] + r[:, w:]


def _pair_unit_lower_inverses(mats, n, first, between):
    row = lax.broadcasted_iota(jnp.int32, (n, 2 * n), 0)
    col = lax.broadcasted_iota(jnp.int32, (n, 2 * n), 1) & (n - 1)
    eye = (row == col).astype(F32)
    ps = [-a for a in mats]
    ts = [eye + p for p in ps]
    splits = [_split_bf16(p) for p in ps]
    ps = [_pair_matmul3(s, s, first) for s in splits]
    between()
    k = 2
    while k < n // 2:
        splits = [_split_bf16(p) for p in ps]
        stacked = []
        for t, s in zip(ts, splits):
            t_hi, t_lo = _split_bf16(t)
            lhs = (jnp.concatenate([t_hi, s[0]], axis=0), jnp.concatenate([t_lo, s[1]], axis=0))
            stacked.append(_pair_matmul3(lhs, s, first))
        ts = [t + r[:n] for t, r in zip(ts, stacked)]
        ps = [r[n:] for r in stacked]
        between()
        k *= 2
    return [t + _pair_matmul3(_split_bf16(t), _split_bf16(p), first) for t, p in zip(ts, ps)]


def _gdn_kernel(q_ref, k_ref, v_ref, z_ref, wq_ref, wk_ref, wv_ref, gcol_ref, grow_ref, pcol_ref, prow_ref,
                nw_ref, o_ref, s_ref, wq_s, u_s, oi_s, kd_s, eg_s, *, n_chunks, first_chunk, unroll):
    c_len = GDN_CHUNK
    d = HEAD_DIM
    nq = GDN_QK_PER_STEP
    nv = 2 * nq
    wq, wk, wv = wq_ref[...], wk_ref[...], wv_ref[...]
    neg_a_col = -jnp.exp(pcol_ref[:, 0:nv])
    dtb_col = pcol_ref[:, nv:2 * nv]
    neg_a_pair = -jnp.exp(prow_ref[0:nq, :])
    dtb_pair = prow_ref[nq:2 * nq, :]
    pair_row = lax.broadcasted_iota(jnp.int32, (c_len, 2 * c_len), 0)
    pair_lane = lax.broadcasted_iota(jnp.int32, (c_len, 2 * c_len), 1)
    pair_col = pair_lane & (c_len - 1)
    first = pair_lane < c_len
    lower_pair = pair_row >= pair_col
    strict_pair = pair_row > pair_col
    lower_f = _tri(c_len).astype(F32)
    r2 = lax.broadcasted_iota(jnp.int32, (2 * c_len, 2 * c_len), 0)
    c2 = lax.broadcasted_iota(jnp.int32, (2 * c_len, 2 * c_len), 1)
    upper_pair = (((r2 < c_len) == (c2 < c_len)) & ((r2 & (c_len - 1)) <= (c2 & (c_len - 1)))).astype(F32)
    norm_w = nw_ref[...]

    def load_chunk(c):
        r0 = pl.multiple_of(c * c_len, c_len)
        valid_col = (r0 + lax.broadcasted_iota(jnp.int32, (c_len, 1), 0)) >= PAD
        valid_pair = (r0 + pair_col[0:1, :]) >= PAD
        gc = gcol_ref[c]
        g_col = jnp.where(valid_col, neg_a_col * _softplus(gc[:, nv:2 * nv] + dtb_col), 0.0)
        g_pair = jnp.where(valid_pair, neg_a_pair * _softplus(grow_ref[c] + dtb_pair), 0.0)
        return dict(
            c=c,
            q=jnp.where(valid_col, _conv_silu(q_ref, wq, r0, c_len), 0.0),
            k=jnp.where(valid_col, _conv_silu(k_ref, wk, r0, c_len), 0.0),
            v=jnp.where(valid_col, _conv_silu(v_ref, wv, r0, c_len), 0.0),
            beta=jnp.where(valid_col, _sigmoid(gc[:, 0:nv]), 0.0),
            cum_col=_hdot(lower_f, g_col),
            cum_pair=_hdot(g_pair, upper_pair),
        )

    def prepare_group(c_first, between):
        chunks = [load_chunk(c_first + j) for j in range(unroll)]
        probs = [dict(ch=ch, a=a) for ch in chunks for a in range(nq)]
        for pr in probs:
            ch, a = pr["ch"], pr["a"]
            q = ch["q"][:, a * d:(a + 1) * d]
            k = ch["k"][:, a * d:(a + 1) * d]
            pr["qn"] = q * lax.rsqrt(jnp.sum(q * q, axis=-1, keepdims=True) + EPS) * (d ** -0.5)
            pr["kn"] = k * lax.rsqrt(jnp.sum(k * k, axis=-1, keepdims=True) + EPS)
        for pr in probs:
            kn_b = pr["kn"].astype(BF16)
            both = lax.dot_general(jnp.concatenate([kn_b, pr["qn"].astype(BF16)], axis=0),
                                   jnp.concatenate([kn_b, kn_b], axis=0), NT, preferred_element_type=F32)
            pr["kk"], pr["qk"] = both[:c_len], both[c_len:]
        between()
        for pr in probs:
            ch, a = pr["ch"], pr["a"]
            h0, h1 = 2 * a, 2 * a + 1
            gcol = jnp.where(first, ch["cum_col"][:, h0:h0 + 1], ch["cum_col"][:, h1:h1 + 1])
            bcol = jnp.where(first, ch["beta"][:, h0:h0 + 1], ch["beta"][:, h1:h1 + 1])
            pr["decay"] = jnp.exp(jnp.where(lower_pair, gcol - ch["cum_pair"][a:a + 1, :], -jnp.inf))
            pr["amat"] = jnp.where(strict_pair, pr["kk"] * bcol * pr["decay"], 0.0)
        ts = _pair_unit_lower_inverses([pr["amat"] for pr in probs], c_len, first, between)
        for pr, t in zip(probs, ts):
            ch, a = pr["ch"], pr["a"]
            rhs = []
            for vh in (2 * a, 2 * a + 1):
                b1 = ch["beta"][:, vh:vh + 1]
                rhs.append(jnp.concatenate([ch["v"][:, vh * d:(vh + 1) * d] * b1,
                                            pr["kn"] * (b1 * jnp.exp(ch["cum_col"][:, vh:vh + 1]))], axis=1))
            zero = jnp.zeros_like(rhs[0])
            pr["uw"] = _bdot(t, jnp.concatenate([jnp.concatenate([rhs[0], zero], axis=1),
                                                 jnp.concatenate([zero, rhs[1]], axis=1)], axis=0))
        for pr in probs:
            uw = pr["uw"]
            zero = jnp.zeros_like(uw[:, :2 * d])
            pr["ow"] = _bdot(pr["qk"] * pr["decay"], jnp.concatenate(
                [jnp.concatenate([uw[:, :2 * d], zero], axis=1),
                 jnp.concatenate([zero, uw[:, 2 * d:]], axis=1)], axis=0))
        for pr in probs:
            ch, a, uw, ow = pr["ch"], pr["a"], pr["uw"], pr["ow"]
            c = ch["c"]
            for i, vh in enumerate((2 * a, 2 * a + 1)):
                gcol1 = ch["cum_col"][:, vh:vh + 1]
                g_last = ch["cum_col"][c_len - 1:c_len, vh:vh + 1]
                u, w = uw[:, 2 * i * d:(2 * i + 1) * d], uw[:, (2 * i + 1) * d:(2 * i + 2) * d]
                oi, qw = ow[:, 2 * i * d:(2 * i + 1) * d], ow[:, (2 * i + 1) * d:(2 * i + 2) * d]
                wq_s[c, vh, pl.ds(0, c_len), :] = w.astype(BF16)
                wq_s[c, vh, pl.ds(c_len, c_len), :] = (pr["qn"] * jnp.exp(gcol1) - qw).astype(BF16)
                u_s[c, vh] = u
                oi_s[c, vh] = oi
                kd_s[c, vh] = (pr["kn"] * jnp.exp(g_last - gcol1)).astype(BF16)
                eg_s[c, vh] = jnp.broadcast_to(jnp.exp(g_last), (8, d))

    def recur_group(c_first):
        heads = range(nv)
        state = dict(s=[s_ref[vh] for vh in heads])

        def products(c):
            state["sq"] = [jnp.dot(wq_s[c, vh], state["s"][vh].astype(BF16), preferred_element_type=F32)
                           for vh in heads]

        def update(c, last):
            r0 = pl.multiple_of(c * c_len, c_len)
            valid_col = (r0 + lax.broadcasted_iota(jnp.int32, (c_len, 1), 0)) >= PAD
            z_all = z_ref[pl.ds(r0, c_len), :].astype(F32)
            sq = state["sq"]
            upd = [_bdot(kd_s[c, vh], u_s[c, vh] - sq[vh][:c_len], TN) for vh in heads]
            state["s"] = [state["s"][vh] * eg_s[c, vh][0:1, :] + upd[vh] for vh in heads]
            outs = []
            for vh in heads:
                o = oi_s[c, vh] + sq[vh][c_len:]
                o = o * lax.rsqrt(jnp.mean(o * o, axis=-1, keepdims=True) + EPS) * norm_w
                outs.append(o * _silu(z_all[:, vh * d:(vh + 1) * d]))
            out = jnp.where(valid_col, jnp.concatenate(outs, axis=1), 0.0)
            o_ref[pl.ds(r0, c_len), :] = out.astype(o_ref.dtype)
            if last:
                for vh in heads:
                    s_ref[vh] = state["s"][vh]

        steps = []
        for j in range(unroll):
            steps.append(functools.partial(products, c_first + j))
            steps.append(functools.partial(update, c_first + j, j == unroll - 1))
        return steps

    def run_between(steps):
        pending = list(steps)

        def between():
            if pending:
                pending.pop(0)()
        return between, pending

    if first_chunk:
        o_ref[pl.ds(0, first_chunk * c_len), :] = jnp.zeros((first_chunk * c_len, nv * d), o_ref.dtype)
    s_ref[...] = jnp.zeros_like(s_ref)
    n_groups = (n_chunks - first_chunk) // unroll

    prepare_group(first_chunk, lambda: None)

    def group_step(g, _):
        between, pending = run_between(recur_group(first_chunk + (g - 1) * unroll))
        prepare_group(first_chunk + g * unroll, between)
        for step in pending:
            step()
        return 0

    lax.fori_loop(1, n_groups, group_step, 0)
    for step in recur_group(first_chunk + (n_groups - 1) * unroll):
        step()


def gdn_core(qkvz, ba, conv_w, a_log, dt_bias, norm_w, batch):
    m = qkvz.shape[0]
    lp = m // batch
    hk = 16
    d = HEAD_DIM
    nq = GDN_QK_PER_STEP
    nv = 2 * nq
    ng = hk // nq
    n_chunks = lp // GDN_CHUNK
    first_chunk, unroll = next((f, u) for u in range(GDN_MAX_UNROLL, 0, -1) for f in (1, 0)
                               if (n_chunks - f) % u == 0)
    qkvz = qkvz.reshape(batch, lp, qkvz.shape[1])
    b_log = ba[:, :2 * hk].reshape(batch, n_chunks, GDN_CHUNK, ng, nv)
    a_log_t = ba[:, 2 * hk:].reshape(batch, n_chunks, GDN_CHUNK, ng, nq, 2)
    gcol = jnp.transpose(jnp.concatenate([b_log, a_log_t.reshape(b_log.shape)], axis=-1), (0, 3, 1, 2, 4))
    grow = jnp.transpose(a_log_t, (0, 3, 1, 4, 5, 2)).reshape(batch, ng, n_chunks, nq, 2 * GDN_CHUNK)
    p = jnp.concatenate([a_log.reshape(ng, nv), dt_bias.reshape(ng, nv)], axis=-1).astype(F32)
    pcol = p.reshape(ng, 1, 2 * nv)
    prow = jnp.concatenate([jnp.repeat(a_log.reshape(ng, nq, 2), GDN_CHUNK, axis=-1),
                            jnp.repeat(dt_bias.reshape(ng, nq, 2), GDN_CHUNK, axis=-1)], axis=1).astype(F32)
    qb, vb = nq * d, nv * d
    kern = functools.partial(_gdn_kernel, n_chunks=n_chunks, first_chunk=first_chunk, unroll=unroll)
    out = pl.pallas_call(
        kern,
        out_shape=jax.ShapeDtypeStruct((batch, lp, 2 * hk * d), BF16),
        grid=(batch, ng),
        in_specs=[
            pl.BlockSpec((None, lp, qb), lambda b, h: (b, 0, h)),
            pl.BlockSpec((None, lp, qb), lambda b, h: (b, 0, ng + h)),
            pl.BlockSpec((None, lp, vb), lambda b, h: (b, 0, ng + h)),
            pl.BlockSpec((None, lp, vb), lambda b, h: (b, 0, 2 * ng + h)),
            pl.BlockSpec((CONV_W, qb), lambda b, h: (0, h)),
            pl.BlockSpec((CONV_W, qb), lambda b, h: (0, ng + h)),
            pl.BlockSpec((CONV_W, vb), lambda b, h: (0, ng + h)),
            pl.BlockSpec((None, None, n_chunks, GDN_CHUNK, 2 * nv), lambda b, h: (b, h, 0, 0, 0)),
            pl.BlockSpec((None, None, n_chunks, nq, 2 * GDN_CHUNK), lambda b, h: (b, h, 0, 0, 0)),
            pl.BlockSpec((None, 1, 2 * nv), lambda b, h: (h, 0, 0)),
            pl.BlockSpec((None, 2 * nq, 2 * GDN_CHUNK), lambda b, h: (h, 0, 0)),
            pl.BlockSpec((1, d), lambda b, h: (0, 0)),
        ],
        out_specs=pl.BlockSpec((None, lp, vb), lambda b, h: (b, 0, h)),
        scratch_shapes=[
            pltpu.VMEM((nv, d, d), F32),
            pltpu.VMEM((n_chunks, nv, 2 * GDN_CHUNK, d), BF16),
            pltpu.VMEM((n_chunks, nv, GDN_CHUNK, d), F32),
            pltpu.VMEM((n_chunks, nv, GDN_CHUNK, d), F32),
            pltpu.VMEM((n_chunks, nv, GDN_CHUNK, d), BF16),
            pltpu.VMEM((n_chunks, nv, 8, d), F32),
        ],
        compiler_params=_params("parallel", "parallel"),
        name="gdn_core",
    )(qkvz, qkvz, qkvz, qkvz, conv_w, conv_w, conv_w, gcol, grow, pcol, prow, norm_w.reshape(1, d).astype(F32))
    return out.reshape(m, 2 * hk * d)


def _ssd_kernel(z_ref, x_ref, b_ref, c_ref, wx_ref, wb_ref, wc_ref, bx_ref, bb_ref, bc_ref, dcol_ref, drow_ref,
                pcol_ref, prow_ref, nw_ref, o_ref, s_ref, *, n_chunks):
    c_len = SSM_CHUNK
    hp = SSM_HEAD_DIM
    nh = SSM_HPG
    width = nh * hp
    s_ref[...] = jnp.zeros_like(s_ref)

    wx, wb, wc = wx_ref[...], wb_ref[...], wc_ref[...]
    bx, bb, bc = bx_ref[...], bb_ref[...], bc_ref[...]
    neg_a_col = -jnp.exp(pcol_ref[0:1, :])
    dtb_col = pcol_ref[1:2, :]
    d_skip = pcol_ref[2:3, :]
    neg_a_row = -jnp.exp(prow_ref[:, 0:1])
    dtb_row = prow_ref[:, 1:2]
    lower = _tri(c_len)
    lower_f = lower.astype(F32)
    upper_f = (lax.broadcasted_iota(jnp.int32, (c_len, c_len), 0)
               <= lax.broadcasted_iota(jnp.int32, (c_len, c_len), 1)).astype(F32)
    lane = lax.broadcasted_iota(jnp.int32, (1, 2 * hp), 1)

    def spread(cols):
        return jnp.concatenate([jnp.where(lane < hp, cols[:, 2 * t:2 * t + 1], cols[:, 2 * t + 1:2 * t + 2])
                                for t in range(nh // 2)], axis=1)

    d_skip_x = spread(d_skip)
    norm_w = nw_ref[...]

    def load_chunk(c):
        r0 = pl.multiple_of(c * c_len, c_len)
        valid_col = (r0 + lax.broadcasted_iota(jnp.int32, (c_len, 1), 0)) >= PAD
        valid_row = (r0 + lax.broadcasted_iota(jnp.int32, (1, c_len), 1)) >= PAD
        dt_col = jnp.where(valid_col, _softplus(dcol_ref[c] + dtb_col), 0.0)
        dt_row = jnp.where(valid_row, _softplus(drow_ref[c] + dtb_row), 0.0)
        xs = jnp.where(valid_col, _conv_silu(x_ref, wx, r0, c_len, bx), 0.0)
        return dict(
            r0=r0, valid_col=valid_col, xs=xs, xdt=xs * spread(dt_col),
            bm=jnp.where(valid_col, _conv_silu(b_ref, wb, r0, c_len, bb), 0.0),
            cm=jnp.where(valid_col, _conv_silu(c_ref, wc, r0, c_len, bc), 0.0),
            cum_col=_hdot(lower_f, dt_col * neg_a_col),
            cum_row=_hdot(dt_row * neg_a_row, upper_f),
        )

    def run_chunks(cs):
        data = [load_chunk(c) for c in cs]
        for ch in data:
            ch["cum_last"] = ch["cum_col"][c_len - 1:c_len, :]
            ch["cb"] = _bdot(ch["cm"], ch["bm"], NT)
        for ch in data:
            ch["upd"] = _bdot(ch["bm"], ch["xdt"] * spread(jnp.exp(ch["cum_last"] - ch["cum_col"])), TN)
        s = s_ref[...]
        for ch in data:
            ch["y"] = _bdot(ch["cm"], s) * spread(jnp.exp(ch["cum_col"]))
            s = s * spread(jnp.exp(ch["cum_last"])) + ch["upd"]
        s_ref[...] = s
        for ch in data:
            diag = []
            for pair in range(nh // 2):
                sc = []
                for j in (2 * pair, 2 * pair + 1):
                    lmat = jnp.exp(jnp.where(lower, ch["cum_col"][:, j:j + 1] - ch["cum_row"][j:j + 1, :], -jnp.inf))
                    sc.append((ch["cb"] * lmat).astype(BF16))
                xp = ch["xdt"][:, pair * 2 * hp:(pair + 1) * 2 * hp]
                rhs = jnp.concatenate([jnp.where(lane < hp, xp, 0.0), jnp.where(lane >= hp, xp, 0.0)], axis=0)
                diag.append(_bdot(jnp.concatenate(sc, axis=1), rhs))
            ch["diag"] = jnp.concatenate(diag, axis=1)
        for ch in data:
            y = ch["y"] + ch["diag"] + ch["xs"] * d_skip_x
            y = y * _silu(z_ref[pl.ds(ch["r0"], c_len), :].astype(F32))
            y = y * lax.rsqrt(jnp.mean(y * y, axis=-1, keepdims=True) + EPS) * norm_w
            o_ref[pl.ds(ch["r0"], c_len), :] = jnp.where(ch["valid_col"], y, 0.0).astype(o_ref.dtype)

    first = n_chunks % SSM_UNROLL
    for c in range(first):
        run_chunks([c])

    def step(it, _):
        run_chunks([first + it * SSM_UNROLL + j for j in range(SSM_UNROLL)])
        return 0

    lax.fori_loop(0, (n_chunks - first) // SSM_UNROLL, step, 0)


def ssd_core(zx, dt_raw, conv_w, conv_b, a_log, dt_bias, d_skip, norm_w, batch):
    m = zx.shape[0]
    lp = m // batch
    ng = 8
    width = SSM_HPG * SSM_HEAD_DIM
    d_inner = ng * width
    n_chunks = lp // SSM_CHUNK
    zx = zx.reshape(batch, lp, zx.shape[1])
    d5 = dt_raw.reshape(batch, n_chunks, SSM_CHUNK, ng, SSM_HPG)
    dcol = jnp.transpose(d5, (0, 3, 1, 2, 4))
    drow = jnp.transpose(d5, (0, 3, 1, 4, 2))
    p3 = jnp.stack([a_log.reshape(ng, SSM_HPG), dt_bias.reshape(ng, SSM_HPG), d_skip.reshape(ng, SSM_HPG)],
                   axis=1).astype(F32)
    prow = jnp.transpose(p3, (0, 2, 1))
    conv_b = conv_b.reshape(1, -1).astype(F32)
    xo, bo, co = d_inner // width, d_inner // SSM_STATE, (d_inner + ng * SSM_STATE) // SSM_STATE
    kern = functools.partial(_ssd_kernel, n_chunks=n_chunks)
    out = pl.pallas_call(
        kern,
        out_shape=jax.ShapeDtypeStruct((batch, lp, d_inner), BF16),
        grid=(batch, ng),
        in_specs=[
            pl.BlockSpec((None, lp, width), lambda b, g: (b, 0, g)),
            pl.BlockSpec((None, lp, width), lambda b, g: (b, 0, xo + g)),
            pl.BlockSpec((None, lp, SSM_STATE), lambda b, g: (b, 0, 2 * bo + g)),
            pl.BlockSpec((None, lp, SSM_STATE), lambda b, g: (b, 0, bo + co + g)),
            pl.BlockSpec((CONV_W, width), lambda b, g: (0, g)),
            pl.BlockSpec((CONV_W, SSM_STATE), lambda b, g: (0, bo + g)),
            pl.BlockSpec((CONV_W, SSM_STATE), lambda b, g: (0, co + g)),
            pl.BlockSpec((1, width), lambda b, g: (0, g)),
            pl.BlockSpec((1, SSM_STATE), lambda b, g: (0, bo + g)),
            pl.BlockSpec((1, SSM_STATE), lambda b, g: (0, co + g)),
            pl.BlockSpec((None, None, n_chunks, SSM_CHUNK, SSM_HPG), lambda b, g: (b, g, 0, 0, 0)),
            pl.BlockSpec((None, None, n_chunks, SSM_HPG, SSM_CHUNK), lambda b, g: (b, g, 0, 0, 0)),
            pl.BlockSpec((None, 3, SSM_HPG), lambda b, g: (g, 0, 0)),
            pl.BlockSpec((None, SSM_HPG, 3), lambda b, g: (g, 0, 0)),
            pl.BlockSpec((1, width), lambda b, g: (0, g)),
        ],
        out_specs=pl.BlockSpec((None, lp, width), lambda b, g: (b, 0, g)),
        scratch_shapes=[pltpu.VMEM((SSM_STATE, width), F32)],
        compiler_params=_params("parallel", "parallel"),
        name="ssd_core",
    )(zx, zx, zx, zx, conv_w, conv_w, conv_w, conv_b, conv_b, conv_b, dcol, drow, p3, prow,
      norm_w.reshape(1, d_inner).astype(F32))
    return out.reshape(m, d_inner)


def _rope_tables(lp):
    inv = ROPE_THETA ** (-jnp.arange(0, MLA_ROPE, 2, dtype=F32) / MLA_ROPE)
    pos = jnp.maximum(jnp.arange(lp, dtype=F32) - PAD, 0.0)
    ang = pos[:, None] * inv[None, :]
    return jnp.tile(jnp.cos(ang), (1, 4)), jnp.tile(jnp.sin(ang), (1, 4))


def _pair_head(lane):
    return lax.shift_right_logical(lane, 5) & 1


def _rope_pair(t, cos, sin):
    lane = lax.broadcasted_iota(jnp.int32, (1, t.shape[1]), 1)
    partner = pltpu.roll(t, t.shape[1] // 2, axis=1)
    return t * cos + jnp.where(lane < t.shape[1] // 2, -partner, partner) * sin


def _normed(x_ref, g_ref, xn_ref, sub):
    def body(r, _):
        r0 = pl.multiple_of(r * sub, sub)
        x = x_ref[pl.ds(r0, sub), :].astype(F32)
        ms = jnp.mean(x * x, axis=-1, keepdims=True)
        xn_ref[pl.ds(r0, sub), :] = (x * lax.rsqrt(ms + EPS) * g_ref[...]).astype(BF16)
        return 0

    lax.fori_loop(0, x_ref.shape[0] // sub, body, 0)


def _q_up_kernel(x_ref, g_ref, w_ref, gn_ref, gp_ref, cos_ref, sin_ref, o_ref, xn_ref, y_ref, *, n_heads, scale, sub):
    d = HEAD_DIM
    _normed(x_ref, g_ref, xn_ref, sub)
    y_ref[...] = jnp.dot(xn_ref[...], w_ref[...], preferred_element_type=F32)
    r = lax.broadcasted_iota(jnp.int32, (d, d), 0)
    c = lax.broadcasted_iota(jnp.int32, (d, d), 1)
    same_head = (_pair_head(r) == _pair_head(c)).astype(F32)
    base = n_heads * d

    def body(i, _):
        rows = pl.ds(pl.multiple_of(i * sub, sub), sub)
        cos, sin = cos_ref[rows, :], sin_ref[rows, :]
        for h in range(n_heads):
            cols = slice(h * d, (h + 1) * d)
            x = y_ref[rows, cols]
            y = x * lax.rsqrt(jnp.mean(x * x, axis=-1, keepdims=True) + EPS) * gn_ref[:, cols]
            o_ref[rows, cols] = (y * scale).astype(o_ref.dtype)
        for p in range(n_heads // 2):
            cols = slice(base + p * d, base + (p + 1) * d)
            x = y_ref[rows, cols]
            ms = _hdot(x * x, same_head) * (1.0 / MLA_ROPE)
            y = x * lax.rsqrt(ms + EPS) * gp_ref[:, p * d:(p + 1) * d]
            o_ref[rows, cols] = (_rope_pair(y, cos, sin) * scale).astype(o_ref.dtype)
        return 0

    lax.fori_loop(0, y_ref.shape[0] // sub, body, 0)


def _kv_up_kernel(x_ref, kpe_ref, g_ref, w_ref, gn_ref, gp_ref, cos_ref, sin_ref, k_ref, v_ref, pe_ref, xn_ref, y_ref,
                  *, n_heads, sub):
    d = HEAD_DIM
    _normed(x_ref, g_ref, xn_ref, sub)
    y_ref[...] = jnp.dot(xn_ref[...], w_ref[:, :n_heads * d], preferred_element_type=F32)
    v_ref[...] = jnp.dot(xn_ref[...], w_ref[:, n_heads * d:], preferred_element_type=F32).astype(v_ref.dtype)

    def body(i, _):
        rows = pl.ds(pl.multiple_of(i * sub, sub), sub)
        for h in range(n_heads):
            cols = slice(h * d, (h + 1) * d)
            x = y_ref[rows, cols]
            y = x * lax.rsqrt(jnp.mean(x * x, axis=-1, keepdims=True) + EPS) * gn_ref[:, cols]
            k_ref[rows, cols] = y.astype(k_ref.dtype)
        x = kpe_ref[rows, :]
        y = x * lax.rsqrt(jnp.mean(x * x, axis=-1, keepdims=True) + EPS) * gp_ref[...]
        pe_ref[rows, :] = _rope_pair(y, cos_ref[rows, :], sin_ref[rows, :]).astype(pe_ref.dtype)
        return 0

    lax.fori_loop(0, y_ref.shape[0] // sub, body, 0)


def _flash_kernel(qn_ref, qpe_ref, kn_ref, kpe_ref, v_ref, o_ref, *, kv_blk):
    d = HEAD_DIM
    blk = qn_ref.shape[0]
    lp = kn_ref.shape[0]
    i = pl.program_id(2)
    lane = lax.broadcasted_iota(jnp.int32, (1, d), 1)
    qpe = qpe_ref[...]
    qf = []
    for hh in range(2):
        mine = _pair_head(lane) == hh
        qf.append(jnp.concatenate([qn_ref[:, hh * d:(hh + 1) * d], jnp.where(mine, qpe, jnp.zeros_like(qpe))],
                                  axis=1))
    qpos = i * blk + lax.broadcasted_iota(jnp.int32, (blk, 1), 0)
    q_limit = jnp.maximum(qpos, PAD)

    def body(j, carry):
        c0 = pl.multiple_of(jnp.minimum(PAD + j * kv_blk, lp - kv_blk), HALO)
        kpe = kpe_ref[pl.ds(c0, kv_blk), :]
        scores = []
        for hh in range(2):
            kf = jnp.concatenate([kn_ref[pl.ds(c0, kv_blk), hh * d:(hh + 1) * d], kpe], axis=1)
            scores.append(lax.dot_general(qf[hh], kf, NT, preferred_element_type=F32))
        kpos = c0 + lax.broadcasted_iota(jnp.int32, (1, kv_blk), 1)
        ok = (kpos >= PAD + j * kv_blk) & (kpos <= q_limit)
        scores = [jnp.where(ok, s, -1e30) for s in scores]
        stats = []
        for hh in range(2):
            m, l, _ = carry[hh]
            s = scores[hh]
            m_new = jnp.maximum(m, jnp.max(s, axis=-1, keepdims=True))
            p = jnp.exp(s - m_new)
            alpha = jnp.exp(m - m_new)
            stats.append((m_new, alpha * l + jnp.sum(p, axis=-1, keepdims=True), alpha, p.astype(BF16)))
        new = []
        for hh in range(2):
            m_new, l, alpha, p = stats[hh]
            pv = jnp.dot(p, v_ref[pl.ds(c0, kv_blk), hh * d:(hh + 1) * d], preferred_element_type=F32)
            new.append((m_new, l, alpha * carry[hh][2] + pv))
        return tuple(new)

    init = tuple((jnp.full((blk, 1), -1e30, F32), jnp.zeros((blk, 1), F32), jnp.zeros((blk, d), F32))
                 for _ in range(2))
    n_kv = lax.div(i * blk + blk - PAD + kv_blk - 1, kv_blk)
    res = lax.fori_loop(0, n_kv, body, init)
    out = jnp.concatenate([acc / l for (_, l, acc) in res], axis=1)
    o_ref[...] = jnp.where(qpos >= PAD, out, 0.0).astype(o_ref.dtype)


def mla_attention(lat, q_col_block, kv_col_block, kpe_col_block, norm_q_lat, norm_kv_lat, wq, wkv, q_norm, k_norm, batch):
    m = lat.shape[0]
    lp = m // batch
    nh = 16
    d = HEAD_DIM
    half = MLA_ROPE // 2
    scale = (d + MLA_ROPE) ** -0.5
    cos, sin = _rope_tables(lp)
    qn_gain = jnp.tile(q_norm[:d], nh).reshape(1, nh * d).astype(F32)
    kn_gain = jnp.tile(k_norm[:d], nh).reshape(1, nh * d).astype(F32)
    qf, qs = q_norm[d:d + half], q_norm[d + half:]
    kf, ks = k_norm[d:d + half], k_norm[d + half:]
    qp_gain = jnp.tile(jnp.concatenate([qf, qf, qs, qs]), nh // 2).reshape(1, nh // 2 * d).astype(F32)
    kp_gain = jnp.concatenate([kf, kf, ks, ks]).reshape(1, d).astype(F32)
    tm = next(t for t in range(min(lp, MAX_ROW_TILE) // 64 * 64, 0, -64) if lp % t == 0)
    per_seq = lp // tm
    q_lora, qw = wq.shape
    kv_lora, kvw = wkv.shape
    whole = lambda i: (0, 0)
    pos = lambda i: (i % per_seq, 0)

    q_prep = pl.pallas_call(
        functools.partial(_q_up_kernel, n_heads=nh, scale=scale, sub=64),
        out_shape=jax.ShapeDtypeStruct((m, qw), BF16),
        grid=(m // tm,),
        in_specs=[
            pl.BlockSpec((tm, q_lora), lambda i: (i, q_col_block)),
            pl.BlockSpec((1, q_lora), whole),
            pl.BlockSpec((q_lora, qw), whole),
            pl.BlockSpec((1, nh * d), whole),
            pl.BlockSpec((1, nh // 2 * d), whole),
            pl.BlockSpec((tm, d), pos),
            pl.BlockSpec((tm, d), pos),
        ],
        out_specs=pl.BlockSpec((tm, qw), lambda i: (i, 0)),
        scratch_shapes=[pltpu.VMEM((tm, q_lora), BF16), pltpu.VMEM((tm, qw), F32)],
        compiler_params=_params("parallel"),
        name="mla_q_up",
    )(lat, norm_q_lat.reshape(1, q_lora).astype(F32), wq, qn_gain, qp_gain, cos, sin).reshape(batch, lp, qw)

    k_prep, v_prep, pe_prep = pl.pallas_call(
        functools.partial(_kv_up_kernel, n_heads=nh, sub=64),
        out_shape=(jax.ShapeDtypeStruct((m, nh * d), BF16),
                   jax.ShapeDtypeStruct((m, nh * d), BF16),
                   jax.ShapeDtypeStruct((m, d), BF16)),
        grid=(m // tm,),
        in_specs=[
            pl.BlockSpec((tm, kv_lora), lambda i: (i, kv_col_block)),
            pl.BlockSpec((tm, d), lambda i: (i, kpe_col_block)),
            pl.BlockSpec((1, kv_lora), whole),
            pl.BlockSpec((kv_lora, kvw), whole),
            pl.BlockSpec((1, nh * d), whole),
            pl.BlockSpec((1, d), whole),
            pl.BlockSpec((tm, d), pos),
            pl.BlockSpec((tm, d), pos),
        ],
        out_specs=(pl.BlockSpec((tm, nh * d), lambda i: (i, 0)),
                   pl.BlockSpec((tm, nh * d), lambda i: (i, 0)),
                   pl.BlockSpec((tm, d), lambda i: (i, 0))),
        scratch_shapes=[pltpu.VMEM((tm, kv_lora), BF16), pltpu.VMEM((tm, nh * d), F32)],
        compiler_params=_params("parallel"),
        name="mla_kv_up",
    )(lat, lat, norm_kv_lat.reshape(1, kv_lora).astype(F32), wkv, kn_gain, kp_gain, cos, sin)
    k_prep, v_prep, pe_prep = (t.reshape(batch, lp, t.shape[1]) for t in (k_prep, v_prep, pe_prep))

    np_ = nh // 2
    tq = next(t for t in range(min(lp, ATT_MAX_Q_ROWS) // 16 * 16, 0, -16) if lp % t == 0)
    out = pl.pallas_call(
        functools.partial(_flash_kernel, kv_blk=min(ATT_KV_BLK, lp)),
        out_shape=jax.ShapeDtypeStruct((batch, lp, nh * d), BF16),
        grid=(batch, np_, lp // tq),
        in_specs=[
            pl.BlockSpec((None, tq, 2 * d), lambda b, p, i: (b, i, p)),
            pl.BlockSpec((None, tq, d), lambda b, p, i: (b, i, nh + p)),
            pl.BlockSpec((None, lp, 2 * d), lambda b, p, i: (b, 0, p)),
            pl.BlockSpec((None, lp, d), lambda b, p, i: (b, 0, 0)),
            pl.BlockSpec((None, lp, 2 * d), lambda b, p, i: (b, 0, p)),
        ],
        out_specs=pl.BlockSpec((None, tq, 2 * d), lambda b, p, i: (b, i, p)),
        compiler_params=_params("parallel", "parallel", "arbitrary"),
        name="mla_flash",
    )(q_prep, q_prep, k_prep, pe_prep, v_prep)
    return out.reshape(m, nh * d)


def _gdn_layer(stream, next_gain, w_in_all, layer, conv_w, a_log, dt_bias, norm_w, w_out_all, batch):
    h, hb, ss = stream
    n_gate = 2 * a_log.shape[0]
    n_big = w_in_all.shape[2] - n_gate
    qkvz = scaled_matmul(hb, ss, w_in_all, layer=layer, n=n_big, name="gdn_in_proj")
    ba = scaled_matmul(hb, ss, w_in_all[layer, :, n_big:], out_dtype=F32, name="gdn_gate_proj")
    o = gdn_core(qkvz, ba, conv_w.astype(F32), a_log, dt_bias, norm_w, batch)
    return matmul_residual(o, w_out_all, h, next_gain, layer=layer, name="gdn_out_proj")


def _mla_layer(stream, next_gain, w_in, norm_q_lat, norm_kv_lat, w_uq, w_ukv, q_norm, k_norm, w_out_all, layer, batch):
    h, hb, ss = stream
    nh, d, r = 16, HEAD_DIM, MLA_ROPE
    q_lora, kv_lora = norm_q_lat.shape[0], norm_kv_lat.shape[0]
    half = r // 2
    w_cq, w_ckv, w_pe = w_in[:, :q_lora], w_in[:, q_lora:q_lora + kv_lora], w_in[:, q_lora + kv_lora:]
    w_pe_t = jnp.concatenate([w_pe[:, :half], w_pe[:, :half], w_pe[:, half:], w_pe[:, half:]], axis=1)
    gap = (-(kv_lora + d)) % q_lora
    w_lat = jnp.concatenate([w_ckv, w_pe_t, jnp.zeros((w_in.shape[0], gap), w_in.dtype), w_cq], axis=1)
    lat = scaled_matmul(hb, ss, w_lat, out_dtype=F32, name="mla_in_proj")
    wq = w_uq.reshape(q_lora, nh, d + r)
    wq_pe = wq[:, :, d:].reshape(q_lora, nh // 2, 2, 2, half)
    wq_pe = jnp.transpose(wq_pe, (0, 1, 3, 2, 4)).reshape(q_lora, nh // 2 * d)
    wq_all = jnp.concatenate([wq[:, :, :d].reshape(q_lora, nh * d), wq_pe], axis=1)
    wkv = w_ukv.reshape(kv_lora, nh, 2 * d)
    wkv_all = jnp.concatenate([wkv[:, :, :d].reshape(kv_lora, nh * d), wkv[:, :, d:].reshape(kv_lora, nh * d)],
                              axis=1)
    o = mla_attention(lat, (kv_lora + d + gap) // q_lora, 0, kv_lora // d, norm_q_lat, norm_kv_lat,
                      wq_all.astype(BF16), wkv_all.astype(BF16), q_norm, k_norm, batch)
    return matmul_residual(o, w_out_all, h, next_gain, layer=layer, name="mla_out_proj")


def _ssm_layer(stream, next_gain, w_in_all, layer, conv_w, conv_b, a_log, dt_bias, d_skip, norm_w, w_out_all, batch):
    h, hb, ss = stream
    n_dt = a_log.shape[0]
    n_big = w_in_all.shape[2] - n_dt
    zx = scaled_matmul(hb, ss, w_in_all, layer=layer, n=n_big, name="ssm_in_proj")
    dt_raw = scaled_matmul(hb, ss, w_in_all[layer, :, n_big:], out_dtype=F32, name="ssm_dt_proj")
    y = ssd_core(zx, dt_raw, conv_w.astype(F32), conv_b, a_log, dt_bias, d_skip, norm_w, batch)
    return matmul_residual(y, w_out_all, h, next_gain, layer=layer, name="ssm_out_proj")


def _mlp_layer(stream, next_gain, w_up_all, w_down_all, layer):
    h, hb, ss = stream
    act = scaled_matmul(hb, ss, w_up_all, layer=layer, act="relu2", name="mlp_up")
    return matmul_residual(act, w_down_all, h, next_gain, layer=layer, name="mlp_down")


def kernel(x, meta_tokens, norm_mix, norm_mlp, mlp_w_up, mlp_w_down, gdn_w_in, gdn_conv_w, gdn_a_log, gdn_dt_bias, gdn_norm, gdn_w_out, mla_w_in, mla_norm_q_lat, mla_norm_kv_lat, mla_w_uq, mla_w_ukv, mla_q_norm, mla_k_norm, mla_w_out, ssm_w_in, ssm_conv_w, ssm_conv_b, ssm_a_log, ssm_dt_bias, ssm_d, ssm_norm, ssm_w_out):
    batch, seq, dm = x.shape
    depth = norm_mix.shape[0]
    lp = ROW_TILE + seq
    meta = jnp.broadcast_to(meta_tokens.astype(x.dtype)[None], (batch, N_META, dm))
    h = jnp.concatenate([jnp.zeros((batch, PAD, dm), x.dtype), meta, x], axis=1).reshape(batch * lp, dm)
    stream = (h,) + tuple(stream_entry(h, norm_mix[0]))
    down_b, gdn_out_b, ssm_out_b, mla_out_b = (w.astype(BF16) for w in (mlp_w_down, gdn_w_out, ssm_w_out, mla_w_out))
    gdn_in_b, ssm_in_b = gdn_w_in.astype(BF16), ssm_w_in.astype(BF16)
    ia = ib = ic = 0
    for i in range(depth):
        kind = i % 3
        if kind == 0:
            stream = _gdn_layer(stream, norm_mlp[i], gdn_in_b, ia, gdn_conv_w[ia], gdn_a_log[ia], gdn_dt_bias[ia],
                                gdn_norm[ia], gdn_out_b, batch)
            ia += 1
        elif kind == 1:
            stream = _mla_layer(stream, norm_mlp[i], mla_w_in[ib], mla_norm_q_lat[ib], mla_norm_kv_lat[ib],
                                mla_w_uq[ib], mla_w_ukv[ib], mla_q_norm[ib], mla_k_norm[ib], mla_out_b, ib, batch)
            ib += 1
        else:
            stream = _ssm_layer(stream, norm_mlp[i], ssm_in_b, ic, ssm_conv_w[ic], ssm_conv_b[ic], ssm_a_log[ic],
                                ssm_dt_bias[ic], ssm_d[ic], ssm_norm[ic], ssm_out_b, batch)
            ic += 1
        following = norm_mix[i + 1] if i + 1 < depth else jnp.ones_like(norm_mlp[i])
        stream = _mlp_layer(stream, following, mlp_w_up, down_b, i)
    return stream[0].reshape(batch, lp, dm)[:, ROW_TILE:]
```
